```python
import jax, jax.numpy as jnp
from jax import lax
import numpy as np

D_MODEL = 1024
BATCH = 4
SEQ = 8192
DEPTH = 4

CHUNK = 128
FOURIER_W = D_MODEL // 4
N_FGROUPS = 4
ML_W = 3 * D_MODEL // 8
RET_W = D_MODEL - FOURIER_W - ML_W
D_MIX = FOURIER_W + ML_W + RET_W
ML_HEADS = 4
ML_DV = ML_W // ML_HEADS
ML_DK = ML_DV // 2
RET_HEADS = 4
RET_DV = RET_W // RET_HEADS
RET_DK = RET_DV // 2
CONV_W = 3
ROPE_BASE = 10000.0
RET_GAMMA_EXP0 = 5.0
RET_BWD_EXP_OFFSET = 0.5
N_EXPERTS = 32
TOP_K = 4
D_FF = D_MODEL
SWIGLU_ALPHA = 1.702
SWIGLU_LIMIT = 7.0
MOE_BLOCK = 128
DEEPNORM_ALPHA = (2.0 * DEPTH) ** 0.25
DEEPNORM_BETA = (8.0 * DEPTH) ** -0.25
LN_EPS = 1e-5

COL_F = 0
COL_MQK = COL_F + FOURIER_W
COL_MV = COL_MQK + 2 * ML_HEADS * ML_DK
COL_MO = COL_MV + ML_W
COL_MG = COL_MO + ML_W
COL_RQ = COL_MG + 4 * ML_HEADS
COL_RK = COL_RQ + RET_HEADS * RET_DK
COL_RV = COL_RK + RET_HEADS * RET_DK
COL_RG = COL_RV + RET_W
PROJ_W = COL_RG + RET_W

kernel_name = "hybrid_fnet_mlstm_retention_moe_deepnorm"


def _layernorm(x, g, b):
    xf = x.astype(jnp.float32)
    mu = xf.mean(-1, keepdims=True)
    var = jnp.square(xf - mu).mean(-1, keepdims=True)
    return ((xf - mu) * lax.rsqrt(var + LN_EPS) * g + b).astype(x.dtype)


def _head_norm(h, w):
    mu = h.mean(-1, keepdims=True)
    var = jnp.square(h - mu).mean(-1, keepdims=True)
    y = (h - mu) * lax.rsqrt(var + LN_EPS)
    return y.reshape(h.shape[0], h.shape[1], -1) * w


def _to_chunks(t):
    Bb, S, H, d = t.shape
    return t.reshape(Bb, S // CHUNK, CHUNK, H, d).transpose(0, 3, 1, 2, 4)


def _gate_chunks(t):
    Bb, S, H = t.shape
    return t.reshape(Bb, S // CHUNK, CHUNK, H).transpose(0, 3, 1, 2)


def _from_chunks(t):
    Bb, H, N, L, d = t.shape
    return t.transpose(0, 2, 3, 1, 4).reshape(Bb, N * L, H, d)


def _centred_dwconv(u, w):
    pad = CONV_W // 2
    S = u.shape[1]
    up = jnp.pad(u, ((0, 0), (pad, pad), (0, 0)))
    return sum(up[:, j:j + S] * w[j] for j in range(CONV_W))


def _rotary(t):
    S, d = t.shape[1], t.shape[3]
    inv = 1.0 / (ROPE_BASE ** (jnp.arange(0, d, 2, dtype=jnp.float32) / d))
    ang = jnp.arange(S, dtype=jnp.float32)[:, None] * inv[None, :]
    cos = jnp.cos(ang)[None, :, None, :]
    sin = jnp.sin(ang)[None, :, None, :]
    t1, t2 = t[..., 0::2], t[..., 1::2]
    return jnp.stack([t1 * cos - t2 * sin, t1 * sin + t2 * cos], axis=-1).reshape(t.shape)


def _fourier_mixer(u):
    Bb, S, _ = u.shape
    ug = u.astype(jnp.float32).reshape(Bb, S, N_FGROUPS, FOURIER_W // N_FGROUPS)
    y = jnp.fft.fft2(ug, axes=(1, 3), norm="ortho").real
    return y.reshape(Bb, S, FOURIER_W)


def _mlstm_chunkwise(q, k, v, i_pre, f_pre):
    qc, kc, vc = _to_chunks(q), _to_chunks(k), _to_chunks(v)
    ic = _gate_chunks(i_pre)
    a = jnp.cumsum(jax.nn.log_sigmoid(_gate_chunks(f_pre)), axis=-1)
    a_end = a[..., -1]
    w_key = a_end[..., None] - a + ic
    m_chunk = w_key.max(-1)
    p = jnp.exp(w_key - m_chunk[..., None])
    c_chunk = jnp.einsum('bhnsv,bhnsk->bhnvk', vc * p[..., None], kc)
    n_chunk = jnp.einsum('bhnsk,bhns->bhnk', kc, p)

    def step(carry, xs):
        c, n, m = carry
        ae, mc, cc, nc = xs
        m_new = jnp.maximum(ae + m, mc)
        s_prev = jnp.exp(ae + m - m_new)
        s_cur = jnp.exp(mc - m_new)
        c_new = s_prev[..., None, None] * c + s_cur[..., None, None] * cc
        n_new = s_prev[..., None] * n + s_cur[..., None] * nc
        return (c_new, n_new, m_new), (c, n, m)

    Bb, H = qc.shape[:2]
    dk, dv = qc.shape[-1], vc.shape[-1]
    init = (jnp.zeros((Bb, H, dv, dk), jnp.float32), jnp.zeros((Bb, H, dk), jnp.float32),
            jnp.zeros((Bb, H), jnp.float32))
    xs = tuple(jnp.moveaxis(t, 2, 0) for t in (a_end, m_chunk, c_chunk, n_chunk))
    _, (c_prev, n_prev, m_prev) = lax.scan(step, init, xs)
    c_prev = jnp.moveaxis(c_prev, 0, 2)
    n_prev = jnp.moveaxis(n_prev, 0, 2)
    m_prev = jnp.moveaxis(m_prev, 0, 2)

    inter_log = a + m_prev[..., None]
    causal = jnp.tril(jnp.ones((CHUNK, CHUNK), dtype=bool))
    d_log = jnp.where(causal, a[..., :, None] - a[..., None, :] + ic[..., None, :], -jnp.inf)
    m_t = jnp.maximum(inter_log, d_log.max(-1))
    scale_inter = jnp.exp(inter_log - m_t)
    scores = jnp.einsum('bhntk,bhnsk->bhnts', qc, kc) * jnp.exp(d_log - m_t[..., None])
    num = (jnp.einsum('bhnts,bhnsv->bhntv', scores, vc)
           + scale_inter[..., None] * jnp.einsum('bhnvk,bhntk->bhntv', c_prev, qc))
    den = scores.sum(-1) + scale_inter * jnp.einsum('bhnk,bhntk->bhnt', n_prev, qc)
    h = num / jnp.maximum(jnp.abs(den), jnp.exp(-m_t))[..., None]
    return _from_chunks(h)


def _mlstm_mixer(qk, v, o, gates, norm_w):
    Bb, S, _ = v.shape
    q, k = jnp.split(qk, 2, axis=-1)
    q = q.reshape(Bb, S, ML_HEADS, ML_DK)
    k = k.reshape(Bb, S, ML_HEADS, ML_DK) * ML_DK ** -0.5
    v = v.reshape(Bb, S, ML_HEADS, ML_DV)
    i_f, f_f, i_b, f_b = jnp.split(gates, 4, axis=-1)
    fl = lambda t: jnp.flip(t, axis=1)
    h_fwd = _mlstm_chunkwise(q, k, v, i_f, f_f)
    h_bwd = fl(_mlstm_chunkwise(fl(q), fl(k), fl(v), fl(i_b), fl(f_b)))
    h = jax.nn.sigmoid(o).reshape(Bb, S, ML_HEADS, ML_DV) * (h_fwd + h_bwd)
    return _head_norm(h, norm_w)


def _ret_log_gamma(offset):
    return jnp.log1p(-jnp.exp2(-(RET_GAMMA_EXP0 + offset) - jnp.arange(RET_HEADS, dtype=jnp.float32)))


def _retention_chunkwise(q, k, v, log_gamma, inclusive):
    qc, kc, vc = _to_chunks(q), _to_chunks(k), _to_chunks(v)
    idx = jnp.arange(CHUNK, dtype=jnp.float32)
    lg = log_gamma[:, None]
    diff = idx[:, None] - idx[None, :]
    mask = (diff >= 0) if inclusive else (diff > 0)
    decay = jnp.where(mask[None], jnp.exp(log_gamma[:, None, None] * jnp.maximum(diff, 0.0)[None]), 0.0)
    w_key = jnp.exp(lg * (CHUNK - 1 - idx))
    w_inter = jnp.exp(lg * (idx + 1.0))
    g_chunk = jnp.exp(log_gamma * CHUNK)
    s_chunk = jnp.einsum('bhnsv,bhnsk,hs->bhnvk', vc, kc, w_key)

    def step(state, s_c):
        return g_chunk[None, :, None, None] * state + s_c, state

    Bb, H = qc.shape[:2]
    init = jnp.zeros((Bb, H, vc.shape[-1], kc.shape[-1]), jnp.float32)
    _, s_prev = lax.scan(step, init, jnp.moveaxis(s_chunk, 2, 0))
    s_prev = jnp.moveaxis(s_prev, 0, 2)
    scores = jnp.einsum('bhntk,bhnsk->bhnts', qc, kc) * decay[None, :, None]
    y = (jnp.einsum('bhnts,bhnsv->bhntv', scores, vc)
         + w_inter[None, :, None, :, None] * jnp.einsum('bhnvk,bhntk->bhntv', s_prev, qc))
    return _from_chunks(y)


def _retention_mixer(q, k, v, g, norm_w):
    Bb, S, _ = q.shape
    q = _rotary(q.reshape(Bb, S, RET_HEADS, RET_DK))
    k = _rotary(k.reshape(Bb, S, RET_HEADS, RET_DK)) * RET_DK ** -0.5
    v = v.reshape(Bb, S, RET_HEADS, RET_DV)
    fl = lambda t: jnp.flip(t, axis=1)
    y_fwd = _retention_chunkwise(q, k, v, _ret_log_gamma(0.0), True)
    y_bwd = fl(_retention_chunkwise(fl(q), fl(k), fl(v), _ret_log_gamma(RET_BWD_EXP_OFFSET), False))
    return jax.nn.silu(g) * _head_norm(y_fwd + y_bwd, norm_w)


def _moe(h, w_router, b_router, w1, b1, w2, b2):
    Bb, S, D = h.shape
    T = Bb * S
    xt = h.reshape(T, D)
    logits = (xt @ w_router + b_router).astype(jnp.float32)
    top_v, top_i = lax.top_k(logits, TOP_K)
    gates = jax.nn.softmax(top_v, axis=-1)
    N = T * TOP_K
    e_flat = top_i.reshape(N)
    tok_flat = jnp.arange(N, dtype=jnp.int32) // TOP_K
    order = jnp.argsort(e_flat)
    e_sorted, tok_sorted = e_flat[order], tok_flat[order]
    g_sorted = gates.reshape(N)[order]
    sizes = jnp.bincount(e_flat, length=N_EXPERTS)
    starts = jnp.cumsum(sizes) - sizes
    psizes = (sizes + MOE_BLOCK - 1) // MOE_BLOCK * MOE_BLOCK
    pends = jnp.cumsum(psizes)
    pstarts = pends - psizes
    pos = pstarts[e_sorted] + jnp.arange(N, dtype=jnp.int32) - starts[e_sorted]
    P = N + N_EXPERTS * MOE_BLOCK
    NB = P // MOE_BLOCK
    src_tok = jnp.zeros((P,), jnp.int32).at[pos].set(tok_sorted)
    gate_pad = jnp.zeros((P,), jnp.float32).at[pos].set(g_sorted)
    block_e = jnp.minimum(jnp.searchsorted(pends, jnp.arange(NB, dtype=jnp.int32) * MOE_BLOCK, side='right'),
                          N_EXPERTS - 1)
    x_pad = xt[src_tok].reshape(NB, MOE_BLOCK, D)

    def expert_block(args):
        xb, e = args
        hc = xb @ w1[e] + b1[e]
        gate, up = hc[:, :D_FF], hc[:, D_FF:]
        gate = jnp.minimum(gate, SWIGLU_LIMIT)
        up = jnp.clip(up, -SWIGLU_LIMIT, SWIGLU_LIMIT)
        glu = gate * jax.nn.sigmoid(SWIGLU_ALPHA * gate)
        return ((up + 1.0) * glu) @ w2[e] + b2[e]

    y_pad = lax.map(expert_block, (x_pad, block_e)).reshape(P, D)
    y = jnp.zeros((T, D), h.dtype).at[src_tok].add(y_pad * gate_pad[:, None].astype(h.dtype))
    return y.reshape(Bb, S, D)


def setup_inputs(seed: int = 0) -> dict:
    key = jax.random.key(seed)
    ks = jax.random.split(key, 20)
    f32 = jnp.float32

    def nrm(k, shape, scale):
        return jax.random.normal(k, shape, f32) * scale

    x = nrm(ks[0], (BATCH, SEQ, D_MODEL), 1.0)
    emb_ln_g = 1.0 + nrm(ks[1], (D_MODEL,), 0.01)
    emb_ln_b = nrm(ks[2], (D_MODEL,), 0.01)
    w_in = nrm(ks[3], (DEPTH, D_MODEL, PROJ_W), D_MODEL ** -0.5)
    b_in = nrm(ks[4], (DEPTH, PROJ_W), 0.01)
    f_bias = jnp.linspace(3.0, 6.0, ML_HEADS, dtype=f32)
    b_in = b_in.at[:, COL_MG + ML_HEADS:COL_MG + 2 * ML_HEADS].add(f_bias)
    b_in = b_in.at[:, COL_MG + 3 * ML_HEADS:COL_MG + 4 * ML_HEADS].add(f_bias)
    conv_w = nrm(ks[5], (DEPTH, CONV_W, 2 * ML_HEADS * ML_DK), CONV_W ** -0.5)
    ml_norm_w = 1.0 + nrm(ks[6], (DEPTH, ML_W), 0.01)
    ret_norm_w = 1.0 + nrm(ks[7], (DEPTH, RET_W), 0.01)
    w_out = nrm(ks[8], (DEPTH, D_MIX, D_MODEL), DEEPNORM_BETA * D_MIX ** -0.5)
    ln1_g = 1.0 + nrm(ks[9], (DEPTH, D_MODEL), 0.01)
    ln1_b = nrm(ks[10], (DEPTH, D_MODEL), 0.01)
    w_router = nrm(ks[11], (DEPTH, D_MODEL, N_EXPERTS), D_MODEL ** -0.5)
    b_router = nrm(ks[12], (DEPTH, N_EXPERTS), 0.01)
    w1 = nrm(ks[13], (DEPTH, N_EXPERTS, D_MODEL, 2 * D_FF), D_MODEL ** -0.5)
    b1 = nrm(ks[14], (DEPTH, N_EXPERTS, 2 * D_FF), 0.01)
    w2 = nrm(ks[15], (DEPTH, N_EXPERTS, D_FF, D_MODEL), DEEPNORM_BETA * D_FF ** -0.5)
    b2 = nrm(ks[16], (DEPTH, N_EXPERTS, D_MODEL), 0.01)
    ln2_g = 1.0 + nrm(ks[17], (DEPTH, D_MODEL), 0.01)
    ln2_b = nrm(ks[18], (DEPTH, D_MODEL), 0.01)
    return {"x": x, "emb_ln_g": emb_ln_g, "emb_ln_b": emb_ln_b, "w_in": w_in, "b_in": b_in,
            "conv_w": conv_w, "ml_norm_w": ml_norm_w, "ret_norm_w": ret_norm_w, "w_out": w_out,
            "ln1_g": ln1_g, "ln1_b": ln1_b, "w_router": w_router, "b_router": b_router,
            "w1": w1, "b1": b1, "w2": w2, "b2": b2, "ln2_g": ln2_g, "ln2_b": ln2_b}


def reference(x, emb_ln_g, emb_ln_b, w_in, b_in, conv_w, ml_norm_w, ret_norm_w, w_out,
              ln1_g, ln1_b, w_router, b_router, w1, b1, w2, b2, ln2_g, ln2_b):
    x = _layernorm(x, emb_ln_g, emb_ln_b)
    for l in range(DEPTH):
        proj = (x @ w_in[l] + b_in[l]).astype(jnp.float32)
        y_f = _fourier_mixer(proj[..., COL_F:COL_MQK])
        qk_m = jax.nn.silu(_centred_dwconv(proj[..., COL_MQK:COL_MV], conv_w[l]))
        y_m = _mlstm_mixer(qk_m, proj[..., COL_MV:COL_MO], proj[..., COL_MO:COL_MG],
                           proj[..., COL_MG:COL_RQ], ml_norm_w[l])
        y_r = _retention_mixer(proj[..., COL_RQ:COL_RK], proj[..., COL_RK:COL_RV],
                               proj[..., COL_RV:COL_RG], proj[..., COL_RG:PROJ_W], ret_norm_w[l])
        mix = jnp.concatenate([y_f, y_m, y_r], axis=-1).astype(x.dtype) @ w_out[l]
        x = _layernorm(DEEPNORM_ALPHA * x + mix, ln1_g[l], ln1_b[l])
        moe = _moe(x, w_router[l], b_router[l], w1[l], b1[l], w2[l], b2[l])
        x = _layernorm(DEEPNORM_ALPHA * x + moe, ln2_g[l], ln2_b[l])
    return x
```

```python
import functools

import jax
import jax.numpy as jnp
from jax import lax
from jax.experimental import pallas as pl
from jax.experimental.pallas import tpu as pltpu

F32 = jnp.float32
BF16 = jnp.bfloat16

D_MODEL = 1024
CHUNK = 128
FOURIER_W = D_MODEL // 4
N_FGROUPS = 4
FG_W = FOURIER_W // N_FGROUPS
ML_W = 3 * D_MODEL // 8
RET_W = D_MODEL - FOURIER_W - ML_W
HEADS = 4
DV = ML_W // HEADS
DK = DV // 2
CONV_W = 3
ROPE_BASE = 10000.0
RET_GAMMA_EXP0 = 5.0
RET_BWD_EXP_OFFSET = 0.5
N_EXPERTS = 32
TOP_K = 4
D_FF = D_MODEL
SWIGLU_ALPHA = 1.702
SWIGLU_LIMIT = 7.0
LN_EPS = 1e-5

COL_F = 0
COL_MQK = COL_F + FOURIER_W
COL_MV = COL_MQK + 2 * HEADS * DK
COL_MO = COL_MV + ML_W
COL_MG = COL_MO + ML_W
COL_RQ = COL_MG + 4 * HEADS
COL_RK = COL_RQ + HEADS * DK
COL_RV = COL_RK + HEADS * DK
COL_RG = COL_RV + RET_W
PROJ_W = COL_RG + RET_W

LANES = 128
SUBLANES = 8
DVP = LANES
QKP = 2 * LANES
HW = HEADS * DVP
DEN_COL = DV
VMEM_LIMIT = 52 * 1024 * 1024

PROJ_GROUPS = (FOURIER_W, 2 * QKP, HW, HW, LANES, 2 * QKP, HW, HW)
MIX_P = FOURIER_W + 2 * HW

ROW_TILE = 512
MOE_BLOCK = 256


def _cparams(sem):
    return pltpu.CompilerParams(dimension_semantics=sem, vmem_limit_bytes=VMEM_LIMIT)


def _pad_last(w, n):
    return jnp.pad(w, [(0, 0)] * (w.ndim - 1) + [(0, n - w.shape[-1])])


def _pad_heads(w):
    lead = w.shape[:-1]
    w = w.reshape(*lead, HEADS, DV)
    w = jnp.pad(w, [(0, 0)] * (len(lead) + 1) + [(0, DVP - DV)])
    return w.reshape(*lead, HW)


def _rot_split(w):
    lead = w.shape[:-1]
    w = w.reshape(*lead, HEADS, DK // 2, 2)
    ev = w[..., 0].reshape(*lead, HEADS * DK // 2)
    od = w[..., 1].reshape(*lead, HEADS * DK // 2)
    return jnp.concatenate([_pad_last(ev, LANES), _pad_last(od, LANES)], axis=-1)


def _layout_proj(w):
    hk = HEADS * DK
    return jnp.concatenate([
        w[..., COL_F:COL_MQK],
        _pad_last(w[..., COL_MQK:COL_MQK + hk], QKP), _pad_last(w[..., COL_MQK + hk:COL_MV], QKP),
        _pad_heads(w[..., COL_MV:COL_MO]), _pad_heads(w[..., COL_MO:COL_MG]),
        _pad_last(w[..., COL_MG:COL_RQ], LANES),
        _rot_split(w[..., COL_RQ:COL_RK]), _rot_split(w[..., COL_RK:COL_RV]),
        _pad_heads(w[..., COL_RV:COL_RG]), _pad_heads(w[..., COL_RG:PROJ_W]),
    ], axis=-1)


def _layout_wout(w):
    wt = jnp.swapaxes(w, -1, -2)
    wt = jnp.concatenate([wt[..., :FOURIER_W], _pad_heads(wt[..., FOURIER_W:FOURIER_W + ML_W]),
                          _pad_heads(wt[..., FOURIER_W + ML_W:])], axis=-1)
    return jnp.swapaxes(wt, -1, -2)


def _split_bf16(w):
    hi = w.astype(BF16)
    return hi, (w - hi.astype(F32)).astype(BF16)


def _dft_tables(seq):
    k = jnp.arange(seq, dtype=jnp.int32)
    kn = (k[:, None] * k[None, :]) % seq
    ang = kn.astype(F32) * (2.0 * jnp.pi / seq)
    scale = seq ** -0.5
    cs = (jnp.cos(ang) * scale).astype(BF16)
    ms = (-jnp.sin(ang) * scale).astype(BF16)
    c = jnp.arange(FG_W, dtype=jnp.int32)
    cc = ((c[:, None] * c[None, :]) % FG_W).astype(F32) * (2.0 * jnp.pi / FG_W)
    eye = jnp.eye(N_FGROUPS, dtype=F32)
    bd_c = jnp.kron(eye, jnp.cos(cc) * FG_W ** -0.5)
    bd_s = jnp.kron(eye, jnp.sin(cc) * FG_W ** -0.5)
    chan = jnp.concatenate([bd_c, bd_s], axis=1).astype(BF16)
    return cs, ms, chan


def _rotary_tables(seq):
    inv = 1.0 / (ROPE_BASE ** (jnp.arange(0, DK, 2, dtype=F32) / DK))
    ang = jnp.arange(seq, dtype=F32)[:, None] * inv[None, :]
    cos = _pad_last(jnp.tile(jnp.cos(ang), (1, HEADS)), LANES)
    sin = _pad_last(jnp.tile(jnp.sin(ang), (1, HEADS)), LANES)
    return cos, sin


def _retention_tables():
    idx = jnp.arange(CHUNK, dtype=F32)
    diff = idx[:, None] - idx[None, :]
    out = []
    for rev in (False, True):
        offset = RET_BWD_EXP_OFFSET if rev else 0.0
        lg = jnp.log1p(-jnp.exp2(-(RET_GAMMA_EXP0 + offset) - jnp.arange(HEADS, dtype=F32)))
        lg3 = lg[:, None, None]
        if rev:
            decay = jnp.where((diff < 0)[None], jnp.exp(lg3 * jnp.maximum(-diff, 0.0)[None]), 0.0)
            w_inter = jnp.exp(lg[:, None] * (CHUNK - idx)[None, :])
            w_key = jnp.exp(lg[:, None] * idx[None, :])
        else:
            decay = jnp.where((diff >= 0)[None], jnp.exp(lg3 * jnp.maximum(diff, 0.0)[None]), 0.0)
            w_inter = jnp.exp(lg[:, None] * (idx + 1.0)[None, :])
            w_key = jnp.exp(lg[:, None] * (CHUNK - 1 - idx)[None, :])
        g_chunk = jnp.exp(lg * CHUNK)
        rep = lambda t: jnp.broadcast_to(t[:, :, None], (HEADS, CHUNK, LANES))
        out.append(jnp.stack([decay, rep(w_inter), rep(w_key),
                              jnp.broadcast_to(g_chunk[:, None, None], (HEADS, CHUNK, LANES))], axis=1))
    return jnp.stack(out, axis=0)


def _layernorm_rows(z, g, b):
    mu = jnp.mean(z, axis=-1, keepdims=True)
    d = z - mu
    var = jnp.mean(d * d, axis=-1, keepdims=True)
    return d * lax.rsqrt(var + LN_EPS) * g + b


def _ln_kernel(x_ref, g_ref, b_ref, o_ref, ob_ref):
    y = _layernorm_rows(x_ref[...], g_ref[...], b_ref[...])
    o_ref[...] = y
    ob_ref[...] = y.astype(BF16)


def _ln(x, g, b):
    t, d = x.shape
    row = pl.BlockSpec((ROW_TILE, d), lambda i: (i, 0))
    vec = pl.BlockSpec((1, d), lambda i: (0, 0))
    return pl.pallas_call(
        _ln_kernel, grid=(t // ROW_TILE,), in_specs=[row, vec, vec], out_specs=[row, row],
        out_shape=[jax.ShapeDtypeStruct((t, d), F32), jax.ShapeDtypeStruct((t, d), BF16)],
        compiler_params=_cparams(("parallel",)), name="input_layernorm",
    )(x, g.reshape(1, d), b.reshape(1, d))


def _ln_res_kernel(alpha, x_ref, y_ref, g_ref, b_ref, o_ref, ob_ref):
    y = _layernorm_rows(alpha * x_ref[...] + y_ref[...], g_ref[...], b_ref[...])
    o_ref[...] = y
    ob_ref[...] = y.astype(BF16)


def _ln_res(x, y, g, b, alpha):
    t, d = x.shape
    row = pl.BlockSpec((ROW_TILE, d), lambda i: (i, 0))
    vec = pl.BlockSpec((1, d), lambda i: (0, 0))
    return pl.pallas_call(
        functools.partial(_ln_res_kernel, alpha), grid=(t // ROW_TILE,),
        in_specs=[row, row, vec, vec], out_specs=[row, row],
        out_shape=[jax.ShapeDtypeStruct((t, d), F32), jax.ShapeDtypeStruct((t, d), BF16)],
        compiler_params=_cparams(("parallel",)), name="residual_layernorm",
    )(x, y, g.reshape(1, d), b.reshape(1, d))


def _inproj_kernel(x_ref, w_ref, b_ref, *o_refs):
    x = x_ref[...]
    c0 = 0
    for o_ref in o_refs:
        n = o_ref.shape[-1]
        o_ref[...] = jnp.dot(x, w_ref[:, c0:c0 + n], preferred_element_type=F32) + b_ref[:, c0:c0 + n]
        c0 += n


def _inproj(xb, w, b):
    t, d = xb.shape
    n_p = w.shape[1]
    return pl.pallas_call(
        _inproj_kernel, grid=(t // ROW_TILE,),
        in_specs=[pl.BlockSpec((ROW_TILE, d), lambda i: (i, 0)),
                  pl.BlockSpec((d, n_p), lambda i: (0, 0)),
                  pl.BlockSpec((1, n_p), lambda i: (0, 0))],
        out_specs=[pl.BlockSpec((ROW_TILE, n), lambda i: (i, 0)) for n in PROJ_GROUPS],
        out_shape=[jax.ShapeDtypeStruct((t, n), F32) for n in PROJ_GROUPS],
        compiler_params=_cparams(("parallel",)), name="input_projection",
    )(xb, w, b)


def _fchan_kernel(u_ref, m_ref, v_ref, w_ref):
    vw = jnp.dot(u_ref[...].astype(BF16), m_ref[...], preferred_element_type=F32)
    v_ref[...] = vw[:, :FOURIER_W].astype(BF16)
    w_ref[...] = vw[:, FOURIER_W:].astype(BF16)


def _fourier_channels(u, chan, batch, seq):
    tiles = seq // ROW_TILE
    out = pl.BlockSpec((ROW_TILE, FOURIER_W), lambda b, i: (i, b))
    return pl.pallas_call(
        _fchan_kernel, grid=(batch, tiles),
        in_specs=[pl.BlockSpec((ROW_TILE, FOURIER_W), lambda b, i: (b * tiles + i, 0)),
                  pl.BlockSpec((FOURIER_W, 2 * FOURIER_W), lambda b, i: (0, 0))],
        out_specs=[out, out],
        out_shape=[jax.ShapeDtypeStruct((seq, batch * FOURIER_W), BF16)] * 2,
        compiler_params=_cparams(("parallel", "parallel")), name="fourier_channel_dft",
    )(u, chan)


def _fseq_kernel(c_ref, s_ref, v_ref, w_ref, o_ref, acc_ref):
    k = pl.program_id(2)

    @pl.when(k == 0)
    def _():
        acc_ref[...] = jnp.zeros_like(acc_ref)

    acc_ref[...] += (jnp.dot(c_ref[...], v_ref[...], preferred_element_type=F32)
                     + jnp.dot(s_ref[...], w_ref[...], preferred_element_type=F32))

    @pl.when(k == pl.num_programs(2) - 1)
    def _():
        o_ref[...] = acc_ref[...]


def _fourier_sequence(cs, ms, zv, zw):
    seq, n = zv.shape
    tm, tn, tk = min(seq, 1024), min(n, 1024), min(seq, 512)
    return pl.pallas_call(
        _fseq_kernel, grid=(seq // tm, n // tn, seq // tk),
        in_specs=[pl.BlockSpec((tm, tk), lambda i, j, k: (i, k)),
                  pl.BlockSpec((tm, tk), lambda i, j, k: (i, k)),
                  pl.BlockSpec((tk, tn), lambda i, j, k: (k, j)),
                  pl.BlockSpec((tk, tn), lambda i, j, k: (k, j))],
        out_specs=pl.BlockSpec((tm, tn), lambda i, j, k: (i, j)),
        out_shape=jax.ShapeDtypeStruct((seq, n), F32),
        scratch_shapes=[pltpu.VMEM((tm, tn), F32)],
        compiler_params=_cparams(("parallel", "parallel", "arbitrary")), name="fourier_sequence_dft",
    )(cs, ms, zv, zw)


def _dot(a, b):
    return jnp.dot(a, b, preferred_element_type=F32)


def _cumsum_dot(tri, x, tri_left):
    hi = x.astype(BF16)
    r1 = x - hi.astype(F32)
    mid = r1.astype(BF16)
    lo = (r1 - mid.astype(F32)).astype(BF16)
    if tri_left:
        return _dot(tri, hi) + _dot(tri, mid) + _dot(tri, lo)
    return _dot(hi, tri) + _dot(mid, tri) + _dot(lo, tri)


def _log_sigmoid(x):
    return jnp.minimum(x, 0.0) - jnp.log1p(jnp.exp(-jnp.abs(x)))


def _mlstm_direction(rev, at_start, at_end, qk, prev8, next8, v, g, convw, c_ref, m_ref, base):
    n = CHUNK
    rows = lax.broadcasted_iota(jnp.int32, (n, 2 * QKP), 0)
    prev_row = prev8[SUBLANES - 1:SUBLANES, :] * jnp.where(at_start, 0.0, 1.0)
    next_row = next8[0:1, :] * jnp.where(at_end, 0.0, 1.0)
    xm1 = jnp.where(rows == 0, prev_row, pltpu.roll(qk, 1, 0))
    xp1 = jnp.where(rows == n - 1, next_row, pltpu.roll(qk, n - 1, 0))
    conv = xm1 * convw[0:1, :] + qk * convw[1:2, :] + xp1 * convw[2:3, :]
    act = conv * jax.nn.sigmoid(conv)
    qb = act[:, :QKP].astype(BF16)
    kt = (act[:, QKP:] * DK ** -0.5).T.astype(BF16)

    r2 = lax.broadcasted_iota(jnp.int32, (n, n), 0)
    c2 = lax.broadcasted_iota(jnp.int32, (n, n), 1)
    lower = jnp.where(c2 <= r2, 1.0, 0.0).astype(BF16)
    upper = jnp.where(c2 >= r2, 1.0, 0.0).astype(BF16)
    lf = _log_sigmoid(g)
    gt = g.T
    lft = _log_sigmoid(gt)
    a_cols = _cumsum_dot(upper if rev else lower, lf, True)
    a_rows = _cumsum_dot(lower if rev else upper, lft, False)
    causal = (c2 >= r2) if rev else (c2 <= r2)
    lane_q = lax.broadcasted_iota(jnp.int32, (1, QKP), 1)
    lane_v = lax.broadcasted_iota(jnp.int32, (1, DVP), 1)

    outs = []
    for h in range(HEADS):
        gi = (2 * HEADS if rev else 0) + h
        gf = gi + HEADS
        a_col = a_cols[:, gf:gf + 1]
        a_row = a_rows[gf:gf + 1, :]
        i_col = g[:, gi:gi + 1]
        i_row = gt[gi:gi + 1, :]
        a_end = a_row[:, 0:1] if rev else a_row[:, n - 1:n]
        m_prev = m_ref[base + h:base + h + 1, 0:1]

        d_log = jnp.where(causal, a_col - a_row + i_row, -jnp.inf)
        inter = a_col + m_prev
        m_t = jnp.maximum(inter, jnp.max(d_log, axis=1, keepdims=True))
        p = jnp.exp(d_log - m_t)
        s_inter = jnp.exp(inter - m_t)
        qm = jnp.where((lane_q >= DK * h) & (lane_q < DK * (h + 1)), qb, jnp.zeros_like(qb))
        scores = _dot(qm, kt) * p
        vh = v[:, DVP * h:DVP * (h + 1)]
        v_aug = jnp.where(lane_v == DEN_COL, 1.0, vh)
        state = c_ref[base + h]
        r = _dot(scores.astype(BF16), v_aug.astype(BF16)) + s_inter * _dot(qm, state.astype(BF16))
        den = r[:, DEN_COL:DEN_COL + 1]
        hid = r / jnp.maximum(jnp.abs(den), jnp.exp(-m_t))
        outs.append(jnp.where(lane_v < DV, hid, 0.0))

        w_key = a_end - a_row + i_row
        m_chunk = jnp.max(w_key, axis=1, keepdims=True)
        m_new = jnp.maximum(a_end + m_prev, m_chunk)
        s_prev = jnp.exp(a_end + m_prev - m_new)
        s_cur = jnp.exp(m_chunk - m_new)
        p_col = jnp.exp(a_end - a_col + i_col - m_chunk) * s_cur
        c_ref[base + h] = s_prev * state + _dot(kt, (p_col * v_aug).astype(BF16))
        m_ref[base + h:base + h + 1, :] = jnp.broadcast_to(m_new, (1, LANES))
    return jnp.concatenate(outs, axis=1)


def _mlstm_kernel(qk_f, qkp_f, qkn_f, v_f, g_f, qk_b, qkp_b, qkn_b, v_b, g_b, convw_ref,
                  hf_ref, hb_ref, c_ref, m_ref):
    i = pl.program_id(1)
    last = pl.num_programs(1) - 1

    @pl.when(i == 0)
    def _():
        c_ref[...] = jnp.zeros_like(c_ref)
        m_ref[...] = jnp.zeros_like(m_ref)

    convw = convw_ref[...]
    hf_ref[0] = _mlstm_direction(False, i == 0, i == last, qk_f[0], qkp_f[0, 0], qkn_f[0, 0], v_f[0], g_f[0],
                                 convw, c_ref, m_ref, 0)
    hb_ref[0] = _mlstm_direction(True, i == last, i == 0, qk_b[0], qkp_b[0, 0], qkn_b[0, 0], v_b[0], g_b[0],
                                 convw, c_ref, m_ref, HEADS)


def _mlstm(qk, v, g, convw, batch, seq):
    n = seq // CHUNK
    g8 = CHUNK // SUBLANES
    qk3 = qk.reshape(batch, seq, 2 * QKP)
    qk8 = qk.reshape(batch, seq // SUBLANES, SUBLANES, 2 * QKP)
    v3 = v.reshape(batch, seq, HW)
    g3 = g.reshape(batch, seq, LANES)
    fwd = lambda b, i: (b, i, 0)
    bwd = lambda b, i: (b, n - 1 - i, 0)

    def halo(chunk_of, delta):
        def index(b, i):
            c = chunk_of(i)
            j = c * g8 - 1 if delta < 0 else (c + 1) * g8
            return (b, jnp.clip(j, 0, seq // SUBLANES - 1), 0, 0)
        return pl.BlockSpec((1, 1, SUBLANES, 2 * QKP), index)

    blk = lambda w, im: pl.BlockSpec((1, CHUNK, w), im)
    c_f = lambda i: i
    c_b = lambda i: n - 1 - i
    return pl.pallas_call(
        _mlstm_kernel, grid=(batch, n),
        in_specs=[blk(2 * QKP, fwd), halo(c_f, -1), halo(c_f, +1), blk(HW, fwd), blk(LANES, fwd),
                  blk(2 * QKP, bwd), halo(c_b, -1), halo(c_b, +1), blk(HW, bwd), blk(LANES, bwd),
                  pl.BlockSpec((CONV_W, 2 * QKP), lambda b, i: (0, 0))],
        out_specs=[blk(HW, fwd), blk(HW, bwd)],
        out_shape=[jax.ShapeDtypeStruct((batch, seq, HW), F32)] * 2,
        scratch_shapes=[pltpu.VMEM((2 * HEADS, QKP, DVP), F32), pltpu.VMEM((2 * HEADS, LANES), F32)],
        compiler_params=_cparams(("parallel", "arbitrary")), name="mlstm_mixer",
    )(qk3, qk8, qk8, v3, g3, qk3, qk8, qk8, v3, g3, convw)


def _rotate(t, cos, sin):
    te, to = t[:, :LANES], t[:, LANES:]
    return jnp.concatenate([te * cos - to * sin, te * sin + to * cos], axis=1)


def _retention_direction(qk, v, cos, sin, tab_ref, d, s_ref, base):
    qb = _rotate(qk[:, :QKP], cos, sin).astype(BF16)
    kt = (_rotate(qk[:, QKP:], cos, sin) * DK ** -0.5).T.astype(BF16)
    lane_q = lax.broadcasted_iota(jnp.int32, (1, QKP), 1) % LANES
    half = DK // 2
    outs = []
    for h in range(HEADS):
        qm = jnp.where((lane_q >= half * h) & (lane_q < half * (h + 1)), qb, jnp.zeros_like(qb))
        scores = _dot(qm, kt) * tab_ref[d, h, 0]
        vh = v[:, DVP * h:DVP * (h + 1)]
        state = s_ref[base + h]
        outs.append(_dot(scores.astype(BF16), vh.astype(BF16))
                    + tab_ref[d, h, 1] * _dot(qm, state.astype(BF16)))
        s_ref[base + h] = (tab_ref[d, h, 3, 0:1, :] * state
                           + _dot(kt, (tab_ref[d, h, 2] * vh).astype(BF16)))
    return jnp.concatenate(outs, axis=1)


def _retention_kernel(qk_f, v_f, cos_f, sin_f, qk_b, v_b, cos_b, sin_b, tab_ref, yf_ref, yb_ref, s_ref):
    @pl.when(pl.program_id(1) == 0)
    def _():
        s_ref[...] = jnp.zeros_like(s_ref)

    yf_ref[0] = _retention_direction(qk_f[0], v_f[0], cos_f[...], sin_f[...], tab_ref, 0, s_ref, 0)
    yb_ref[0] = _retention_direction(qk_b[0], v_b[0], cos_b[...], sin_b[...], tab_ref, 1, s_ref, HEADS)


def _retention(qk, v, cos, sin, tables, batch, seq):
    n = seq // CHUNK
    qk3 = qk.reshape(batch, seq, 2 * QKP)
    v3 = v.reshape(batch, seq, HW)
    fwd = lambda b, i: (b, i, 0)
    bwd = lambda b, i: (b, n - 1 - i, 0)
    blk = lambda w, im: pl.BlockSpec((1, CHUNK, w), im)
    pos_f = pl.BlockSpec((CHUNK, LANES), lambda b, i: (i, 0))
    pos_b = pl.BlockSpec((CHUNK, LANES), lambda b, i: (n - 1 - i, 0))
    return pl.pallas_call(
        _retention_kernel, grid=(batch, n),
        in_specs=[blk(2 * QKP, fwd), blk(HW, fwd), pos_f, pos_f,
                  blk(2 * QKP, bwd), blk(HW, bwd), pos_b, pos_b,
                  pl.BlockSpec(tables.shape, lambda b, i: (0, 0, 0, 0, 0))],
        out_specs=[blk(HW, fwd), blk(HW, bwd)],
        out_shape=[jax.ShapeDtypeStruct((batch, seq, HW), F32)] * 2,
        scratch_shapes=[pltpu.VMEM((2 * HEADS, QKP, DVP), F32)],
        compiler_params=_cparams(("parallel", "arbitrary")), name="retention_mixer",
    )(qk3, v3, cos, sin, qk3, v3, cos, sin, tables)


def _head_norm(z, w):
    valid = lax.broadcasted_iota(jnp.int32, (1, DVP), 1) < DV
    outs = []
    for h in range(HEADS):
        zh = z[:, DVP * h:DVP * (h + 1)]
        mu = jnp.sum(jnp.where(valid, zh, 0.0), axis=1, keepdims=True) * (1.0 / DV)
        d = jnp.where(valid, zh - mu, 0.0)
        var = jnp.sum(d * d, axis=1, keepdims=True) * (1.0 / DV)
        outs.append(d * lax.rsqrt(var + LN_EPS))
    return jnp.concatenate(outs, axis=1) * w


def _outproj_kernel(alpha, x_ref, yf_ref, hf_ref, hb_ref, o_ref, rf_ref, rb_ref, rg_ref, mlw_ref, rtw_ref,
                    w_ref, g_ref, b_ref, wrh_ref, wrl_ref, br_ref, x1_ref, x1b_ref, lg_ref):
    y_m = _head_norm(jax.nn.sigmoid(o_ref[...]) * (hf_ref[...] + hb_ref[...]), mlw_ref[...])
    rg = rg_ref[...]
    y_r = rg * jax.nn.sigmoid(rg) * _head_norm(rf_ref[...] + rb_ref[...], rtw_ref[...])
    cat = jnp.concatenate([yf_ref[...], y_m, y_r], axis=1).astype(BF16)
    mix = _dot(cat, w_ref[...])
    x1 = _layernorm_rows(alpha * x_ref[...] + mix, g_ref[...], b_ref[...])
    x1_ref[...] = x1
    hi = x1.astype(BF16)
    x1b_ref[...] = hi
    lo = (x1 - hi.astype(F32)).astype(BF16)
    lg_ref[...] = _dot(hi, wrh_ref[...]) + _dot(lo, wrh_ref[...]) + _dot(hi, wrl_ref[...]) + br_ref[...]


def _outproj(x, yf, hf, hb, mo, rf, rb, rg, mlw, rtw, w, g, b, wrh, wrl, br, alpha, batch, seq):
    t, d = x.shape
    tiles = seq // ROW_TILE
    row = lambda n: pl.BlockSpec((ROW_TILE, n), lambda i: (i, 0))
    const = lambda a: pl.BlockSpec(a.shape, lambda i: (0, 0))
    args = (x, yf, hf, hb, mo, rf, rb, rg, mlw, rtw, w, g, b, wrh, wrl, br)
    in_specs = [row(d), pl.BlockSpec((ROW_TILE, FOURIER_W), lambda i: (i % tiles, i // tiles)),
                row(HW), row(HW), row(HW), row(HW), row(HW), row(HW)] + [const(a) for a in args[8:]]
    return pl.pallas_call(
        functools.partial(_outproj_kernel, alpha), grid=(t // ROW_TILE,), in_specs=in_specs,
        out_specs=[row(d), row(d), row(LANES)],
        out_shape=[jax.ShapeDtypeStruct((t, d), F32), jax.ShapeDtypeStruct((t, d), BF16),
                   jax.ShapeDtypeStruct((t, LANES), F32)],
        compiler_params=_cparams(("parallel",)), name="mixer_output_projection",
    )(*args)


def _expert_kernel(be_ref, nb_ref, x_ref, w1_ref, b1_ref, w2_ref, b2_ref, o_ref):
    i = pl.program_id(0)

    @pl.when(i < nb_ref[0])
    def _():
        hc = _dot(x_ref[...], w1_ref[0]) + b1_ref[0]
        gate = jnp.minimum(hc[:, :D_FF], SWIGLU_LIMIT)
        up = jnp.clip(hc[:, D_FF:], -SWIGLU_LIMIT, SWIGLU_LIMIT)
        glu = gate * jax.nn.sigmoid(SWIGLU_ALPHA * gate)
        o_ref[...] = _dot(((up + 1.0) * glu).astype(BF16), w2_ref[0]) + b2_ref[0]

    @pl.when(i >= nb_ref[0])
    def _():
        o_ref[...] = jnp.zeros_like(o_ref)


def _experts(block_e, n_blocks, xs, w1, b1, w2, b2, layer):
    p, d = xs.shape
    off = layer * N_EXPERTS
    wmap = lambda i, be, nb: (off + be[i], 0, 0)
    xmap = lambda i, be, nb: (jnp.minimum(i, nb[0] - 1), 0)
    grid_spec = pltpu.PrefetchScalarGridSpec(
        num_scalar_prefetch=2, grid=(p // MOE_BLOCK,),
        in_specs=[pl.BlockSpec((MOE_BLOCK, d), xmap),
                  pl.BlockSpec((1, d, 2 * D_FF), wmap), pl.BlockSpec((1, 1, 2 * D_FF), wmap),
                  pl.BlockSpec((1, D_FF, d), wmap), pl.BlockSpec((1, 1, d), wmap)],
        out_specs=pl.BlockSpec((MOE_BLOCK, d), lambda i, be, nb: (i, 0)))
    return pl.pallas_call(
        _expert_kernel, grid_spec=grid_spec, out_shape=jax.ShapeDtypeStruct((p, d), F32),
        compiler_params=_cparams(("arbitrary",)), name="routed_experts",
    )(block_e, n_blocks, xs, w1, b1, w2, b2)


def _route(logits):
    t = logits.shape[0]
    top_v, top_i = lax.top_k(logits, TOP_K)
    gates = jax.nn.softmax(top_v, axis=-1)
    n = t * TOP_K
    e_flat = top_i.reshape(n)
    onehot = (e_flat[:, None] == jnp.arange(N_EXPERTS, dtype=e_flat.dtype)[None, :]).astype(jnp.int32)
    csum = jnp.cumsum(onehot, axis=0)
    rank = jnp.sum((csum - onehot) * onehot, axis=1)
    sizes = csum[-1]
    psizes = (sizes + MOE_BLOCK - 1) // MOE_BLOCK * MOE_BLOCK
    pends = jnp.cumsum(psizes)
    pstarts = pends - psizes
    pos = jnp.sum(onehot * pstarts[None, :], axis=1) + rank
    p = n + N_EXPERTS * MOE_BLOCK
    nb = p // MOE_BLOCK
    src_tok = jnp.zeros((p,), jnp.int32).at[pos].set(jnp.arange(n, dtype=jnp.int32) // TOP_K,
                                                     unique_indices=True)
    block_e = jnp.minimum(jnp.searchsorted(pends, jnp.arange(nb, dtype=jnp.int32) * MOE_BLOCK, side='right'),
                          N_EXPERTS - 1).astype(jnp.int32)
    n_blocks = (pends[-1] // MOE_BLOCK).astype(jnp.int32).reshape(1)
    return gates, pos.reshape(t, TOP_K), src_tok, block_e, n_blocks


def kernel(x, emb_ln_g, emb_ln_b, w_in, b_in, conv_w, ml_norm_w, ret_norm_w, w_out, ln1_g, ln1_b,
           w_router, b_router, w1, b1, w2, b2, ln2_g, ln2_b):
    batch, seq, d = x.shape
    depth = w_in.shape[0]
    assert d == D_MODEL and seq % ROW_TILE == 0
    t = batch * seq
    alpha = (2.0 * depth) ** 0.25

    w_in_p = _layout_proj(w_in).astype(BF16)
    b_in_p = _layout_proj(b_in)[:, None, :]
    hk = HEADS * DK
    conv_p = jnp.concatenate([_pad_last(conv_w[..., :hk], QKP), _pad_last(conv_w[..., hk:], QKP)], axis=-1)
    mlw_p = _pad_heads(ml_norm_w)[:, None, :]
    rtw_p = _pad_heads(ret_norm_w)[:, None, :]
    w_out_p = _layout_wout(w_out).astype(BF16)
    wr_hi, wr_lo = _split_bf16(_pad_last(w_router, LANES))
    br_p = _pad_last(b_router, LANES)[:, None, :]
    w1_b = w1.astype(BF16).reshape(depth * N_EXPERTS, d, 2 * D_FF)
    w2_b = w2.astype(BF16).reshape(depth * N_EXPERTS, D_FF, d)
    b1_r = b1.reshape(depth * N_EXPERTS, 1, 2 * D_FF)
    b2_r = b2.reshape(depth * N_EXPERTS, 1, d)

    cs, ms, chan = _dft_tables(seq)
    cos, sin = _rotary_tables(seq)
    ret_tab = _retention_tables()

    xf, xb = _ln(x.reshape(t, d), emb_ln_g, emb_ln_b)
    for l in range(depth):
        pf, mqk, mv, mo, mg, rqk, rv, rg = _inproj(xb, w_in_p[l], b_in_p[l])
        zv, zw = _fourier_channels(pf, chan, batch, seq)
        yf = _fourier_sequence(cs, ms, zv, zw)
        hf, hb = _mlstm(mqk, mv, mg, conv_p[l], batch, seq)
        rf, rb = _retention(rqk, rv, cos, sin, ret_tab, batch, seq)
        x1, x1b, logits = _outproj(
            xf, yf, hf.reshape(t, HW), hb.reshape(t, HW), mo, rf.reshape(t, HW), rb.reshape(t, HW), rg,
            mlw_p[l], rtw_p[l], w_out_p[l], ln1_g[l][None, :], ln1_b[l][None, :],
            wr_hi[l], wr_lo[l], br_p[l], alpha, batch, seq)
        gates, pos, src_tok, block_e, n_blocks = _route(logits[:, :N_EXPERTS])
        xs = jnp.take(x1b, src_tok, axis=0)
        ys = _experts(block_e, n_blocks, xs, w1_b, b1_r, w2_b, b2_r, l)
        moe = jnp.sum(jnp.take(ys, pos, axis=0) * gates[:, :, None], axis=1)
        xf, xb = _ln_res(x1, moe, ln2_g[l], ln2_b[l], alpha)
    return xf.reshape(batch, seq, d)
```

```python
import functools

import jax
import jax.numpy as jnp
from jax import lax
from jax.experimental import pallas as pl
from jax.experimental.pallas import tpu as pltpu

F32 = jnp.float32
BF16 = jnp.bfloat16

D_MODEL = 1024
CHUNK = 128
FOURIER_W = D_MODEL // 4
N_FGROUPS = 4
FG_W = FOURIER_W // N_FGROUPS
ML_W = 3 * D_MODEL // 8
RET_W = D_MODEL - FOURIER_W - ML_W
HEADS = 4
DV = ML_W // HEADS
DK = DV // 2
CONV_W = 3
ROPE_BASE = 10000.0
RET_GAMMA_EXP0 = 5.0
RET_BWD_EXP_OFFSET = 0.5
N_EXPERTS = 32
TOP_K = 4
D_FF = D_MODEL
SWIGLU_ALPHA = 1.702
SWIGLU_LIMIT = 7.0
LN_EPS = 1e-5

COL_F = 0
COL_MQK = COL_F + FOURIER_W
COL_MV = COL_MQK + 2 * HEADS * DK
COL_MO = COL_MV + ML_W
COL_MG = COL_MO + ML_W
COL_RQ = COL_MG + 4 * HEADS
COL_RK = COL_RQ + HEADS * DK
COL_RV = COL_RK + HEADS * DK
COL_RG = COL_RV + RET_W
PROJ_W = COL_RG + RET_W

LANES = 128
SUBLANES = 8
DVP = LANES
QKP = 2 * LANES
HW = HEADS * DVP
DEN_COL = DV
VMEM_LIMIT = 52 * 1024 * 1024

PROJ_GROUPS = (FOURIER_W, 2 * QKP, HW, HW, LANES, 2 * QKP, HW, HW)
PROJ_DTYPES = (BF16, F32, BF16, BF16, F32, F32, BF16, BF16)
MIX_P = FOURIER_W + 2 * HW

ROW_TILE = 512
MOE_BLOCK = 256


def _cparams(sem):
    return pltpu.CompilerParams(dimension_semantics=sem, vmem_limit_bytes=VMEM_LIMIT)


def _pad_last(w, n):
    return jnp.pad(w, [(0, 0)] * (w.ndim - 1) + [(0, n - w.shape[-1])])


def _pad_heads(w):
    lead = w.shape[:-1]
    w = w.reshape(*lead, HEADS, DV)
    w = jnp.pad(w, [(0, 0)] * (len(lead) + 1) + [(0, DVP - DV)])
    return w.reshape(*lead, HW)


def _rot_split(w):
    lead = w.shape[:-1]
    w = w.reshape(*lead, HEADS, DK // 2, 2)
    ev = w[..., 0].reshape(*lead, HEADS * DK // 2)
    od = w[..., 1].reshape(*lead, HEADS * DK // 2)
    return jnp.concatenate([_pad_last(ev, LANES), _pad_last(od, LANES)], axis=-1)


def _layout_proj(w):
    hk = HEADS * DK
    return jnp.concatenate([
        w[..., COL_F:COL_MQK],
        _pad_last(w[..., COL_MQK:COL_MQK + hk], QKP), _pad_last(w[..., COL_MQK + hk:COL_MV], QKP),
        _pad_heads(w[..., COL_MV:COL_MO]), _pad_heads(w[..., COL_MO:COL_MG]),
        _pad_last(w[..., COL_MG:COL_RQ], LANES),
        _rot_split(w[..., COL_RQ:COL_RK]), _rot_split(w[..., COL_RK:COL_RV]),
        _pad_heads(w[..., COL_RV:COL_RG]), _pad_heads(w[..., COL_RG:PROJ_W]),
    ], axis=-1)


def _layout_wout(w):
    wt = jnp.swapaxes(w, -1, -2)
    wt = jnp.concatenate([wt[..., :FOURIER_W], _pad_heads(wt[..., FOURIER_W:FOURIER_W + ML_W]),
                          _pad_heads(wt[..., FOURIER_W + ML_W:])], axis=-1)
    return jnp.swapaxes(wt, -1, -2)


def _split_bf16(w):
    hi = w.astype(BF16)
    return hi, (w - hi.astype(F32)).astype(BF16)


def _dft_tables(seq):
    k = jnp.arange(seq, dtype=jnp.int32)
    kn = (k[:, None] * k[None, :]) % seq
    ang = kn.astype(F32) * (2.0 * jnp.pi / seq)
    scale = seq ** -0.5
    cs = (jnp.cos(ang) * scale).astype(BF16)
    ms = (-jnp.sin(ang) * scale).astype(BF16)
    c = jnp.arange(FG_W, dtype=jnp.int32)
    cc = ((c[:, None] * c[None, :]) % FG_W).astype(F32) * (2.0 * jnp.pi / FG_W)
    eye = jnp.eye(N_FGROUPS, dtype=F32)
    bd_c = jnp.kron(eye, jnp.cos(cc) * FG_W ** -0.5)
    bd_s = jnp.kron(eye, jnp.sin(cc) * FG_W ** -0.5)
    chan = jnp.concatenate([bd_c, bd_s], axis=1).astype(BF16)
    return cs, ms, chan


def _rotary_tables(seq):
    inv = 1.0 / (ROPE_BASE ** (jnp.arange(0, DK, 2, dtype=F32) / DK))
    ang = jnp.arange(seq, dtype=F32)[:, None] * inv[None, :]
    cos = _pad_last(jnp.tile(jnp.cos(ang), (1, HEADS)), LANES)
    sin = _pad_last(jnp.tile(jnp.sin(ang), (1, HEADS)), LANES)
    return cos, sin


def _retention_tables():
    idx = jnp.arange(CHUNK, dtype=F32)
    diff = idx[:, None] - idx[None, :]
    out = []
    for rev in (False, True):
        offset = RET_BWD_EXP_OFFSET if rev else 0.0
        lg = jnp.log1p(-jnp.exp2(-(RET_GAMMA_EXP0 + offset) - jnp.arange(HEADS, dtype=F32)))
        lg3 = lg[:, None, None]
        if rev:
            decay = jnp.where((diff < 0)[None], jnp.exp(lg3 * jnp.maximum(-diff, 0.0)[None]), 0.0)
            w_inter = jnp.exp(lg[:, None] * (CHUNK - idx)[None, :])
            w_key = jnp.exp(lg[:, None] * idx[None, :])
        else:
            decay = jnp.where((diff >= 0)[None], jnp.exp(lg3 * jnp.maximum(diff, 0.0)[None]), 0.0)
            w_inter = jnp.exp(lg[:, None] * (idx + 1.0)[None, :])
            w_key = jnp.exp(lg[:, None] * (CHUNK - 1 - idx)[None, :])
        g_chunk = jnp.exp(lg * CHUNK)
        rep = lambda t: jnp.broadcast_to(t[:, :, None], (HEADS, CHUNK, LANES))
        out.append(jnp.stack([decay, rep(w_inter), rep(w_key),
                              jnp.broadcast_to(g_chunk[:, None, None], (HEADS, CHUNK, LANES))], axis=1))
    return jnp.stack(out, axis=0)


def _layernorm_rows(z, g, b):
    mu = jnp.mean(z, axis=-1, keepdims=True)
    d = z - mu
    var = jnp.mean(d * d, axis=-1, keepdims=True)
    return d * lax.rsqrt(var + LN_EPS) * g + b


def _ln_kernel(x_ref, g_ref, b_ref, o_ref, ob_ref):
    y = _layernorm_rows(x_ref[...], g_ref[...], b_ref[...])
    o_ref[...] = y
    ob_ref[...] = y.astype(BF16)


def _ln(x, g, b):
    t, d = x.shape
    row = pl.BlockSpec((ROW_TILE, d), lambda i: (i, 0))
    vec = pl.BlockSpec((1, d), lambda i: (0, 0))
    return pl.pallas_call(
        _ln_kernel, grid=(t // ROW_TILE,), in_specs=[row, vec, vec], out_specs=[row, row],
        out_shape=[jax.ShapeDtypeStruct((t, d), F32), jax.ShapeDtypeStruct((t, d), BF16)],
        compiler_params=_cparams(("parallel",)), name="input_layernorm",
    )(x, g.reshape(1, d), b.reshape(1, d))


def _inproj_kernel(x_ref, w_ref, b_ref, *o_refs):
    x = x_ref[...]
    c0 = 0
    for o_ref in o_refs:
        n = o_ref.shape[-1]
        y = jnp.dot(x, w_ref[:, c0:c0 + n], preferred_element_type=F32) + b_ref[:, c0:c0 + n]
        o_ref[...] = y.astype(o_ref.dtype)
        c0 += n


def _inproj(xb, w, b):
    t, d = xb.shape
    n_p = w.shape[1]
    return pl.pallas_call(
        _inproj_kernel, grid=(t // ROW_TILE,),
        in_specs=[pl.BlockSpec((ROW_TILE, d), lambda i: (i, 0)),
                  pl.BlockSpec((d, n_p), lambda i: (0, 0)),
                  pl.BlockSpec((1, n_p), lambda i: (0, 0))],
        out_specs=[pl.BlockSpec((ROW_TILE, n), lambda i: (i, 0)) for n in PROJ_GROUPS],
        out_shape=[jax.ShapeDtypeStruct((t, n), dt) for n, dt in zip(PROJ_GROUPS, PROJ_DTYPES)],
        compiler_params=_cparams(("parallel",)), name="input_projection",
    )(xb, w, b)


def _fchan_kernel(u_ref, m_ref, v_ref, w_ref):
    vw = jnp.dot(u_ref[...].astype(BF16), m_ref[...], preferred_element_type=F32)
    v_ref[...] = vw[:, :FOURIER_W].astype(BF16)
    w_ref[...] = vw[:, FOURIER_W:].astype(BF16)


def _fourier_channels(u, chan, batch, seq):
    tiles = seq // ROW_TILE
    out = pl.BlockSpec((ROW_TILE, FOURIER_W), lambda b, i: (i, b))
    return pl.pallas_call(
        _fchan_kernel, grid=(batch, tiles),
        in_specs=[pl.BlockSpec((ROW_TILE, FOURIER_W), lambda b, i: (b * tiles + i, 0)),
                  pl.BlockSpec((FOURIER_W, 2 * FOURIER_W), lambda b, i: (0, 0))],
        out_specs=[out, out],
        out_shape=[jax.ShapeDtypeStruct((seq, batch * FOURIER_W), BF16)] * 2,
        compiler_params=_cparams(("parallel", "parallel")), name="fourier_channel_dft",
    )(u, chan)


def _fseq_kernel(c_ref, s_ref, v_ref, w_ref, o_ref, acc_ref):
    k = pl.program_id(2)

    @pl.when(k == 0)
    def _():
        acc_ref[...] = jnp.zeros_like(acc_ref)

    acc_ref[...] += (jnp.dot(c_ref[...], v_ref[...], preferred_element_type=F32)
                     + jnp.dot(s_ref[...], w_ref[...], preferred_element_type=F32))

    @pl.when(k == pl.num_programs(2) - 1)
    def _():
        o_ref[...] = acc_ref[...]


def _fourier_sequence(cs, ms, zv, zw):
    seq, n = zv.shape
    tm, tn, tk = min(seq, 1024), min(n, 1024), min(seq, 512)
    return pl.pallas_call(
        _fseq_kernel, grid=(seq // tm, n // tn, seq // tk),
        in_specs=[pl.BlockSpec((tm, tk), lambda i, j, k: (i, k)),
                  pl.BlockSpec((tm, tk), lambda i, j, k: (i, k)),
                  pl.BlockSpec((tk, tn), lambda i, j, k: (k, j)),
                  pl.BlockSpec((tk, tn), lambda i, j, k: (k, j))],
        out_specs=pl.BlockSpec((tm, tn), lambda i, j, k: (i, j)),
        out_shape=jax.ShapeDtypeStruct((seq, n), F32),
        scratch_shapes=[pltpu.VMEM((tm, tn), F32)],
        compiler_params=_cparams(("parallel", "parallel", "arbitrary")), name="fourier_sequence_dft",
    )(cs, ms, zv, zw)


def _dot(a, b):
    return jnp.dot(a, b, preferred_element_type=F32)


def _cumsum_dot(tri, x, tri_left):
    hi = x.astype(BF16)
    r1 = x - hi.astype(F32)
    mid = r1.astype(BF16)
    lo = (r1 - mid.astype(F32)).astype(BF16)
    if tri_left:
        return _dot(tri, hi) + _dot(tri, mid) + _dot(tri, lo)
    return _dot(hi, tri) + _dot(mid, tri) + _dot(lo, tri)


def _log_sigmoid(x):
    return jnp.minimum(x, 0.0) - jnp.log1p(jnp.exp(-jnp.abs(x)))


def _mlstm_direction(rev, at_start, at_end, qk, prev8, next8, v, g, convw, c_ref, m_ref, base):
    n = CHUNK
    rows = lax.broadcasted_iota(jnp.int32, (n, 2 * QKP), 0)
    prev_row = prev8[SUBLANES - 1:SUBLANES, :] * jnp.where(at_start, 0.0, 1.0)
    next_row = next8[0:1, :] * jnp.where(at_end, 0.0, 1.0)
    xm1 = jnp.where(rows == 0, prev_row, pltpu.roll(qk, 1, 0))
    xp1 = jnp.where(rows == n - 1, next_row, pltpu.roll(qk, n - 1, 0))
    conv = xm1 * convw[0:1, :] + qk * convw[1:2, :] + xp1 * convw[2:3, :]
    act = conv * jax.nn.sigmoid(conv)
    qb = act[:, :QKP].astype(BF16)
    kt = (act[:, QKP:] * DK ** -0.5).T.astype(BF16)

    r2 = lax.broadcasted_iota(jnp.int32, (n, n), 0)
    c2 = lax.broadcasted_iota(jnp.int32, (n, n), 1)
    lower = jnp.where(c2 <= r2, 1.0, 0.0).astype(BF16)
    upper = jnp.where(c2 >= r2, 1.0, 0.0).astype(BF16)
    lf = _log_sigmoid(g)
    gt = g.T
    lft = _log_sigmoid(gt)
    a_cols = _cumsum_dot(upper if rev else lower, lf, True)
    a_rows = _cumsum_dot(lower if rev else upper, lft, False)
    causal = (c2 >= r2) if rev else (c2 <= r2)
    lane_q = lax.broadcasted_iota(jnp.int32, (1, QKP), 1)
    lane_v = lax.broadcasted_iota(jnp.int32, (1, DVP), 1)

    outs = []
    for h in range(HEADS):
        gi = (2 * HEADS if rev else 0) + h
        gf = gi + HEADS
        a_col = a_cols[:, gf:gf + 1]
        a_row = a_rows[gf:gf + 1, :]
        i_col = g[:, gi:gi + 1]
        i_row = gt[gi:gi + 1, :]
        a_end = a_row[:, 0:1] if rev else a_row[:, n - 1:n]
        m_prev = m_ref[base + h:base + h + 1, 0:1]

        d_log = jnp.where(causal, a_col - a_row + i_row, -jnp.inf)
        inter = a_col + m_prev
        m_t = jnp.maximum(inter, jnp.max(d_log, axis=1, keepdims=True))
        p = jnp.exp(d_log - m_t)
        s_inter = jnp.exp(inter - m_t)
        qm = jnp.where((lane_q >= DK * h) & (lane_q < DK * (h + 1)), qb, jnp.zeros_like(qb))
        scores = _dot(qm, kt) * p
        vh = v[:, DVP * h:DVP * (h + 1)]
        v_aug = jnp.where(lane_v == DEN_COL, 1.0, vh)
        state = c_ref[base + h]
        r = _dot(scores.astype(BF16), v_aug.astype(BF16)) + s_inter * _dot(qm, state.astype(BF16))
        den = r[:, DEN_COL:DEN_COL + 1]
        hid = r / jnp.maximum(jnp.abs(den), jnp.exp(-m_t))
        outs.append(jnp.where(lane_v < DV, hid, 0.0))

        w_key = a_end - a_row + i_row
        m_chunk = jnp.max(w_key, axis=1, keepdims=True)
        m_new = jnp.maximum(a_end + m_prev, m_chunk)
        s_prev = jnp.exp(a_end + m_prev - m_new)
        s_cur = jnp.exp(m_chunk - m_new)
        p_col = jnp.exp(a_end - a_col + i_col - m_chunk) * s_cur
        c_ref[base + h] = s_prev * state + _dot(kt, (p_col * v_aug).astype(BF16))
        m_ref[base + h:base + h + 1, :] = jnp.broadcast_to(m_new, (1, LANES))
    return jnp.concatenate(outs, axis=1)


def _mlstm_kernel(qk_f, qkp_f, qkn_f, v_f, g_f, qk_b, qkp_b, qkn_b, v_b, g_b, convw_ref,
                  hf_ref, hb_ref, c_ref, m_ref):
    i = pl.program_id(1)
    last = pl.num_programs(1) - 1

    @pl.when(i == 0)
    def _():
        c_ref[...] = jnp.zeros_like(c_ref)
        m_ref[...] = jnp.zeros_like(m_ref)

    convw = convw_ref[...]
    hf_ref[0] = _mlstm_direction(False, i == 0, i == last, qk_f[0], qkp_f[0, 0], qkn_f[0, 0], v_f[0], g_f[0],
                                 convw, c_ref, m_ref, 0)
    hb_ref[0] = _mlstm_direction(True, i == last, i == 0, qk_b[0], qkp_b[0, 0], qkn_b[0, 0], v_b[0], g_b[0],
                                 convw, c_ref, m_ref, HEADS)


def _mlstm(qk, v, g, convw, batch, seq):
    n = seq // CHUNK
    g8 = CHUNK // SUBLANES
    qk3 = qk.reshape(batch, seq, 2 * QKP)
    qk8 = qk.reshape(batch, seq // SUBLANES, SUBLANES, 2 * QKP)
    v3 = v.reshape(batch, seq, HW)
    g3 = g.reshape(batch, seq, LANES)
    fwd = lambda b, i: (b, i, 0)
    bwd = lambda b, i: (b, n - 1 - i, 0)

    def halo(chunk_of, delta):
        def index(b, i):
            c = chunk_of(i)
            j = c * g8 - 1 if delta < 0 else (c + 1) * g8
            return (b, jnp.clip(j, 0, seq // SUBLANES - 1), 0, 0)
        return pl.BlockSpec((1, 1, SUBLANES, 2 * QKP), index)

    blk = lambda w, im: pl.BlockSpec((1, CHUNK, w), im)
    c_f = lambda i: i
    c_b = lambda i: n - 1 - i
    return pl.pallas_call(
        _mlstm_kernel, grid=(batch, n),
        in_specs=[blk(2 * QKP, fwd), halo(c_f, -1), halo(c_f, +1), blk(HW, fwd), blk(LANES, fwd),
                  blk(2 * QKP, bwd), halo(c_b, -1), halo(c_b, +1), blk(HW, bwd), blk(LANES, bwd),
                  pl.BlockSpec((CONV_W, 2 * QKP), lambda b, i: (0, 0))],
        out_specs=[blk(HW, fwd), blk(HW, bwd)],
        out_shape=[jax.ShapeDtypeStruct((batch, seq, HW), F32)] * 2,
        scratch_shapes=[pltpu.VMEM((2 * HEADS, QKP, DVP), F32), pltpu.VMEM((2 * HEADS, LANES), F32)],
        compiler_params=_cparams(("parallel", "arbitrary")), name="mlstm_mixer",
    )(qk3, qk8, qk8, v3, g3, qk3, qk8, qk8, v3, g3, convw)


def _rotate(t, cos, sin):
    te, to = t[:, :LANES], t[:, LANES:]
    return jnp.concatenate([te * cos - to * sin, te * sin + to * cos], axis=1)


def _retention_direction(qk, v, cos, sin, tab_ref, d, s_ref, base):
    qb = _rotate(qk[:, :QKP], cos, sin).astype(BF16)
    kt = (_rotate(qk[:, QKP:], cos, sin) * DK ** -0.5).T.astype(BF16)
    lane_q = lax.broadcasted_iota(jnp.int32, (1, QKP), 1) % LANES
    half = DK // 2
    outs = []
    for h in range(HEADS):
        qm = jnp.where((lane_q >= half * h) & (lane_q < half * (h + 1)), qb, jnp.zeros_like(qb))
        scores = _dot(qm, kt) * tab_ref[d, h, 0]
        vh = v[:, DVP * h:DVP * (h + 1)]
        state = s_ref[base + h]
        outs.append(_dot(scores.astype(BF16), vh.astype(BF16))
                    + tab_ref[d, h, 1] * _dot(qm, state.astype(BF16)))
        s_ref[base + h] = (tab_ref[d, h, 3, 0:1, :] * state
                           + _dot(kt, (tab_ref[d, h, 2] * vh).astype(BF16)))
    return jnp.concatenate(outs, axis=1)


def _retention_kernel(qk_f, v_f, cos_f, sin_f, qk_b, v_b, cos_b, sin_b, tab_ref, yf_ref, yb_ref, s_ref):
    @pl.when(pl.program_id(1) == 0)
    def _():
        s_ref[...] = jnp.zeros_like(s_ref)

    yf_ref[0] = _retention_direction(qk_f[0], v_f[0], cos_f[...], sin_f[...], tab_ref, 0, s_ref, 0)
    yb_ref[0] = _retention_direction(qk_b[0], v_b[0], cos_b[...], sin_b[...], tab_ref, 1, s_ref, HEADS)


def _retention(qk, v, cos, sin, tables, batch, seq):
    n = seq // CHUNK
    qk3 = qk.reshape(batch, seq, 2 * QKP)
    v3 = v.reshape(batch, seq, HW)
    fwd = lambda b, i: (b, i, 0)
    bwd = lambda b, i: (b, n - 1 - i, 0)
    blk = lambda w, im: pl.BlockSpec((1, CHUNK, w), im)
    pos_f = pl.BlockSpec((CHUNK, LANES), lambda b, i: (i, 0))
    pos_b = pl.BlockSpec((CHUNK, LANES), lambda b, i: (n - 1 - i, 0))
    return pl.pallas_call(
        _retention_kernel, grid=(batch, n),
        in_specs=[blk(2 * QKP, fwd), blk(HW, fwd), pos_f, pos_f,
                  blk(2 * QKP, bwd), blk(HW, bwd), pos_b, pos_b,
                  pl.BlockSpec(tables.shape, lambda b, i: (0, 0, 0, 0, 0))],
        out_specs=[blk(HW, fwd), blk(HW, bwd)],
        out_shape=[jax.ShapeDtypeStruct((batch, seq, HW), F32)] * 2,
        scratch_shapes=[pltpu.VMEM((2 * HEADS, QKP, DVP), F32)],
        compiler_params=_cparams(("parallel", "arbitrary")), name="retention_mixer",
    )(qk3, v3, cos, sin, qk3, v3, cos, sin, tables)


def _head_norm(z, w):
    valid = lax.broadcasted_iota(jnp.int32, (1, DVP), 1) < DV
    outs = []
    for h in range(HEADS):
        zh = z[:, DVP * h:DVP * (h + 1)]
        mu = jnp.sum(jnp.where(valid, zh, 0.0), axis=1, keepdims=True) * (1.0 / DV)
        d = jnp.where(valid, zh - mu, 0.0)
        var = jnp.sum(d * d, axis=1, keepdims=True) * (1.0 / DV)
        outs.append(d * lax.rsqrt(var + LN_EPS))
    return jnp.concatenate(outs, axis=1) * w


def _outproj_kernel(alpha, x_ref, yf_ref, hf_ref, hb_ref, o_ref, rf_ref, rb_ref, rg_ref, mlw_ref, rtw_ref,
                    w_ref, g_ref, b_ref, wrh_ref, wrl_ref, br_ref, x1_ref, x1b_ref, lg_ref):
    y_m = _head_norm(jax.nn.sigmoid(o_ref[...].astype(F32)) * (hf_ref[...] + hb_ref[...]), mlw_ref[...])
    rg = rg_ref[...].astype(F32)
    y_r = rg * jax.nn.sigmoid(rg) * _head_norm(rf_ref[...] + rb_ref[...], rtw_ref[...])
    cat = jnp.concatenate([yf_ref[...], y_m, y_r], axis=1).astype(BF16)
    mix = _dot(cat, w_ref[...])
    x1 = _layernorm_rows(alpha * x_ref[...] + mix, g_ref[...], b_ref[...])
    x1_ref[...] = x1
    hi = x1.astype(BF16)
    x1b_ref[...] = hi
    lo = (x1 - hi.astype(F32)).astype(BF16)
    lg_ref[...] = _dot(hi, wrh_ref[...]) + _dot(lo, wrh_ref[...]) + _dot(hi, wrl_ref[...]) + br_ref[...]


def _outproj(x, yf, hf, hb, mo, rf, rb, rg, mlw, rtw, w, g, b, wrh, wrl, br, alpha, batch, seq):
    t, d = x.shape
    tiles = seq // ROW_TILE
    row = lambda n: pl.BlockSpec((ROW_TILE, n), lambda i: (i, 0))
    const = lambda a: pl.BlockSpec(a.shape, lambda i: (0, 0))
    args = (x, yf, hf, hb, mo, rf, rb, rg, mlw, rtw, w, g, b, wrh, wrl, br)
    in_specs = [row(d), pl.BlockSpec((ROW_TILE, FOURIER_W), lambda i: (i % tiles, i // tiles)),
                row(HW), row(HW), row(HW), row(HW), row(HW), row(HW)] + [const(a) for a in args[8:]]
    return pl.pallas_call(
        functools.partial(_outproj_kernel, alpha), grid=(t // ROW_TILE,), in_specs=in_specs,
        out_specs=[row(d), row(d), row(LANES)],
        out_shape=[jax.ShapeDtypeStruct((t, d), F32), jax.ShapeDtypeStruct((t, d), BF16),
                   jax.ShapeDtypeStruct((t, LANES), F32)],
        compiler_params=_cparams(("parallel",)), name="mixer_output_projection",
    )(*args)


def _expert_kernel(be_ref, nb_ref, x_ref, w1_ref, b1_ref, w2_ref, b2_ref, o_ref, w1b_ref, w2b_ref):
    i = pl.program_id(0)

    @pl.when(jnp.logical_or(i == 0, be_ref[i] != be_ref[jnp.maximum(i - 1, 0)]))
    def _():
        w1b_ref[...] = w1_ref[0].astype(BF16)
        w2b_ref[...] = w2_ref[0].astype(BF16)

    @pl.when(i < nb_ref[0])
    def _():
        hc = _dot(x_ref[...], w1b_ref[...]) + b1_ref[0]
        gate = jnp.minimum(hc[:, :D_FF], SWIGLU_LIMIT)
        up = jnp.clip(hc[:, D_FF:], -SWIGLU_LIMIT, SWIGLU_LIMIT)
        glu = gate * jax.nn.sigmoid(SWIGLU_ALPHA * gate)
        y = _dot(((up + 1.0) * glu).astype(BF16), w2b_ref[...]) + b2_ref[0]
        o_ref[...] = y.astype(o_ref.dtype)

    @pl.when(i >= nb_ref[0])
    def _():
        o_ref[...] = jnp.zeros_like(o_ref)


def _experts(block_e, n_blocks, xs, w1, b1, w2, b2, layer):
    p, d = xs.shape
    off = layer * N_EXPERTS
    wmap = lambda i, be, nb: (off + be[i], 0, 0)
    xmap = lambda i, be, nb: (jnp.minimum(i, nb[0] - 1), 0)
    grid_spec = pltpu.PrefetchScalarGridSpec(
        num_scalar_prefetch=2, grid=(p // MOE_BLOCK,),
        in_specs=[pl.BlockSpec((MOE_BLOCK, d), xmap),
                  pl.BlockSpec((1, d, 2 * D_FF), wmap), pl.BlockSpec((1, 1, 2 * D_FF), wmap),
                  pl.BlockSpec((1, D_FF, d), wmap), pl.BlockSpec((1, 1, d), wmap)],
        out_specs=pl.BlockSpec((MOE_BLOCK, d), lambda i, be, nb: (i, 0)),
        scratch_shapes=[pltpu.VMEM((d, 2 * D_FF), BF16), pltpu.VMEM((D_FF, d), BF16)])
    return pl.pallas_call(
        _expert_kernel, grid_spec=grid_spec, out_shape=jax.ShapeDtypeStruct((p, d), BF16),
        compiler_params=_cparams(("arbitrary",)), name="routed_experts",
    )(block_e, n_blocks, xs, w1, b1, w2, b2)


def _route_kernel(lg_ref, gate_ref, eid_ref, pos_ref, cnt_ref, run_ref, pst_ref):
    ph = pl.program_id(0)
    i = pl.program_id(1)
    tm = lg_ref.shape[0]
    lane = lax.broadcasted_iota(jnp.int32, (tm, LANES), 1)
    x = jnp.where(lane < N_EXPERTS, lg_ref[...], -jnp.inf)
    vals, ids = [], []
    onehot = jnp.zeros((tm, LANES), F32)
    for _ in range(TOP_K):
        m = jnp.max(x, axis=1, keepdims=True)
        idx = jnp.min(jnp.where(x == m, lane, LANES), axis=1, keepdims=True)
        sel = lane == idx
        onehot = jnp.where(sel, 1.0, onehot)
        x = jnp.where(sel, -jnp.inf, x)
        vals.append(m)
        ids.append(idx)
    tile_counts = jnp.sum(onehot, axis=0, keepdims=True)

    @pl.when(jnp.logical_and(ph == 0, i == 0))
    def _():
        run_ref[...] = jnp.zeros_like(run_ref)

    @pl.when(ph == 0)
    def _():
        run_ref[...] += tile_counts

    @pl.when(jnp.logical_and(ph == 1, i == 0))
    def _():
        sizes = run_ref[...]
        cnt_ref[...] = sizes
        psizes = jnp.ceil(sizes * (1.0 / MOE_BLOCK)) * MOE_BLOCK
        r2 = lax.broadcasted_iota(jnp.int32, (LANES, LANES), 0)
        c2 = lax.broadcasted_iota(jnp.int32, (LANES, LANES), 1)
        incl = jnp.where(r2 <= c2, 1.0, 0.0).astype(BF16)
        pst_ref[...] = _cumsum_dot(incl, psizes, False) - psizes
        run_ref[...] = jnp.zeros_like(run_ref)

    @pl.when(ph == 1)
    def _():
        rr = lax.broadcasted_iota(jnp.int32, (tm, tm), 0)
        cc = lax.broadcasted_iota(jnp.int32, (tm, tm), 1)
        strict = jnp.where(cc < rr, 1.0, 0.0).astype(BF16)
        base = _dot(strict, onehot.astype(BF16)) + run_ref[0:1, :] + pst_ref[0:1, :]
        denom = jnp.ones_like(vals[0])
        for k in range(1, TOP_K):
            denom = denom + jnp.exp(vals[k] - vals[0])
        gates = jnp.zeros((tm, LANES), F32)
        eids = jnp.zeros((tm, LANES), jnp.int32)
        poss = jnp.zeros((tm, LANES), jnp.int32)
        for k in range(TOP_K):
            pk = jnp.sum(jnp.where(lane == ids[k], base, 0.0), axis=1, keepdims=True)
            gates = jnp.where(lane == k, jnp.exp(vals[k] - vals[0]) / denom, gates)
            eids = jnp.where(lane == k, ids[k], eids)
            poss = jnp.where(lane == k, pk.astype(jnp.int32), poss)
        gate_ref[...] = gates
        eid_ref[...] = eids
        pos_ref[...] = poss
        run_ref[...] += tile_counts


def _route(logits):
    t = logits.shape[0]
    tile = pl.BlockSpec((ROW_TILE, LANES), lambda ph, i: (i, 0))
    out_tile = pl.BlockSpec((ROW_TILE, LANES), lambda ph, i: (i * ph, 0))
    gates, eids, poss, cnt = pl.pallas_call(
        _route_kernel, grid=(2, t // ROW_TILE), in_specs=[tile],
        out_specs=[out_tile, out_tile, out_tile, pl.BlockSpec((SUBLANES, LANES), lambda ph, i: (0, 0))],
        out_shape=[jax.ShapeDtypeStruct((t, LANES), F32), jax.ShapeDtypeStruct((t, LANES), jnp.int32),
                   jax.ShapeDtypeStruct((t, LANES), jnp.int32), jax.ShapeDtypeStruct((SUBLANES, LANES), F32)],
        scratch_shapes=[pltpu.VMEM((SUBLANES, LANES), F32), pltpu.VMEM((SUBLANES, LANES), F32)],
        compiler_params=_cparams(("arbitrary", "arbitrary")), name="router_topk",
    )(logits)
    n = t * TOP_K
    p = n + N_EXPERTS * MOE_BLOCK
    nb = p // MOE_BLOCK
    sizes = cnt[0, :N_EXPERTS].astype(jnp.int32)
    psizes = (sizes + MOE_BLOCK - 1) // MOE_BLOCK * MOE_BLOCK
    pends = jnp.cumsum(psizes)
    pstarts = pends - psizes
    starts = jnp.cumsum(sizes) - sizes
    block_e = jnp.minimum(jnp.searchsorted(pends, jnp.arange(nb, dtype=jnp.int32) * MOE_BLOCK, side='right'),
                          N_EXPERTS - 1).astype(jnp.int32)
    n_blocks = (pends[-1] // MOE_BLOCK).astype(jnp.int32).reshape(1)
    pos = poss[:, :TOP_K]
    order = jnp.argsort(eids[:, :TOP_K].reshape(n), stable=True).astype(jnp.int32)
    e_row = jnp.repeat(block_e, MOE_BLOCK)
    local = jnp.arange(p, dtype=jnp.int32) - pstarts[e_row]
    pair = jnp.take(order, jnp.clip(starts[e_row] + local, 0, n - 1))
    src_tok = jnp.where(local < sizes[e_row], pair // TOP_K, 0)
    return gates, pos, src_tok, block_e, n_blocks


def _combine_kernel(alpha, x_ref, y_ref, gate_ref, g_ref, b_ref, o_ref, ob_ref):
    gates = gate_ref[...]
    moe = gates[:, 0:1] * y_ref[0].astype(F32)
    for k in range(1, TOP_K):
        moe = moe + gates[:, k:k + 1] * y_ref[k].astype(F32)
    y = _layernorm_rows(alpha * x_ref[...] + moe, g_ref[...], b_ref[...])
    o_ref[...] = y
    ob_ref[...] = y.astype(BF16)


def _combine(x, yk, gates, g, b, alpha):
    t, d = x.shape
    row = pl.BlockSpec((ROW_TILE, d), lambda i: (i, 0))
    vec = pl.BlockSpec((1, d), lambda i: (0, 0))
    return pl.pallas_call(
        functools.partial(_combine_kernel, alpha), grid=(t // ROW_TILE,),
        in_specs=[row, pl.BlockSpec((TOP_K, ROW_TILE, d), lambda i: (0, i, 0)),
                  pl.BlockSpec((ROW_TILE, LANES), lambda i: (i, 0)), vec, vec],
        out_specs=[row, row],
        out_shape=[jax.ShapeDtypeStruct((t, d), F32), jax.ShapeDtypeStruct((t, d), BF16)],
        compiler_params=_cparams(("parallel",)), name="expert_combine_layernorm",
    )(x, yk, gates, g.reshape(1, d), b.reshape(1, d))


def kernel(x, emb_ln_g, emb_ln_b, w_in, b_in, conv_w, ml_norm_w, ret_norm_w, w_out, ln1_g, ln1_b,
           w_router, b_router, w1, b1, w2, b2, ln2_g, ln2_b):
    batch, seq, d = x.shape
    depth = w_in.shape[0]
    assert d == D_MODEL and seq % ROW_TILE == 0
    t = batch * seq
    alpha = (2.0 * depth) ** 0.25

    w_in_p = _layout_proj(w_in).astype(BF16)
    b_in_p = _layout_proj(b_in)[:, None, :]
    hk = HEADS * DK
    conv_p = jnp.concatenate([_pad_last(conv_w[..., :hk], QKP), _pad_last(conv_w[..., hk:], QKP)], axis=-1)
    mlw_p = _pad_heads(ml_norm_w)[:, None, :]
    rtw_p = _pad_heads(ret_norm_w)[:, None, :]
    w_out_p = _layout_wout(w_out).astype(BF16)
    wr_hi, wr_lo = _split_bf16(_pad_last(w_router, LANES))
    br_p = _pad_last(b_router, LANES)[:, None, :]
    w1_r = w1.reshape(depth * N_EXPERTS, d, 2 * D_FF)
    w2_r = w2.reshape(depth * N_EXPERTS, D_FF, d)
    b1_r = b1.reshape(depth * N_EXPERTS, 1, 2 * D_FF)
    b2_r = b2.reshape(depth * N_EXPERTS, 1, d)

    cs, ms, chan = _dft_tables(seq)
    cos, sin = _rotary_tables(seq)
    ret_tab = _retention_tables()

    xf, xb = _ln(x.reshape(t, d), emb_ln_g, emb_ln_b)
    for l in range(depth):
        pf, mqk, mv, mo, mg, rqk, rv, rg = _inproj(xb, w_in_p[l], b_in_p[l])
        zv, zw = _fourier_channels(pf, chan, batch, seq)
        yf = _fourier_sequence(cs, ms, zv, zw)
        hf, hb = _mlstm(mqk, mv, mg, conv_p[l], batch, seq)
        rf, rb = _retention(rqk, rv, cos, sin, ret_tab, batch, seq)
        x1, x1b, logits = _outproj(
            xf, yf, hf.reshape(t, HW), hb.reshape(t, HW), mo, rf.reshape(t, HW), rb.reshape(t, HW), rg,
            mlw_p[l], rtw_p[l], w_out_p[l], ln1_g[l][None, :], ln1_b[l][None, :],
            wr_hi[l], wr_lo[l], br_p[l], alpha, batch, seq)
        gates, pos, src_tok, block_e, n_blocks = _route(logits)
        xs = jnp.take(x1b, src_tok, axis=0)
        ys = _experts(block_e, n_blocks, xs, w1_r, b1_r, w2_r, b2_r, l)
        yk = jnp.take(ys, pos.T.reshape(TOP_K * t), axis=0).reshape(TOP_K, t, d)
        xf, xb = _combine(x1, yk, gates, ln2_g[l], ln2_b[l], alpha)
    return xf.reshape(batch, seq, d)
```

```python
import functools

import jax
import jax.numpy as jnp
from jax import lax
from jax.experimental import pallas as pl
from jax.experimental.pallas import tpu as pltpu

F32 = jnp.float32
BF16 = jnp.bfloat16

D_MODEL = 1024
CHUNK = 128
FOURIER_W = D_MODEL // 4
N_FGROUPS = 4
FG_W = FOURIER_W // N_FGROUPS
ML_W = 3 * D_MODEL // 8
RET_W = D_MODEL - FOURIER_W - ML_W
HEADS = 4
DV = ML_W // HEADS
DK = DV // 2
CONV_W = 3
ROPE_BASE = 10000.0
RET_GAMMA_EXP0 = 5.0
RET_BWD_EXP_OFFSET = 0.5
N_EXPERTS = 32
TOP_K = 4
D_FF = D_MODEL
SWIGLU_ALPHA = 1.702
SWIGLU_LIMIT = 7.0
LN_EPS = 1e-5

COL_F = 0
COL_MQK = COL_F + FOURIER_W
COL_MV = COL_MQK + 2 * HEADS * DK
COL_MO = COL_MV + ML_W
COL_MG = COL_MO + ML_W
COL_RQ = COL_MG + 4 * HEADS
COL_RK = COL_RQ + HEADS * DK
COL_RV = COL_RK + HEADS * DK
COL_RG = COL_RV + RET_W
PROJ_W = COL_RG + RET_W

LANES = 128
SUBLANES = 8
DVP = LANES
QKP = 2 * LANES
HW = HEADS * DVP
DEN_COL = DV
GATE_ROWS = 4 * SUBLANES
VMEM_LIMIT = 52 * 1024 * 1024

PROJ_GROUPS = (FOURIER_W, 2 * QKP, HW, HW, 2 * QKP, HW, HW)
PROJ_DTYPES = (BF16, F32, BF16, BF16, F32, BF16, BF16)
MV_OFFSET = FOURIER_W + 2 * QKP
MIX_P = FOURIER_W + 2 * HW

ROW_TILE = 512
MOE_BLOCK = 256


def _cparams(sem):
    return pltpu.CompilerParams(dimension_semantics=sem, vmem_limit_bytes=VMEM_LIMIT)


def _pad_last(w, n):
    return jnp.pad(w, [(0, 0)] * (w.ndim - 1) + [(0, n - w.shape[-1])])


def _pad_heads(w):
    lead = w.shape[:-1]
    w = w.reshape(*lead, HEADS, DV)
    w = jnp.pad(w, [(0, 0)] * (len(lead) + 1) + [(0, DVP - DV)])
    return w.reshape(*lead, HW)


def _pair_split(w):
    lead = w.shape[:-1]
    w = w.reshape(*lead, HEADS, DK // 2, 2)
    w = jnp.swapaxes(w, -1, -2).reshape(*lead, HEADS * DK)
    return _pad_last(w, QKP)


def _layout_proj(w):
    hk = HEADS * DK
    return jnp.concatenate([
        w[..., COL_F:COL_MQK],
        _pad_last(w[..., COL_MQK:COL_MQK + hk], QKP), _pad_last(w[..., COL_MQK + hk:COL_MV], QKP),
        _pad_heads(w[..., COL_MV:COL_MO]), _pad_heads(w[..., COL_MO:COL_MG]),
        _pair_split(w[..., COL_RQ:COL_RK]), _pair_split(w[..., COL_RK:COL_RV]),
        _pad_heads(w[..., COL_RV:COL_RG]), _pad_heads(w[..., COL_RG:PROJ_W]),
    ], axis=-1)


def _layout_gates(w):
    g = w[..., COL_MG:COL_RQ]
    pick = lambda j: _pad_last(g[..., HEADS * j:HEADS * (j + 1)], SUBLANES)
    return jnp.concatenate([pick(1), pick(0), pick(3), pick(2)], axis=-1)


def _layout_wout(w):
    wt = jnp.swapaxes(w, -1, -2)
    wt = jnp.concatenate([wt[..., :FOURIER_W], _pad_heads(wt[..., FOURIER_W:FOURIER_W + ML_W]),
                          _pad_heads(wt[..., FOURIER_W + ML_W:])], axis=-1)
    return jnp.swapaxes(wt, -1, -2)


def _split_bf16(w):
    hi = w.astype(BF16)
    return hi, (w - hi.astype(F32)).astype(BF16)


def _dft_tables(seq):
    k = jnp.arange(seq, dtype=jnp.int32)
    kn = (k[:, None] * k[None, :]) % seq
    ang = kn.astype(F32) * (2.0 * jnp.pi / seq)
    scale = seq ** -0.5
    cs = (jnp.cos(ang) * scale).astype(BF16)
    ms = (-jnp.sin(ang) * scale).astype(BF16)
    c = jnp.arange(FG_W, dtype=jnp.int32)
    cc = ((c[:, None] * c[None, :]) % FG_W).astype(F32) * (2.0 * jnp.pi / FG_W)
    eye = jnp.eye(N_FGROUPS, dtype=F32)
    bd_c = jnp.kron(eye, jnp.cos(cc) * FG_W ** -0.5)
    bd_s = jnp.kron(eye, jnp.sin(cc) * FG_W ** -0.5)
    chan = jnp.concatenate([bd_c, bd_s], axis=1).astype(BF16)
    return cs, ms, chan


def _rotary_tables(seq):
    inv = 1.0 / (ROPE_BASE ** (jnp.arange(0, DK, 2, dtype=F32) / DK))
    ang = jnp.arange(seq, dtype=F32)[:, None] * inv[None, :]
    cos, sin = jnp.cos(ang), jnp.sin(ang)
    zero = jnp.zeros_like(sin)
    heads = lambda a, b: _pad_last(jnp.tile(jnp.concatenate([a, b], axis=1), (1, HEADS)), QKP)
    return heads(cos, cos), heads(-sin, zero), heads(zero, sin)


def _retention_tables():
    idx = jnp.arange(CHUNK, dtype=F32)
    diff = idx[:, None] - idx[None, :]
    tiles, rows = [], []
    for rev in (False, True):
        offset = RET_BWD_EXP_OFFSET if rev else 0.0
        lg = jnp.log1p(-jnp.exp2(-(RET_GAMMA_EXP0 + offset) - jnp.arange(HEADS, dtype=F32)))
        lg3 = lg[:, None, None]
        if rev:
            decay = jnp.where((diff < 0)[None], jnp.exp(lg3 * jnp.maximum(-diff, 0.0)[None]), 0.0)
            w_inter = jnp.exp(lg[:, None] * (CHUNK - idx)[None, :])
            w_key = jnp.exp(lg[:, None] * idx[None, :])
        else:
            decay = jnp.where((diff >= 0)[None], jnp.exp(lg3 * jnp.maximum(diff, 0.0)[None]), 0.0)
            w_inter = jnp.exp(lg[:, None] * (idx + 1.0)[None, :])
            w_key = jnp.exp(lg[:, None] * (CHUNK - 1 - idx)[None, :])
        g_chunk = jnp.broadcast_to(jnp.exp(lg * CHUNK)[:, None], (HEADS, CHUNK))
        tiles.append(jnp.stack([decay, jnp.broadcast_to(w_inter[:, :, None], (HEADS, CHUNK, LANES))], axis=1))
        pad = lambda t: jnp.pad(t, ((0, SUBLANES - HEADS), (0, 0)))
        rows.append(jnp.stack([pad(w_key), pad(g_chunk)], axis=0))
    return jnp.stack(tiles, axis=0), jnp.stack(rows, axis=0)


def _layernorm_rows(z, g, b):
    mu = jnp.mean(z, axis=-1, keepdims=True)
    d = z - mu
    var = jnp.mean(d * d, axis=-1, keepdims=True)
    return d * lax.rsqrt(var + LN_EPS) * g + b


def _ln_kernel(x_ref, g_ref, b_ref, o_ref, ob_ref):
    y = _layernorm_rows(x_ref[...], g_ref[...], b_ref[...])
    o_ref[...] = y
    ob_ref[...] = y.astype(BF16)


def _ln(x, g, b):
    t, d = x.shape
    row = pl.BlockSpec((ROW_TILE, d), lambda i: (i, 0))
    vec = pl.BlockSpec((1, d), lambda i: (0, 0))
    return pl.pallas_call(
        _ln_kernel, grid=(t // ROW_TILE,), in_specs=[row, vec, vec], out_specs=[row, row],
        out_shape=[jax.ShapeDtypeStruct((t, d), F32), jax.ShapeDtypeStruct((t, d), BF16)],
        compiler_params=_cparams(("parallel",)), name="input_layernorm",
    )(x, g.reshape(1, d), b.reshape(1, d))


def _inproj_kernel(x_ref, w_ref, b_ref, wg_ref, bg_ref, *o_refs):
    x = x_ref[...]
    c0 = 0
    for o_ref in o_refs[:-1]:
        n = o_ref.shape[-1]
        y = jnp.dot(x, w_ref[:, c0:c0 + n], preferred_element_type=F32) + b_ref[:, c0:c0 + n]
        o_ref[...] = y.astype(o_ref.dtype)
        c0 += n
    gt = lax.dot_general(wg_ref[...], x, (((1,), (1,)), ((), ())), preferred_element_type=F32)
    o_refs[-1][...] = gt + bg_ref[...]


def _inproj(xb, w, b, wg, bg):
    t, d = xb.shape
    n_p = w.shape[1]
    const = lambda a: pl.BlockSpec(a.shape, lambda i: (0, 0))
    return pl.pallas_call(
        _inproj_kernel, grid=(t // ROW_TILE,),
        in_specs=[pl.BlockSpec((ROW_TILE, d), lambda i: (i, 0)), const(w), const(b), const(wg), const(bg)],
        out_specs=[pl.BlockSpec((ROW_TILE, n), lambda i: (i, 0)) for n in PROJ_GROUPS]
        + [pl.BlockSpec((GATE_ROWS, ROW_TILE), lambda i: (0, i))],
        out_shape=[jax.ShapeDtypeStruct((t, n), dt) for n, dt in zip(PROJ_GROUPS, PROJ_DTYPES)]
        + [jax.ShapeDtypeStruct((GATE_ROWS, t), F32)],
        compiler_params=_cparams(("parallel",)), name="input_projection",
    )(xb, w, b, wg, bg)


def _fchan_kernel(u_ref, m_ref, v_ref, w_ref):
    vw = jnp.dot(u_ref[...].astype(BF16), m_ref[...], preferred_element_type=F32)
    v_ref[...] = vw[:, :FOURIER_W].astype(BF16)
    w_ref[...] = vw[:, FOURIER_W:].astype(BF16)


def _fourier_channels(u, chan, batch, seq):
    tiles = seq // ROW_TILE
    out = pl.BlockSpec((ROW_TILE, FOURIER_W), lambda b, i: (i, b))
    return pl.pallas_call(
        _fchan_kernel, grid=(batch, tiles),
        in_specs=[pl.BlockSpec((ROW_TILE, FOURIER_W), lambda b, i: (b * tiles + i, 0)),
                  pl.BlockSpec((FOURIER_W, 2 * FOURIER_W), lambda b, i: (0, 0))],
        out_specs=[out, out],
        out_shape=[jax.ShapeDtypeStruct((seq, batch * FOURIER_W), BF16)] * 2,
        compiler_params=_cparams(("parallel", "parallel")), name="fourier_channel_dft",
    )(u, chan)


def _fseq_kernel(c_ref, s_ref, v_ref, w_ref, o_ref, acc_ref):
    k = pl.program_id(2)

    @pl.when(k == 0)
    def _():
        acc_ref[...] = jnp.zeros_like(acc_ref)

    acc_ref[...] += (jnp.dot(c_ref[...], v_ref[...], preferred_element_type=F32)
                     + jnp.dot(s_ref[...], w_ref[...], preferred_element_type=F32))

    @pl.when(k == pl.num_programs(2) - 1)
    def _():
        o_ref[...] = acc_ref[...]


def _fourier_sequence(cs, ms, zv, zw):
    seq, n = zv.shape
    tm, tn, tk = min(seq, 1024), min(n, 1024), min(seq, 512)
    return pl.pallas_call(
        _fseq_kernel, grid=(seq // tm, n // tn, seq // tk),
        in_specs=[pl.BlockSpec((tm, tk), lambda i, j, k: (i, k)),
                  pl.BlockSpec((tm, tk), lambda i, j, k: (i, k)),
                  pl.BlockSpec((tk, tn), lambda i, j, k: (k, j)),
                  pl.BlockSpec((tk, tn), lambda i, j, k: (k, j))],
        out_specs=pl.BlockSpec((tm, tn), lambda i, j, k: (i, j)),
        out_shape=jax.ShapeDtypeStruct((seq, n), F32),
        scratch_shapes=[pltpu.VMEM((tm, tn), F32)],
        compiler_params=_cparams(("parallel", "parallel", "arbitrary")), name="fourier_sequence_dft",
    )(cs, ms, zv, zw)


def _dot(a, b):
    return jnp.dot(a, b, preferred_element_type=F32)


def _cumsum_dot(tri, x, tri_left):
    hi = x.astype(BF16)
    r1 = x - hi.astype(F32)
    mid = r1.astype(BF16)
    lo = (r1 - mid.astype(F32)).astype(BF16)
    if tri_left:
        return _dot(tri, hi) + _dot(tri, mid) + _dot(tri, lo)
    return _dot(hi, tri) + _dot(mid, tri) + _dot(lo, tri)


def _log_sigmoid(x):
    return jnp.minimum(x, 0.0) - jnp.log1p(jnp.exp(-jnp.abs(x)))


def _mixer_prep_kernel(mqk_ref, prev_ref, next_ref, rqk_ref, rc_ref, rs1_ref, rs2_ref, convw_ref, g_ref,
                       mq_ref, mkt_ref, rq_ref, rkt_ref, gw_ref):
    i = pl.program_id(1)
    n = CHUNK
    r2 = lax.broadcasted_iota(jnp.int32, (n, n), 0)
    c2 = lax.broadcasted_iota(jnp.int32, (n, n), 1)
    groups = []
    for rev in (False, True):
        r0 = 2 * SUBLANES if rev else 0
        tri = jnp.where((c2 <= r2) if rev else (c2 >= r2), 1.0, 0.0).astype(BF16)
        a = _cumsum_dot(tri, _log_sigmoid(g_ref[r0:r0 + SUBLANES, :]), False)
        groups += [g_ref[r0 + SUBLANES:r0 + 2 * SUBLANES, :] - a, a]
    gw_ref[0] = jnp.concatenate(groups, axis=0)

    qk = mqk_ref[0]
    rows = lax.broadcasted_iota(jnp.int32, (n, 2 * QKP), 0)
    prev_row = prev_ref[0, 0, SUBLANES - 1:SUBLANES, :] * jnp.where(i == 0, 0.0, 1.0)
    next_row = next_ref[0, 0, 0:1, :] * jnp.where(i == pl.num_programs(1) - 1, 0.0, 1.0)
    xm1 = jnp.where(rows == 0, prev_row, pltpu.roll(qk, 1, 0))
    xp1 = jnp.where(rows == n - 1, next_row, pltpu.roll(qk, n - 1, 0))
    conv = xm1 * convw_ref[0:1, :] + qk * convw_ref[1:2, :] + xp1 * convw_ref[2:3, :]
    act = conv * jax.nn.sigmoid(conv)
    mq_ref[0] = act[:, :QKP].astype(BF16)
    mkt_ref[0, 0] = (act[:, QKP:] * DK ** -0.5).T.astype(BF16)

    rc, rs1, rs2 = rc_ref[...], rs1_ref[...], rs2_ref[...]
    half = DK // 2

    def rotate(t):
        return t * rc + pltpu.roll(t, QKP - half, 1) * rs1 + pltpu.roll(t, half, 1) * rs2

    rqk = rqk_ref[0]
    rq_ref[0] = rotate(rqk[:, :QKP]).astype(BF16)
    rkt_ref[0, 0] = (rotate(rqk[:, QKP:]) * DK ** -0.5).T.astype(BF16)


def _mixer_prep(mqk, rqk, rc, rs1, rs2, convw, gates, batch, seq):
    n = seq // CHUNK
    g8 = CHUNK // SUBLANES
    mqk3 = mqk.reshape(batch, seq, 2 * QKP)
    mqk8 = mqk.reshape(batch, seq // SUBLANES, SUBLANES, 2 * QKP)
    rqk3 = rqk.reshape(batch, seq, 2 * QKP)
    chunk = lambda w: pl.BlockSpec((1, CHUNK, w), lambda b, i: (b, i, 0))
    halo = lambda index: pl.BlockSpec((1, 1, SUBLANES, 2 * QKP), index)
    pos = pl.BlockSpec((CHUNK, QKP), lambda b, i: (i, 0))
    kt = pl.BlockSpec((1, 1, QKP, CHUNK), lambda b, i: (b, i, 0, 0))
    q_shape = jax.ShapeDtypeStruct((batch, seq, QKP), BF16)
    kt_shape = jax.ShapeDtypeStruct((batch, n, QKP, CHUNK), BF16)
    return pl.pallas_call(
        _mixer_prep_kernel, grid=(batch, n),
        in_specs=[chunk(2 * QKP),
                  halo(lambda b, i: (b, jnp.maximum(i * g8 - 1, 0), 0, 0)),
                  halo(lambda b, i: (b, jnp.minimum((i + 1) * g8, seq // SUBLANES - 1), 0, 0)),
                  chunk(2 * QKP), pos, pos, pos,
                  pl.BlockSpec((CONV_W, 2 * QKP), lambda b, i: (0, 0)),
                  pl.BlockSpec((GATE_ROWS, CHUNK), lambda b, i: (0, b * n + i))],
        out_specs=[chunk(QKP), kt, chunk(QKP), kt,
                   pl.BlockSpec((1, GATE_ROWS, CHUNK), lambda b, i: (b * n + i, 0, 0))],
        out_shape=[q_shape, kt_shape, q_shape, kt_shape,
                   jax.ShapeDtypeStruct((batch * n, GATE_ROWS, CHUNK), F32)],
        compiler_params=_cparams(("parallel", "parallel")), name="mixer_qk_prep",
    )(mqk3, mqk8, mqk8, rqk3, rc, rs1, rs2, convw, gates)


def _head_key_blocks(kt):
    rows = lax.broadcasted_iota(jnp.int32, kt.shape, 0)
    zero = jnp.zeros_like(kt)
    return jnp.concatenate([jnp.where((rows >= DK * h) & (rows < DK * (h + 1)), kt, zero)
                            for h in range(HEADS)], axis=1)


def _mlstm_direction(rev, q, kt, v, gw, c_ref, cb_ref, m_ref, d):
    n = CHUNK
    r0 = 2 * SUBLANES if rev else 0
    key_w = gw[r0:r0 + SUBLANES, :]
    a = gw[r0 + SUBLANES:r0 + 2 * SUBLANES, :]
    r2 = lax.broadcasted_iota(jnp.int32, (n, n), 0)
    c2 = lax.broadcasted_iota(jnp.int32, (n, n), 1)
    causal = (c2 >= r2) if rev else (c2 <= r2)
    lane = lax.broadcasted_iota(jnp.int32, (n, LANES), 1)

    cmax = jnp.zeros((n, LANES), F32)
    for h in range(HEADS):
        cm_h = jnp.max(jnp.where(causal, key_w[h:h + 1, :], -jnp.inf), axis=1, keepdims=True)
        cmax = jnp.where(lane == DEN_COL + h, cm_h, cmax)
    zero = jnp.zeros((SUBLANES, n), F32)
    den_group = DEN_COL // SUBLANES
    a_col = jnp.concatenate([zero] * den_group + [a] + [zero] * (n // SUBLANES - den_group - 1), axis=0).T
    m_lane = m_ref[d, 1, 0:1, :]
    mm = jnp.maximum(m_lane, cmax)
    s_inter = jnp.exp(m_lane - mm)
    e_negm = jnp.exp(-a_col - mm)

    s_all = _dot(q, _head_key_blocks(kt))
    inter_all = _dot(q, cb_ref[d])
    rs = []
    for h in range(HEADS):
        blk = slice(DVP * h, DVP * (h + 1))
        col = slice(DEN_COL + h, DEN_COL + h + 1)
        p = jnp.exp(jnp.where(causal, key_w[h:h + 1, :] - mm[:, col], -jnp.inf))
        scores = (s_all[:, blk] * p).astype(BF16)
        rs.append(_dot(scores, v[:, blk]) + s_inter[:, col] * inter_all[:, blk])
    den = rs[0]
    for h in range(1, HEADS):
        den = den + rs[h]
    rden = 1.0 / jnp.maximum(jnp.abs(den), e_negm)
    out = jnp.concatenate([rs[h] * rden[:, DEN_COL + h:DEN_COL + h + 1] for h in range(HEADS)], axis=1)

    m_prev = m_ref[d, 0]
    a_end = jnp.broadcast_to(a[:, 0:1] if rev else a[:, n - 1:n], (SUBLANES, n))
    w_key = a_end + key_w
    m_chunk = jnp.broadcast_to(jnp.max(w_key, axis=1, keepdims=True), (SUBLANES, n))
    m_new = jnp.maximum(a_end + m_prev, m_chunk)
    s_prev = jnp.exp(a_end + m_prev - m_new)
    p_key = jnp.exp(w_key - m_chunk) * jnp.exp(m_chunk - m_new)
    m_ref[d, 0] = m_new
    row8 = lax.broadcasted_iota(jnp.int32, (SUBLANES, n), 0)
    lane8 = lax.broadcasted_iota(jnp.int32, (SUBLANES, n), 1)
    on_diag = (lane8 == row8 + DEN_COL) & (row8 < HEADS)
    m_ref[d, 1] = jnp.broadcast_to(jnp.sum(jnp.where(on_diag, m_new, 0.0), axis=0, keepdims=True),
                                   (SUBLANES, n))
    for h in range(HEADS):
        blk = slice(DVP * h, DVP * (h + 1))
        keys = slice(DK * h, DK * (h + 1))
        kp = (kt[keys, :].astype(F32) * p_key[h:h + 1, :]).astype(BF16)
        c_new = s_prev[h:h + 1, :] * c_ref[d, h] + _dot(kp, v[:, blk])
        c_ref[d, h] = c_new
        cb_ref[d, keys, blk] = c_new.astype(BF16)
    return out


def _mlstm_kernel(q_f, kt_f, v_f, g_f, q_b, kt_b, v_b, g_b, hf_ref, hb_ref, c_ref, cb_ref, m_ref):
    @pl.when(pl.program_id(1) == 0)
    def _():
        c_ref[...] = jnp.zeros_like(c_ref)
        cb_ref[...] = jnp.zeros_like(cb_ref)
        m_ref[...] = jnp.zeros_like(m_ref)

    hf_ref[0] = _mlstm_direction(False, q_f[0], kt_f[0, 0], v_f[0], g_f[0], c_ref, cb_ref, m_ref, 0
                                 ).astype(hf_ref.dtype)
    hb_ref[0] = _mlstm_direction(True, q_b[0], kt_b[0, 0], v_b[0], g_b[0], c_ref, cb_ref, m_ref, 1
                                 ).astype(hb_ref.dtype)


def _mixer_specs(batch, seq):
    n = seq // CHUNK
    fwd = lambda b, i: (b, i, 0)
    bwd = lambda b, i: (b, n - 1 - i, 0)
    blk = lambda w, im: pl.BlockSpec((1, CHUNK, w), im)
    ktf = pl.BlockSpec((1, 1, QKP, CHUNK), lambda b, i: (b, i, 0, 0))
    ktb = pl.BlockSpec((1, 1, QKP, CHUNK), lambda b, i: (b, n - 1 - i, 0, 0))
    return n, fwd, bwd, blk, ktf, ktb


def _mlstm(q, kt, v, gw, batch, seq):
    n, fwd, bwd, blk, ktf, ktb = _mixer_specs(batch, seq)
    v3 = v.reshape(batch, seq, HW)
    gf = pl.BlockSpec((1, GATE_ROWS, CHUNK), lambda b, i: (b * n + i, 0, 0))
    gb = pl.BlockSpec((1, GATE_ROWS, CHUNK), lambda b, i: (b * n + n - 1 - i, 0, 0))
    return pl.pallas_call(
        _mlstm_kernel, grid=(batch, n),
        in_specs=[blk(QKP, fwd), ktf, blk(HW, fwd), gf, blk(QKP, bwd), ktb, blk(HW, bwd), gb],
        out_specs=[blk(HW, fwd), blk(HW, bwd)],
        out_shape=[jax.ShapeDtypeStruct((batch, seq, HW), BF16)] * 2,
        scratch_shapes=[pltpu.VMEM((2, HEADS, DK, DVP), F32), pltpu.VMEM((2, QKP, HW), BF16),
                        pltpu.VMEM((2, 2, SUBLANES, CHUNK), F32)],
        compiler_params=_cparams(("parallel", "arbitrary")), name="mlstm_mixer",
    )(q, kt, v3, gw, q, kt, v3, gw)


def _retention_direction(q, kt, v, tile_ref, row_ref, s_ref, sb_ref, d):
    s_all = _dot(q, _head_key_blocks(kt))
    inter_all = _dot(q, sb_ref[d])
    outs = []
    for h in range(HEADS):
        blk = slice(DVP * h, DVP * (h + 1))
        scores = (s_all[:, blk] * tile_ref[d, h, 0]).astype(BF16)
        outs.append(_dot(scores, v[:, blk]) + tile_ref[d, h, 1] * inter_all[:, blk])
    for h in range(HEADS):
        blk = slice(DVP * h, DVP * (h + 1))
        keys = slice(DK * h, DK * (h + 1))
        kp = (kt[keys, :].astype(F32) * row_ref[d, 0, h:h + 1, :]).astype(BF16)
        s_new = row_ref[d, 1, h:h + 1, :] * s_ref[d, h] + _dot(kp, v[:, blk])
        s_ref[d, h] = s_new
        sb_ref[d, keys, blk] = s_new.astype(BF16)
    return jnp.concatenate(outs, axis=1)


def _retention_kernel(q_f, kt_f, v_f, q_b, kt_b, v_b, tile_ref, row_ref, yf_ref, yb_ref, s_ref, sb_ref):
    @pl.when(pl.program_id(1) == 0)
    def _():
        s_ref[...] = jnp.zeros_like(s_ref)
        sb_ref[...] = jnp.zeros_like(sb_ref)

    yf_ref[0] = _retention_direction(q_f[0], kt_f[0, 0], v_f[0], tile_ref, row_ref, s_ref, sb_ref, 0
                                     ).astype(yf_ref.dtype)
    yb_ref[0] = _retention_direction(q_b[0], kt_b[0, 0], v_b[0], tile_ref, row_ref, s_ref, sb_ref, 1
                                     ).astype(yb_ref.dtype)


def _retention(q, kt, v, tiles, rows, batch, seq):
    n, fwd, bwd, blk, ktf, ktb = _mixer_specs(batch, seq)
    v3 = v.reshape(batch, seq, HW)
    return pl.pallas_call(
        _retention_kernel, grid=(batch, n),
        in_specs=[blk(QKP, fwd), ktf, blk(HW, fwd), blk(QKP, bwd), ktb, blk(HW, bwd),
                  pl.BlockSpec(tiles.shape, lambda b, i: (0, 0, 0, 0, 0)),
                  pl.BlockSpec(rows.shape, lambda b, i: (0, 0, 0, 0))],
        out_specs=[blk(HW, fwd), blk(HW, bwd)],
        out_shape=[jax.ShapeDtypeStruct((batch, seq, HW), BF16)] * 2,
        scratch_shapes=[pltpu.VMEM((2, HEADS, DK, DVP), F32), pltpu.VMEM((2, QKP, HW), BF16)],
        compiler_params=_cparams(("parallel", "arbitrary")), name="retention_mixer",
    )(q, kt, v3, q, kt, v3, tiles, rows)


def _head_norm(z, w):
    valid = lax.broadcasted_iota(jnp.int32, (1, DVP), 1) < DV
    outs = []
    for h in range(HEADS):
        zh = z[:, DVP * h:DVP * (h + 1)]
        mu = jnp.sum(jnp.where(valid, zh, 0.0), axis=1, keepdims=True) * (1.0 / DV)
        d = jnp.where(valid, zh - mu, 0.0)
        var = jnp.sum(d * d, axis=1, keepdims=True) * (1.0 / DV)
        outs.append(d * lax.rsqrt(var + LN_EPS))
    return jnp.concatenate(outs, axis=1) * w


def _outproj_kernel(alpha, x_ref, yf_ref, hf_ref, hb_ref, o_ref, rf_ref, rb_ref, rg_ref, mlw_ref, rtw_ref,
                    w_ref, g_ref, b_ref, wrh_ref, wrl_ref, br_ref, x1_ref, x1b_ref, lg_ref):
    f32 = lambda ref: ref[...].astype(F32)
    y_m = _head_norm(jax.nn.sigmoid(f32(o_ref)) * (f32(hf_ref) + f32(hb_ref)), mlw_ref[...])
    rg = f32(rg_ref)
    y_r = rg * jax.nn.sigmoid(rg) * _head_norm(f32(rf_ref) + f32(rb_ref), rtw_ref[...])
    cat = jnp.concatenate([yf_ref[...], y_m, y_r], axis=1).astype(BF16)
    mix = _dot(cat, w_ref[...])
    x1 = _layernorm_rows(alpha * x_ref[...] + mix, g_ref[...], b_ref[...])
    x1_ref[...] = x1
    hi = x1.astype(BF16)
    x1b_ref[...] = hi
    lo = (x1 - hi.astype(F32)).astype(BF16)
    lg_ref[...] = _dot(hi, wrh_ref[...]) + _dot(lo, wrh_ref[...]) + _dot(hi, wrl_ref[...]) + br_ref[...]


def _outproj(x, yf, hf, hb, mo, rf, rb, rg, mlw, rtw, w, g, b, wrh, wrl, br, alpha, batch, seq):
    t, d = x.shape
    tiles = seq // ROW_TILE
    row = lambda n: pl.BlockSpec((ROW_TILE, n), lambda i: (i, 0))
    const = lambda a: pl.BlockSpec(a.shape, lambda i: (0, 0))
    args = (x, yf, hf, hb, mo, rf, rb, rg, mlw, rtw, w, g, b, wrh, wrl, br)
    in_specs = [row(d), pl.BlockSpec((ROW_TILE, FOURIER_W), lambda i: (i % tiles, i // tiles)),
                row(HW), row(HW), row(HW), row(HW), row(HW), row(HW)] + [const(a) for a in args[8:]]
    return pl.pallas_call(
        functools.partial(_outproj_kernel, alpha), grid=(t // ROW_TILE,), in_specs=in_specs,
        out_specs=[row(d), row(d), row(LANES)],
        out_shape=[jax.ShapeDtypeStruct((t, d), F32), jax.ShapeDtypeStruct((t, d), BF16),
                   jax.ShapeDtypeStruct((t, LANES), F32)],
        compiler_params=_cparams(("parallel",)), name="mixer_output_projection",
    )(*args)


def _expert_kernel(be_ref, nb_ref, x_ref, w1_ref, b1_ref, w2_ref, b2_ref, o_ref, w1b_ref, w2b_ref):
    i = pl.program_id(0)

    @pl.when(jnp.logical_or(i == 0, be_ref[i] != be_ref[jnp.maximum(i - 1, 0)]))
    def _():
        w1b_ref[...] = w1_ref[0].astype(BF16)
        w2b_ref[...] = w2_ref[0].astype(BF16)

    @pl.when(i < nb_ref[0])
    def _():
        hc = _dot(x_ref[...], w1b_ref[...]) + b1_ref[0]
        gate = jnp.minimum(hc[:, :D_FF], SWIGLU_LIMIT)
        up = jnp.clip(hc[:, D_FF:], -SWIGLU_LIMIT, SWIGLU_LIMIT)
        glu = gate * jax.nn.sigmoid(SWIGLU_ALPHA * gate)
        y = _dot(((up + 1.0) * glu).astype(BF16), w2b_ref[...]) + b2_ref[0]
        o_ref[...] = y.astype(o_ref.dtype)

    @pl.when(i >= nb_ref[0])
    def _():
        o_ref[...] = jnp.zeros_like(o_ref)


def _experts(block_e, n_blocks, xs, w1, b1, w2, b2, layer):
    p, d = xs.shape
    off = layer * N_EXPERTS
    wmap = lambda i, be, nb: (off + be[i], 0, 0)
    xmap = lambda i, be, nb: (jnp.minimum(i, nb[0] - 1), 0)
    grid_spec = pltpu.PrefetchScalarGridSpec(
        num_scalar_prefetch=2, grid=(p // MOE_BLOCK,),
        in_specs=[pl.BlockSpec((MOE_BLOCK, d), xmap),
                  pl.BlockSpec((1, d, 2 * D_FF), wmap), pl.BlockSpec((1, 1, 2 * D_FF), wmap),
                  pl.BlockSpec((1, D_FF, d), wmap), pl.BlockSpec((1, 1, d), wmap)],
        out_specs=pl.BlockSpec((MOE_BLOCK, d), lambda i, be, nb: (i, 0)),
        scratch_shapes=[pltpu.VMEM((d, 2 * D_FF), BF16), pltpu.VMEM((D_FF, d), BF16)])
    return pl.pallas_call(
        _expert_kernel, grid_spec=grid_spec, out_shape=jax.ShapeDtypeStruct((p, d), BF16),
        compiler_params=_cparams(("arbitrary",)), name="routed_experts",
    )(block_e, n_blocks, xs, w1, b1, w2, b2)


def _route_kernel(lg_ref, gate_ref, eid_ref, pos_ref, cnt_ref, run_ref, pst_ref):
    ph = pl.program_id(0)
    i = pl.program_id(1)
    tm = lg_ref.shape[0]
    lane = lax.broadcasted_iota(jnp.int32, (tm, LANES), 1)
    x = jnp.where(lane < N_EXPERTS, lg_ref[...], -jnp.inf)
    vals, ids = [], []
    onehot = jnp.zeros((tm, LANES), F32)
    for _ in range(TOP_K):
        m = jnp.max(x, axis=1, keepdims=True)
        idx = jnp.min(jnp.where(x == m, lane, LANES), axis=1, keepdims=True)
        sel = lane == idx
        onehot = jnp.where(sel, 1.0, onehot)
        x = jnp.where(sel, -jnp.inf, x)
        vals.append(m)
        ids.append(idx)
    tile_counts = jnp.sum(onehot, axis=0, keepdims=True)

    @pl.when(jnp.logical_and(ph == 0, i == 0))
    def _():
        run_ref[...] = jnp.zeros_like(run_ref)

    @pl.when(ph == 0)
    def _():
        run_ref[...] += tile_counts

    @pl.when(jnp.logical_and(ph == 1, i == 0))
    def _():
        sizes = run_ref[...]
        cnt_ref[...] = sizes
        psizes = jnp.ceil(sizes * (1.0 / MOE_BLOCK)) * MOE_BLOCK
        r2 = lax.broadcasted_iota(jnp.int32, (LANES, LANES), 0)
        c2 = lax.broadcasted_iota(jnp.int32, (LANES, LANES), 1)
        incl = jnp.where(r2 <= c2, 1.0, 0.0).astype(BF16)
        pst_ref[...] = _cumsum_dot(incl, psizes, False) - psizes
        run_ref[...] = jnp.zeros_like(run_ref)

    @pl.when(ph == 1)
    def _():
        rr = lax.broadcasted_iota(jnp.int32, (tm, tm), 0)
        cc = lax.broadcasted_iota(jnp.int32, (tm, tm), 1)
        strict = jnp.where(cc < rr, 1.0, 0.0).astype(BF16)
        base = _dot(strict, onehot.astype(BF16)) + run_ref[0:1, :] + pst_ref[0:1, :]
        denom = jnp.ones_like(vals[0])
        for k in range(1, TOP_K):
            denom = denom + jnp.exp(vals[k] - vals[0])
        gates = jnp.zeros((tm, LANES), F32)
        eids = jnp.zeros((tm, LANES), jnp.int32)
        poss = jnp.zeros((tm, LANES), jnp.int32)
        for k in range(TOP_K):
            pk = jnp.sum(jnp.where(lane == ids[k], base, 0.0), axis=1, keepdims=True)
            gates = jnp.where(lane == k, jnp.exp(vals[k] - vals[0]) / denom, gates)
            eids = jnp.where(lane == k, ids[k], eids)
            poss = jnp.where(lane == k, pk.astype(jnp.int32), poss)
        gate_ref[...] = gates
        eid_ref[...] = eids
        pos_ref[...] = poss
        run_ref[...] += tile_counts


def _route(logits):
    t = logits.shape[0]
    tile = pl.BlockSpec((ROW_TILE, LANES), lambda ph, i: (i, 0))
    out_tile = pl.BlockSpec((ROW_TILE, LANES), lambda ph, i: (i * ph, 0))
    gates, eids, poss, cnt = pl.pallas_call(
        _route_kernel, grid=(2, t // ROW_TILE), in_specs=[tile],
        out_specs=[out_tile, out_tile, out_tile, pl.BlockSpec((SUBLANES, LANES), lambda ph, i: (0, 0))],
        out_shape=[jax.ShapeDtypeStruct((t, LANES), F32), jax.ShapeDtypeStruct((t, LANES), jnp.int32),
                   jax.ShapeDtypeStruct((t, LANES), jnp.int32), jax.ShapeDtypeStruct((SUBLANES, LANES), F32)],
        scratch_shapes=[pltpu.VMEM((SUBLANES, LANES), F32), pltpu.VMEM((SUBLANES, LANES), F32)],
        compiler_params=_cparams(("arbitrary", "arbitrary")), name="router_topk",
    )(logits)
    n = t * TOP_K
    p = n + N_EXPERTS * MOE_BLOCK
    nb = p // MOE_BLOCK
    sizes = cnt[0, :N_EXPERTS].astype(jnp.int32)
    psizes = (sizes + MOE_BLOCK - 1) // MOE_BLOCK * MOE_BLOCK
    pends = jnp.cumsum(psizes)
    pstarts = pends - psizes
    starts = jnp.cumsum(sizes) - sizes
    block_e = jnp.minimum(jnp.searchsorted(pends, jnp.arange(nb, dtype=jnp.int32) * MOE_BLOCK, side='right'),
                          N_EXPERTS - 1).astype(jnp.int32)
    n_blocks = (pends[-1] // MOE_BLOCK).astype(jnp.int32).reshape(1)
    pos = poss[:, :TOP_K]
    order = jnp.argsort(eids[:, :TOP_K].reshape(n), stable=True).astype(jnp.int32)
    e_row = jnp.repeat(block_e, MOE_BLOCK)
    local = jnp.arange(p, dtype=jnp.int32) - pstarts[e_row]
    pair = jnp.take(order, jnp.clip(starts[e_row] + local, 0, n - 1))
    src_tok = jnp.where(local < sizes[e_row], pair // TOP_K, 0)
    return gates, pos, src_tok, block_e, n_blocks


def _combine_kernel(alpha, x_ref, y_ref, gate_ref, g_ref, b_ref, o_ref, ob_ref):
    gates = gate_ref[...]
    moe = gates[:, 0:1] * y_ref[0].astype(F32)
    for k in range(1, TOP_K):
        moe = moe + gates[:, k:k + 1] * y_ref[k].astype(F32)
    y = _layernorm_rows(alpha * x_ref[...] + moe, g_ref[...], b_ref[...])
    o_ref[...] = y
    ob_ref[...] = y.astype(BF16)


def _combine(x, yk, gates, g, b, alpha):
    t, d = x.shape
    row = pl.BlockSpec((ROW_TILE, d), lambda i: (i, 0))
    vec = pl.BlockSpec((1, d), lambda i: (0, 0))
    return pl.pallas_call(
        functools.partial(_combine_kernel, alpha), grid=(t // ROW_TILE,),
        in_specs=[row, pl.BlockSpec((TOP_K, ROW_TILE, d), lambda i: (0, i, 0)),
                  pl.BlockSpec((ROW_TILE, LANES), lambda i: (i, 0)), vec, vec],
        out_specs=[row, row],
        out_shape=[jax.ShapeDtypeStruct((t, d), F32), jax.ShapeDtypeStruct((t, d), BF16)],
        compiler_params=_cparams(("parallel",)), name="expert_combine_layernorm",
    )(x, yk, gates, g.reshape(1, d), b.reshape(1, d))


def kernel(x, emb_ln_g, emb_ln_b, w_in, b_in, conv_w, ml_norm_w, ret_norm_w, w_out, ln1_g, ln1_b,
           w_router, b_router, w1, b1, w2, b2, ln2_g, ln2_b):
    batch, seq, d = x.shape
    depth = w_in.shape[0]
    assert d == D_MODEL and seq % ROW_TILE == 0
    t = batch * seq
    alpha = (2.0 * depth) ** 0.25

    w_in_p = _layout_proj(w_in).astype(BF16)
    den_cols = jnp.array([MV_OFFSET + DVP * h + DEN_COL + h for h in range(HEADS)])
    b_in_p = _layout_proj(b_in).at[:, den_cols].set(1.0)[:, None, :]
    wg_p = jnp.swapaxes(_layout_gates(w_in), -1, -2).astype(BF16)
    bg_p = _layout_gates(b_in)[:, :, None]
    hk = HEADS * DK
    conv_p = jnp.concatenate([_pad_last(conv_w[..., :hk], QKP), _pad_last(conv_w[..., hk:], QKP)], axis=-1)
    mlw_p = _pad_heads(ml_norm_w)[:, None, :]
    rtw_p = _pad_heads(ret_norm_w)[:, None, :]
    w_out_p = _layout_wout(w_out).astype(BF16)
    wr_hi, wr_lo = _split_bf16(_pad_last(w_router, LANES))
    br_p = _pad_last(b_router, LANES)[:, None, :]
    w1_r = w1.reshape(depth * N_EXPERTS, d, 2 * D_FF)
    w2_r = w2.reshape(depth * N_EXPERTS, D_FF, d)
    b1_r = b1.reshape(depth * N_EXPERTS, 1, 2 * D_FF)
    b2_r = b2.reshape(depth * N_EXPERTS, 1, d)

    cs, ms, chan = _dft_tables(seq)
    rc, rs1, rs2 = _rotary_tables(seq)
    ret_tiles, ret_rows = _retention_tables()

    xf, xb = _ln(x.reshape(t, d), emb_ln_g, emb_ln_b)
    for l in range(depth):
        pf, mqk, mv, mo, rqk, rv, rg, gates_t = _inproj(xb, w_in_p[l], b_in_p[l], wg_p[l], bg_p[l])
        zv, zw = _fourier_channels(pf, chan, batch, seq)
        yf = _fourier_sequence(cs, ms, zv, zw)
        mq, mkt, rq, rkt, gw = _mixer_prep(mqk, rqk, rc, rs1, rs2, conv_p[l], gates_t, batch, seq)
        hf, hb = _mlstm(mq, mkt, mv, gw, batch, seq)
        rf, rb = _retention(rq, rkt, rv, ret_tiles, ret_rows, batch, seq)
        x1, x1b, logits = _outproj(
            xf, yf, hf.reshape(t, HW), hb.reshape(t, HW), mo, rf.reshape(t, HW), rb.reshape(t, HW), rg,
            mlw_p[l], rtw_p[l], w_out_p[l], ln1_g[l][None, :], ln1_b[l][None, :],
            wr_hi[l], wr_lo[l], br_p[l], alpha, batch, seq)
        gates, pos, src_tok, block_e, n_blocks = _route(logits)
        xs = jnp.take(x1b, src_tok, axis=0)
        ys = _experts(block_e, n_blocks, xs, w1_r, b1_r, w2_r, b2_r, l)
        yk = jnp.take(ys, pos.T.reshape(TOP_K * t), axis=0).reshape(TOP_K, t, d)
        xf, xb = _combine(x1, yk, gates, ln2_g[l], ln2_b[l], alpha)
    return xf.reshape(batch, seq, d)
```

```python
import functools

import jax
import jax.numpy as jnp
from jax import lax
from jax.experimental import pallas as pl
from jax.experimental.pallas import tpu as pltpu

F32 = jnp.float32
BF16 = jnp.bfloat16

D_MODEL = 1024
CHUNK = 128
FOURIER_W = D_MODEL // 4
N_FGROUPS = 4
FG_W = FOURIER_W // N_FGROUPS
ML_W = 3 * D_MODEL // 8
RET_W = D_MODEL - FOURIER_W - ML_W
HEADS = 4
DV = ML_W // HEADS
DK = DV // 2
CONV_W = 3
ROPE_BASE = 10000.0
RET_GAMMA_EXP0 = 5.0
RET_BWD_EXP_OFFSET = 0.5
N_EXPERTS = 32
TOP_K = 4
D_FF = D_MODEL
SWIGLU_ALPHA = 1.702
SWIGLU_LIMIT = 7.0
LN_EPS = 1e-5

COL_F = 0
COL_MQK = COL_F + FOURIER_W
COL_MV = COL_MQK + 2 * HEADS * DK
COL_MO = COL_MV + ML_W
COL_MG = COL_MO + ML_W
COL_RQ = COL_MG + 4 * HEADS
COL_RK = COL_RQ + HEADS * DK
COL_RV = COL_RK + HEADS * DK
COL_RG = COL_RV + RET_W
PROJ_W = COL_RG + RET_W

LANES = 128
SUBLANES = 8
DVP = LANES
QKP = 2 * LANES
HW = HEADS * DVP
DEN_COL = DV
GATE_ROWS = 4 * SUBLANES
VMEM_LIMIT = 52 * 1024 * 1024

PROJ_GROUPS = (FOURIER_W, 2 * QKP, HW, HW, 2 * QKP, HW, HW)
PROJ_DTYPES = (BF16, F32, BF16, BF16, F32, BF16, BF16)
MV_OFFSET = FOURIER_W + 2 * QKP
MIX_P = FOURIER_W + 2 * HW

FFT_N1 = 128
ROW_TILE = 512
MOE_BLOCK = 256

def _cparams(sem):
    return pltpu.CompilerParams(dimension_semantics=sem, vmem_limit_bytes=VMEM_LIMIT)


def _pad_last(w, n):
    return jnp.pad(w, [(0, 0)] * (w.ndim - 1) + [(0, n - w.shape[-1])])


def _pad_heads(w):
    lead = w.shape[:-1]
    w = w.reshape(*lead, HEADS, DV)
    w = jnp.pad(w, [(0, 0)] * (len(lead) + 1) + [(0, DVP - DV)])
    return w.reshape(*lead, HW)


def _pair_split(w):
    lead = w.shape[:-1]
    w = w.reshape(*lead, HEADS, DK // 2, 2)
    w = jnp.swapaxes(w, -1, -2).reshape(*lead, HEADS * DK)
    return _pad_last(w, QKP)


def _layout_proj(w):
    hk = HEADS * DK
    return jnp.concatenate([
        w[..., COL_F:COL_MQK],
        _pad_last(w[..., COL_MQK:COL_MQK + hk], QKP), _pad_last(w[..., COL_MQK + hk:COL_MV], QKP),
        _pad_heads(w[..., COL_MV:COL_MO]), _pad_heads(w[..., COL_MO:COL_MG]),
        _pair_split(w[..., COL_RQ:COL_RK]), _pair_split(w[..., COL_RK:COL_RV]),
        _pad_heads(w[..., COL_RV:COL_RG]), _pad_heads(w[..., COL_RG:PROJ_W]),
    ], axis=-1)


def _layout_gates(w):
    g = w[..., COL_MG:COL_RQ]
    pick = lambda j: _pad_last(g[..., HEADS * j:HEADS * (j + 1)], SUBLANES)
    return jnp.concatenate([pick(1), pick(0), pick(3), pick(2)], axis=-1)


def _layout_wout(w):
    wt = jnp.swapaxes(w, -1, -2)
    wt = jnp.concatenate([wt[..., :FOURIER_W], _pad_heads(wt[..., FOURIER_W:FOURIER_W + ML_W]),
                          _pad_heads(wt[..., FOURIER_W + ML_W:])], axis=-1)
    return jnp.swapaxes(wt, -1, -2)


def _split_bf16(w):
    hi = w.astype(BF16)
    return hi, (w - hi.astype(F32)).astype(BF16)


def _dft_tables(seq):
    n1c, n2c, g = FFT_N1, seq // FFT_N1, SUBLANES
    eye = jnp.eye(g, dtype=F32)
    ar = lambda n: jnp.arange(n, dtype=jnp.int32)
    n1 = (ar(n1c // g)[:, None] * g + ar(g)[None, :])[:, None, None, :]
    kn = (ar(n2c)[None, :, None, None] * (n1 + n1c * ar(n2c)[None, None, :, None])) % seq
    ang = kn.astype(F32) * (2.0 * jnp.pi / seq)
    spread = lambda t: jnp.einsum('aknj,jl->akjnl', t, eye).reshape(n1c // g, n2c * g, n2c * g)
    gr, gi = spread(jnp.cos(ang) * n2c ** -0.5), spread(-jnp.sin(ang) * n2c ** -0.5)
    stage1 = jnp.concatenate([jnp.concatenate([gr, gi], axis=2),
                              jnp.concatenate([gi, -gr], axis=2)], axis=1).astype(BF16)
    phi = ((ar(n1c)[:, None] * ar(n1c)[None, :]) % n1c).astype(F32) * (2.0 * jnp.pi / n1c)
    spread2 = lambda t: jnp.einsum('kn,jl->kjln', t, eye).reshape(n1c * g, g * n1c)
    stage2 = jnp.concatenate([spread2(jnp.cos(phi) * n1c ** -0.5),
                              spread2(jnp.sin(phi) * n1c ** -0.5)], axis=1).astype(BF16)
    c = jnp.arange(FG_W, dtype=jnp.int32)
    cc = ((c[:, None] * c[None, :]) % FG_W).astype(F32) * (2.0 * jnp.pi / FG_W)
    eye = jnp.eye(N_FGROUPS, dtype=F32)
    bd_c = jnp.kron(eye, jnp.cos(cc) * FG_W ** -0.5)
    bd_s = jnp.kron(eye, jnp.sin(cc) * FG_W ** -0.5)
    chan = jnp.concatenate([bd_c, bd_s], axis=1).astype(BF16)
    return stage1, stage2, chan


def _rotary_tables(seq):
    inv = 1.0 / (ROPE_BASE ** (jnp.arange(0, DK, 2, dtype=F32) / DK))
    ang = jnp.arange(seq, dtype=F32)[:, None] * inv[None, :]
    cos, sin = jnp.cos(ang), jnp.sin(ang)
    zero = jnp.zeros_like(sin)
    heads = lambda a, b: _pad_last(jnp.tile(jnp.concatenate([a, b], axis=1), (1, HEADS)), QKP)
    return heads(cos, cos), heads(-sin, zero), heads(zero, sin)


def _retention_tables():
    idx = jnp.arange(CHUNK, dtype=F32)
    diff = idx[:, None] - idx[None, :]
    tiles, rows = [], []
    for rev in (False, True):
        offset = RET_BWD_EXP_OFFSET if rev else 0.0
        lg = jnp.log1p(-jnp.exp2(-(RET_GAMMA_EXP0 + offset) - jnp.arange(HEADS, dtype=F32)))
        lg3 = lg[:, None, None]
        if rev:
            decay = jnp.where((diff < 0)[None], jnp.exp(lg3 * jnp.maximum(-diff, 0.0)[None]), 0.0)
            w_inter = jnp.exp(lg[:, None] * (CHUNK - idx)[None, :])
            w_key = jnp.exp(lg[:, None] * idx[None, :])
        else:
            decay = jnp.where((diff >= 0)[None], jnp.exp(lg3 * jnp.maximum(diff, 0.0)[None]), 0.0)
            w_inter = jnp.exp(lg[:, None] * (idx + 1.0)[None, :])
            w_key = jnp.exp(lg[:, None] * (CHUNK - 1 - idx)[None, :])
        g_chunk = jnp.broadcast_to(jnp.exp(lg * CHUNK)[:, None], (HEADS, CHUNK))
        tiles.append(jnp.stack([decay, jnp.broadcast_to(w_inter[:, :, None], (HEADS, CHUNK, LANES))], axis=1))
        pad = lambda t: jnp.pad(t, ((0, SUBLANES - HEADS), (0, 0)))
        rows.append(jnp.stack([pad(w_key), pad(g_chunk)], axis=0))
    return jnp.stack(tiles, axis=0), jnp.stack(rows, axis=0)


def _layernorm_rows(z, g, b):
    mu = jnp.mean(z, axis=-1, keepdims=True)
    d = z - mu
    var = jnp.mean(d * d, axis=-1, keepdims=True)
    return d * lax.rsqrt(var + LN_EPS) * g + b


def _ln_kernel(x_ref, g_ref, b_ref, o_ref, ob_ref):
    y = _layernorm_rows(x_ref[...], g_ref[...], b_ref[...])
    o_ref[...] = y
    ob_ref[...] = y.astype(BF16)


def _ln(x, g, b):
    t, d = x.shape
    row = pl.BlockSpec((ROW_TILE, d), lambda i: (i, 0))
    vec = pl.BlockSpec((1, d), lambda i: (0, 0))
    return pl.pallas_call(
        _ln_kernel, grid=(t // ROW_TILE,), in_specs=[row, vec, vec], out_specs=[row, row],
        out_shape=[jax.ShapeDtypeStruct((t, d), F32), jax.ShapeDtypeStruct((t, d), BF16)],
        compiler_params=_cparams(("parallel",)), name="input_layernorm",
    )(x, g.reshape(1, d), b.reshape(1, d))


def _inproj_kernel(x_ref, w_ref, b_ref, wg_ref, bg_ref, *o_refs):
    x = x_ref[...]
    c0 = 0
    for o_ref in o_refs[:-1]:
        n = o_ref.shape[-1]
        y = jnp.dot(x, w_ref[:, c0:c0 + n], preferred_element_type=F32) + b_ref[:, c0:c0 + n]
        o_ref[...] = y.astype(o_ref.dtype)
        c0 += n
    gt = lax.dot_general(wg_ref[...], x, (((1,), (1,)), ((), ())), preferred_element_type=F32)
    o_refs[-1][...] = gt + bg_ref[...]


def _inproj(xb, w, b, wg, bg):
    t, d = xb.shape
    n_p = w.shape[1]
    const = lambda a: pl.BlockSpec(a.shape, lambda i: (0, 0))
    return pl.pallas_call(
        _inproj_kernel, grid=(t // ROW_TILE,),
        in_specs=[pl.BlockSpec((ROW_TILE, d), lambda i: (i, 0)), const(w), const(b), const(wg), const(bg)],
        out_specs=[pl.BlockSpec((ROW_TILE, n), lambda i: (i, 0)) for n in PROJ_GROUPS]
        + [pl.BlockSpec((GATE_ROWS, ROW_TILE), lambda i: (0, i))],
        out_shape=[jax.ShapeDtypeStruct((t, n), dt) for n, dt in zip(PROJ_GROUPS, PROJ_DTYPES)]
        + [jax.ShapeDtypeStruct((GATE_ROWS, t), F32)],
        compiler_params=_cparams(("parallel",)), name="input_projection",
    )(xb, w, b, wg, bg)


def _fchan_kernel(u_ref, m_ref, v_ref, w_ref):
    vw = jnp.dot(u_ref[...], m_ref[...], preferred_element_type=F32)
    v_ref[...] = vw[:, :FOURIER_W]
    w_ref[...] = vw[:, FOURIER_W:]


def _fourier_channels(u, chan):
    t = u.shape[0]
    row = pl.BlockSpec((ROW_TILE, FOURIER_W), lambda i: (i, 0))
    return pl.pallas_call(
        _fchan_kernel, grid=(t // ROW_TILE,),
        in_specs=[row, pl.BlockSpec((FOURIER_W, 2 * FOURIER_W), lambda i: (0, 0))],
        out_specs=[row, row], out_shape=[jax.ShapeDtypeStruct((t, FOURIER_W), F32)] * 2,
        compiler_params=_cparams(("parallel",)), name="fourier_channel_dft",
    )(u, chan)


def _fstage1_kernel(m_ref, v_ref, w_ref, br_ref, bi_ref):
    rows = v_ref.shape[1] * SUBLANES
    x = jnp.concatenate([v_ref[0].reshape(rows, FOURIER_W), w_ref[0].reshape(rows, FOURIER_W)], axis=0)
    y = jnp.dot(m_ref[0], x.astype(BF16), preferred_element_type=F32)
    br_ref[0] = y[:rows].reshape(br_ref.shape[1:])
    bi_ref[0] = y[rows:].reshape(bi_ref.shape[1:])


def _fstage2_kernel(m_ref, br_ref, bi_ref, o_ref):
    rows = br_ref.shape[2] * SUBLANES
    x = jnp.concatenate([br_ref[0].reshape(rows, FOURIER_W), bi_ref[0].reshape(rows, FOURIER_W)], axis=0)
    y = jnp.dot(m_ref[...], x.astype(BF16), preferred_element_type=F32)
    o_ref[0] = y.reshape(o_ref.shape[1:])


def _fourier_sequence(stage1, stage2, v, w, batch, seq):
    n1c, n2c, g = FFT_N1, seq // FFT_N1, SUBLANES
    view = lambda a: a.reshape(batch, n2c, n1c, FOURIER_W)
    blk1 = pl.BlockSpec((1, n2c, g, FOURIER_W), lambda a, b: (b, 0, a, 0))
    shape1 = jax.ShapeDtypeStruct((batch, n2c, n1c, FOURIER_W), F32)
    br, bi = pl.pallas_call(
        _fstage1_kernel, grid=(n1c // g, batch),
        in_specs=[pl.BlockSpec((1,) + stage1.shape[1:], lambda a, b: (a, 0, 0)), blk1, blk1],
        out_specs=[blk1, blk1], out_shape=[shape1, shape1],
        compiler_params=_cparams(("parallel", "parallel")), name="fourier_sequence_stage1",
    )(stage1, view(v), view(w))
    blk2 = pl.BlockSpec((1, g, n1c, FOURIER_W), lambda kb, b: (b, kb, 0, 0))
    y = pl.pallas_call(
        _fstage2_kernel, grid=(n2c // g, batch),
        in_specs=[pl.BlockSpec(stage2.shape, lambda kb, b: (0, 0)), blk2, blk2],
        out_specs=pl.BlockSpec((1, n1c, g, FOURIER_W), lambda kb, b: (b, 0, kb, 0)),
        out_shape=jax.ShapeDtypeStruct((batch, n1c, n2c, FOURIER_W), F32),
        compiler_params=_cparams(("parallel", "parallel")), name="fourier_sequence_stage2",
    )(stage2, br, bi)
    return y.reshape(batch * seq, FOURIER_W)


def _dot(a, b):
    return jnp.dot(a, b, preferred_element_type=F32)


def _cumsum_dot(tri, x, tri_left):
    hi = x.astype(BF16)
    r1 = x - hi.astype(F32)
    mid = r1.astype(BF16)
    lo = (r1 - mid.astype(F32)).astype(BF16)
    if tri_left:
        return _dot(tri, hi) + _dot(tri, mid) + _dot(tri, lo)
    return _dot(hi, tri) + _dot(mid, tri) + _dot(lo, tri)


def _log_sigmoid(x):
    return jnp.minimum(x, 0.0) - jnp.log1p(jnp.exp(-jnp.abs(x)))


def _mixer_prep_kernel(mqk_ref, prev_ref, next_ref, rqk_ref, rc_ref, rs1_ref, rs2_ref, convw_ref, g_ref,
                       mq_ref, mkt_ref, rq_ref, rkt_ref, gw_ref):
    i = pl.program_id(1)
    n = CHUNK
    r2 = lax.broadcasted_iota(jnp.int32, (n, n), 0)
    c2 = lax.broadcasted_iota(jnp.int32, (n, n), 1)
    groups = []
    for rev in (False, True):
        r0 = 2 * SUBLANES if rev else 0
        tri = jnp.where((c2 <= r2) if rev else (c2 >= r2), 1.0, 0.0).astype(BF16)
        a = _cumsum_dot(tri, _log_sigmoid(g_ref[r0:r0 + SUBLANES, :]), False)
        groups += [g_ref[r0 + SUBLANES:r0 + 2 * SUBLANES, :] - a, a]
    gw_ref[0] = jnp.concatenate(groups, axis=0)

    qk = mqk_ref[0]
    rows = lax.broadcasted_iota(jnp.int32, (n, 2 * QKP), 0)
    prev_row = prev_ref[0, 0, SUBLANES - 1:SUBLANES, :] * jnp.where(i == 0, 0.0, 1.0)
    next_row = next_ref[0, 0, 0:1, :] * jnp.where(i == pl.num_programs(1) - 1, 0.0, 1.0)
    xm1 = jnp.where(rows == 0, prev_row, pltpu.roll(qk, 1, 0))
    xp1 = jnp.where(rows == n - 1, next_row, pltpu.roll(qk, n - 1, 0))
    conv = xm1 * convw_ref[0:1, :] + qk * convw_ref[1:2, :] + xp1 * convw_ref[2:3, :]
    act = conv * jax.nn.sigmoid(conv)
    mq_ref[0] = act[:, :QKP].astype(BF16)
    mkt_ref[0, 0] = (act[:, QKP:] * DK ** -0.5).T.astype(BF16)

    rc, rs1, rs2 = rc_ref[...], rs1_ref[...], rs2_ref[...]
    half = DK // 2

    def rotate(t):
        return t * rc + pltpu.roll(t, QKP - half, 1) * rs1 + pltpu.roll(t, half, 1) * rs2

    rqk = rqk_ref[0]
    rq_ref[0] = rotate(rqk[:, :QKP]).astype(BF16)
    rkt_ref[0, 0] = (rotate(rqk[:, QKP:]) * DK ** -0.5).T.astype(BF16)


def _mixer_prep(mqk, rqk, rc, rs1, rs2, convw, gates, batch, seq):
    n = seq // CHUNK
    g8 = CHUNK // SUBLANES
    mqk3 = mqk.reshape(batch, seq, 2 * QKP)
    mqk8 = mqk.reshape(batch, seq // SUBLANES, SUBLANES, 2 * QKP)
    rqk3 = rqk.reshape(batch, seq, 2 * QKP)
    chunk = lambda w: pl.BlockSpec((1, CHUNK, w), lambda b, i: (b, i, 0))
    halo = lambda index: pl.BlockSpec((1, 1, SUBLANES, 2 * QKP), index)
    pos = pl.BlockSpec((CHUNK, QKP), lambda b, i: (i, 0))
    kt = pl.BlockSpec((1, 1, QKP, CHUNK), lambda b, i: (b, i, 0, 0))
    q_shape = jax.ShapeDtypeStruct((batch, seq, QKP), BF16)
    kt_shape = jax.ShapeDtypeStruct((batch, n, QKP, CHUNK), BF16)
    return pl.pallas_call(
        _mixer_prep_kernel, grid=(batch, n),
        in_specs=[chunk(2 * QKP),
                  halo(lambda b, i: (b, jnp.maximum(i * g8 - 1, 0), 0, 0)),
                  halo(lambda b, i: (b, jnp.minimum((i + 1) * g8, seq // SUBLANES - 1), 0, 0)),
                  chunk(2 * QKP), pos, pos, pos,
                  pl.BlockSpec((CONV_W, 2 * QKP), lambda b, i: (0, 0)),
                  pl.BlockSpec((GATE_ROWS, CHUNK), lambda b, i: (0, b * n + i))],
        out_specs=[chunk(QKP), kt, chunk(QKP), kt,
                   pl.BlockSpec((1, GATE_ROWS, CHUNK), lambda b, i: (b * n + i, 0, 0))],
        out_shape=[q_shape, kt_shape, q_shape, kt_shape,
                   jax.ShapeDtypeStruct((batch * n, GATE_ROWS, CHUNK), F32)],
        compiler_params=_cparams(("parallel", "parallel")), name="mixer_qk_prep",
    )(mqk3, mqk8, mqk8, rqk3, rc, rs1, rs2, convw, gates)


def _head_key_blocks(kt):
    rows = lax.broadcasted_iota(jnp.int32, kt.shape, 0)
    zero = jnp.zeros_like(kt)
    return jnp.concatenate([jnp.where((rows >= DK * h) & (rows < DK * (h + 1)), kt, zero)
                            for h in range(HEADS)], axis=1)


def _mlstm_direction(rev, q, kt, v, gw, c_ref, cb_ref, m_ref, d):
    n = CHUNK
    r0 = 2 * SUBLANES if rev else 0
    key_w = gw[r0:r0 + SUBLANES, :]
    a = gw[r0 + SUBLANES:r0 + 2 * SUBLANES, :]
    r2 = lax.broadcasted_iota(jnp.int32, (n, n), 0)
    c2 = lax.broadcasted_iota(jnp.int32, (n, n), 1)
    causal = (c2 >= r2) if rev else (c2 <= r2)
    lane = lax.broadcasted_iota(jnp.int32, (n, LANES), 1)

    cmax = jnp.zeros((n, LANES), F32)
    for h in range(HEADS):
        cm_h = jnp.max(jnp.where(causal, key_w[h:h + 1, :], -jnp.inf), axis=1, keepdims=True)
        cmax = jnp.where(lane == DEN_COL + h, cm_h, cmax)
    zero = jnp.zeros((SUBLANES, n), F32)
    den_group = DEN_COL // SUBLANES
    a_col = jnp.concatenate([zero] * den_group + [a] + [zero] * (n // SUBLANES - den_group - 1), axis=0).T
    m_lane = m_ref[d, 1, 0:1, :]
    mm = jnp.maximum(m_lane, cmax)
    s_inter = jnp.exp(m_lane - mm)
    e_negm = jnp.exp(-a_col - mm)

    s_all = _dot(q, _head_key_blocks(kt))
    inter_all = _dot(q, cb_ref[d])
    rs = []
    for h in range(HEADS):
        blk = slice(DVP * h, DVP * (h + 1))
        col = slice(DEN_COL + h, DEN_COL + h + 1)
        p = jnp.exp(jnp.where(causal, key_w[h:h + 1, :] - mm[:, col], -jnp.inf))
        scores = (s_all[:, blk] * p).astype(BF16)
        rs.append(_dot(scores, v[:, blk]) + s_inter[:, col] * inter_all[:, blk])
    den = rs[0]
    for h in range(1, HEADS):
        den = den + rs[h]
    rden = 1.0 / jnp.maximum(jnp.abs(den), e_negm)
    out = jnp.concatenate([rs[h] * rden[:, DEN_COL + h:DEN_COL + h + 1] for h in range(HEADS)], axis=1)

    m_prev = m_ref[d, 0]
    a_end = jnp.broadcast_to(a[:, 0:1] if rev else a[:, n - 1:n], (SUBLANES, n))
    w_key = a_end + key_w
    m_chunk = jnp.broadcast_to(jnp.max(w_key, axis=1, keepdims=True), (SUBLANES, n))
    m_new = jnp.maximum(a_end + m_prev, m_chunk)
    s_prev = jnp.exp(a_end + m_prev - m_new)
    p_key = jnp.exp(w_key - m_chunk) * jnp.exp(m_chunk - m_new)
    m_ref[d, 0] = m_new
    row8 = lax.broadcasted_iota(jnp.int32, (SUBLANES, n), 0)
    lane8 = lax.broadcasted_iota(jnp.int32, (SUBLANES, n), 1)
    on_diag = (lane8 == row8 + DEN_COL) & (row8 < HEADS)
    m_ref[d, 1] = jnp.broadcast_to(jnp.sum(jnp.where(on_diag, m_new, 0.0), axis=0, keepdims=True),
                                   (SUBLANES, n))
    for h in range(HEADS):
        blk = slice(DVP * h, DVP * (h + 1))
        keys = slice(DK * h, DK * (h + 1))
        kp = (kt[keys, :].astype(F32) * p_key[h:h + 1, :]).astype(BF16)
        c_new = s_prev[h:h + 1, :] * c_ref[d, h] + _dot(kp, v[:, blk])
        c_ref[d, h] = c_new
        cb_ref[d, keys, blk] = c_new.astype(BF16)
    return out


def _mlstm_kernel(q_f, kt_f, v_f, g_f, q_b, kt_b, v_b, g_b, hf_ref, hb_ref, c_ref, cb_ref, m_ref):
    @pl.when(pl.program_id(1) == 0)
    def _():
        c_ref[...] = jnp.zeros_like(c_ref)
        cb_ref[...] = jnp.zeros_like(cb_ref)
        m_ref[...] = jnp.zeros_like(m_ref)

    hf_ref[0] = _mlstm_direction(False, q_f[0], kt_f[0, 0], v_f[0], g_f[0], c_ref, cb_ref, m_ref, 0
                                 ).astype(hf_ref.dtype)
    hb_ref[0] = _mlstm_direction(True, q_b[0], kt_b[0, 0], v_b[0], g_b[0], c_ref, cb_ref, m_ref, 1
                                 ).astype(hb_ref.dtype)


def _mixer_specs(batch, seq):
    n = seq // CHUNK
    fwd = lambda b, i: (b, i, 0)
    bwd = lambda b, i: (b, n - 1 - i, 0)
    blk = lambda w, im: pl.BlockSpec((1, CHUNK, w), im)
    ktf = pl.BlockSpec((1, 1, QKP, CHUNK), lambda b, i: (b, i, 0, 0))
    ktb = pl.BlockSpec((1, 1, QKP, CHUNK), lambda b, i: (b, n - 1 - i, 0, 0))
    return n, fwd, bwd, blk, ktf, ktb


def _mlstm(q, kt, v, gw, batch, seq):
    n, fwd, bwd, blk, ktf, ktb = _mixer_specs(batch, seq)
    v3 = v.reshape(batch, seq, HW)
    gf = pl.BlockSpec((1, GATE_ROWS, CHUNK), lambda b, i: (b * n + i, 0, 0))
    gb = pl.BlockSpec((1, GATE_ROWS, CHUNK), lambda b, i: (b * n + n - 1 - i, 0, 0))
    return pl.pallas_call(
        _mlstm_kernel, grid=(batch, n),
        in_specs=[blk(QKP, fwd), ktf, blk(HW, fwd), gf, blk(QKP, bwd), ktb, blk(HW, bwd), gb],
        out_specs=[blk(HW, fwd), blk(HW, bwd)],
        out_shape=[jax.ShapeDtypeStruct((batch, seq, HW), BF16)] * 2,
        scratch_shapes=[pltpu.VMEM((2, HEADS, DK, DVP), F32), pltpu.VMEM((2, QKP, HW), BF16),
                        pltpu.VMEM((2, 2, SUBLANES, CHUNK), F32)],
        compiler_params=_cparams(("parallel", "arbitrary")), name="mlstm_mixer",
    )(q, kt, v3, gw, q, kt, v3, gw)


def _retention_direction(q, kt, v, tile_ref, row_ref, s_ref, sb_ref, d):
    s_all = _dot(q, _head_key_blocks(kt))
    inter_all = _dot(q, sb_ref[d])
    outs = []
    for h in range(HEADS):
        blk = slice(DVP * h, DVP * (h + 1))
        scores = (s_all[:, blk] * tile_ref[d, h, 0]).astype(BF16)
        outs.append(_dot(scores, v[:, blk]) + tile_ref[d, h, 1] * inter_all[:, blk])
    for h in range(HEADS):
        blk = slice(DVP * h, DVP * (h + 1))
        keys = slice(DK * h, DK * (h + 1))
        kp = (kt[keys, :].astype(F32) * row_ref[d, 0, h:h + 1, :]).astype(BF16)
        s_new = row_ref[d, 1, h:h + 1, :] * s_ref[d, h] + _dot(kp, v[:, blk])
        s_ref[d, h] = s_new
        sb_ref[d, keys, blk] = s_new.astype(BF16)
    return jnp.concatenate(outs, axis=1)


def _retention_kernel(q_f, kt_f, v_f, q_b, kt_b, v_b, tile_ref, row_ref, yf_ref, yb_ref, s_ref, sb_ref):
    @pl.when(pl.program_id(1) == 0)
    def _():
        s_ref[...] = jnp.zeros_like(s_ref)
        sb_ref[...] = jnp.zeros_like(sb_ref)

    yf_ref[0] = _retention_direction(q_f[0], kt_f[0, 0], v_f[0], tile_ref, row_ref, s_ref, sb_ref, 0
                                     ).astype(yf_ref.dtype)
    yb_ref[0] = _retention_direction(q_b[0], kt_b[0, 0], v_b[0], tile_ref, row_ref, s_ref, sb_ref, 1
                                     ).astype(yb_ref.dtype)


def _retention(q, kt, v, tiles, rows, batch, seq):
    n, fwd, bwd, blk, ktf, ktb = _mixer_specs(batch, seq)
    v3 = v.reshape(batch, seq, HW)
    return pl.pallas_call(
        _retention_kernel, grid=(batch, n),
        in_specs=[blk(QKP, fwd), ktf, blk(HW, fwd), blk(QKP, bwd), ktb, blk(HW, bwd),
                  pl.BlockSpec(tiles.shape, lambda b, i: (0, 0, 0, 0, 0)),
                  pl.BlockSpec(rows.shape, lambda b, i: (0, 0, 0, 0))],
        out_specs=[blk(HW, fwd), blk(HW, bwd)],
        out_shape=[jax.ShapeDtypeStruct((batch, seq, HW), BF16)] * 2,
        scratch_shapes=[pltpu.VMEM((2, HEADS, DK, DVP), F32), pltpu.VMEM((2, QKP, HW), BF16)],
        compiler_params=_cparams(("parallel", "arbitrary")), name="retention_mixer",
    )(q, kt, v3, q, kt, v3, tiles, rows)


def _head_norm(z, w):
    valid = lax.broadcasted_iota(jnp.int32, (1, DVP), 1) < DV
    outs = []
    for h in range(HEADS):
        zh = z[:, DVP * h:DVP * (h + 1)]
        mu = jnp.sum(jnp.where(valid, zh, 0.0), axis=1, keepdims=True) * (1.0 / DV)
        d = jnp.where(valid, zh - mu, 0.0)
        var = jnp.sum(d * d, axis=1, keepdims=True) * (1.0 / DV)
        outs.append(d * lax.rsqrt(var + LN_EPS))
    return jnp.concatenate(outs, axis=1) * w


def _outproj_kernel(alpha, x_ref, yf_ref, hf_ref, hb_ref, o_ref, rf_ref, rb_ref, rg_ref, mlw_ref, rtw_ref,
                    w_ref, g_ref, b_ref, wrh_ref, wrl_ref, br_ref, x1_ref, x1b_ref, lg_ref):
    f32 = lambda ref: ref[...].astype(F32)
    y_m = _head_norm(jax.nn.sigmoid(f32(o_ref)) * (f32(hf_ref) + f32(hb_ref)), mlw_ref[...])
    rg = f32(rg_ref)
    y_r = rg * jax.nn.sigmoid(rg) * _head_norm(f32(rf_ref) + f32(rb_ref), rtw_ref[...])
    cat = jnp.concatenate([yf_ref[...], y_m, y_r], axis=1).astype(BF16)
    mix = _dot(cat, w_ref[...])
    x1 = _layernorm_rows(alpha * x_ref[...] + mix, g_ref[...], b_ref[...])
    x1_ref[...] = x1
    hi = x1.astype(BF16)
    x1b_ref[...] = hi
    lo = (x1 - hi.astype(F32)).astype(BF16)
    lg_ref[...] = _dot(hi, wrh_ref[...]) + _dot(lo, wrh_ref[...]) + _dot(hi, wrl_ref[...]) + br_ref[...]


def _outproj(x, yf, hf, hb, mo, rf, rb, rg, mlw, rtw, w, g, b, wrh, wrl, br, alpha):
    t, d = x.shape
    row = lambda n: pl.BlockSpec((ROW_TILE, n), lambda i: (i, 0))
    const = lambda a: pl.BlockSpec(a.shape, lambda i: (0, 0))
    args = (x, yf, hf, hb, mo, rf, rb, rg, mlw, rtw, w, g, b, wrh, wrl, br)
    in_specs = [row(d), row(FOURIER_W),
                row(HW), row(HW), row(HW), row(HW), row(HW), row(HW)] + [const(a) for a in args[8:]]
    return pl.pallas_call(
        functools.partial(_outproj_kernel, alpha), grid=(t // ROW_TILE,), in_specs=in_specs,
        out_specs=[row(d), row(d), row(LANES)],
        out_shape=[jax.ShapeDtypeStruct((t, d), F32), jax.ShapeDtypeStruct((t, d), BF16),
                   jax.ShapeDtypeStruct((t, LANES), F32)],
        compiler_params=_cparams(("parallel",)), name="mixer_output_projection",
    )(*args)


def _expert_kernel(be_ref, nb_ref, x_ref, w1_ref, b1_ref, w2_ref, b2_ref, o_ref, w1b_ref, w2b_ref):
    i = pl.program_id(0)

    @pl.when(jnp.logical_or(i == 0, be_ref[i] != be_ref[jnp.maximum(i - 1, 0)]))
    def _():
        w1b_ref[...] = w1_ref[0].astype(BF16)
        w2b_ref[...] = w2_ref[0].astype(BF16)

    @pl.when(i < nb_ref[0])
    def _():
        hc = _dot(x_ref[...], w1b_ref[...]) + b1_ref[0]
        gate = jnp.minimum(hc[:, :D_FF], SWIGLU_LIMIT)
        up = jnp.clip(hc[:, D_FF:], -SWIGLU_LIMIT, SWIGLU_LIMIT)
        glu = gate * jax.nn.sigmoid(SWIGLU_ALPHA * gate)
        y = _dot(((up + 1.0) * glu).astype(BF16), w2b_ref[...]) + b2_ref[0]
        o_ref[...] = y.astype(o_ref.dtype)

    @pl.when(i >= nb_ref[0])
    def _():
        o_ref[...] = jnp.zeros_like(o_ref)


def _experts(block_e, n_blocks, xs, w1, b1, w2, b2, layer):
    p, d = xs.shape
    off = layer * N_EXPERTS
    wmap = lambda i, be, nb: (off + be[i], 0, 0)
    xmap = lambda i, be, nb: (jnp.minimum(i, nb[0] - 1), 0)
    grid_spec = pltpu.PrefetchScalarGridSpec(
        num_scalar_prefetch=2, grid=(p // MOE_BLOCK,),
        in_specs=[pl.BlockSpec((MOE_BLOCK, d), xmap),
                  pl.BlockSpec((1, d, 2 * D_FF), wmap), pl.BlockSpec((1, 1, 2 * D_FF), wmap),
                  pl.BlockSpec((1, D_FF, d), wmap), pl.BlockSpec((1, 1, d), wmap)],
        out_specs=pl.BlockSpec((MOE_BLOCK, d), lambda i, be, nb: (i, 0)),
        scratch_shapes=[pltpu.VMEM((d, 2 * D_FF), BF16), pltpu.VMEM((D_FF, d), BF16)])
    return pl.pallas_call(
        _expert_kernel, grid_spec=grid_spec, out_shape=jax.ShapeDtypeStruct((p, d), BF16),
        compiler_params=_cparams(("arbitrary",)), name="routed_experts",
    )(block_e, n_blocks, xs, w1, b1, w2, b2)


def _route_kernel(lg_ref, gate_ref, eid_ref, pos_ref, cnt_ref, run_ref, pst_ref):
    ph = pl.program_id(0)
    i = pl.program_id(1)
    tm = lg_ref.shape[0]
    lane = lax.broadcasted_iota(jnp.int32, (tm, LANES), 1)
    x = jnp.where(lane < N_EXPERTS, lg_ref[...], -jnp.inf)
    vals, ids = [], []
    onehot = jnp.zeros((tm, LANES), F32)
    for _ in range(TOP_K):
        m = jnp.max(x, axis=1, keepdims=True)
        idx = jnp.min(jnp.where(x == m, lane, LANES), axis=1, keepdims=True)
        sel = lane == idx
        onehot = jnp.where(sel, 1.0, onehot)
        x = jnp.where(sel, -jnp.inf, x)
        vals.append(m)
        ids.append(idx)
    tile_counts = jnp.sum(onehot, axis=0, keepdims=True)

    @pl.when(jnp.logical_and(ph == 0, i == 0))
    def _():
        run_ref[...] = jnp.zeros_like(run_ref)

    @pl.when(ph == 0)
    def _():
        run_ref[...] += tile_counts

    @pl.when(jnp.logical_and(ph == 1, i == 0))
    def _():
        sizes = run_ref[...]
        cnt_ref[...] = sizes
        psizes = jnp.ceil(sizes * (1.0 / MOE_BLOCK)) * MOE_BLOCK
        r2 = lax.broadcasted_iota(jnp.int32, (LANES, LANES), 0)
        c2 = lax.broadcasted_iota(jnp.int32, (LANES, LANES), 1)
        incl = jnp.where(r2 <= c2, 1.0, 0.0).astype(BF16)
        pst_ref[...] = _cumsum_dot(incl, psizes, False) - psizes
        run_ref[...] = jnp.zeros_like(run_ref)

    @pl.when(ph == 1)
    def _():
        rr = lax.broadcasted_iota(jnp.int32, (tm, tm), 0)
        cc = lax.broadcasted_iota(jnp.int32, (tm, tm), 1)
        strict = jnp.where(cc < rr, 1.0, 0.0).astype(BF16)
        base = _dot(strict, onehot.astype(BF16)) + run_ref[0:1, :] + pst_ref[0:1, :]
        denom = jnp.ones_like(vals[0])
        for k in range(1, TOP_K):
            denom = denom + jnp.exp(vals[k] - vals[0])
        gates = jnp.zeros((tm, LANES), F32)
        eids = jnp.zeros((tm, LANES), jnp.int32)
        poss = jnp.zeros((tm, LANES), jnp.int32)
        for k in range(TOP_K):
            pk = jnp.sum(jnp.where(lane == ids[k], base, 0.0), axis=1, keepdims=True)
            gates = jnp.where(lane == k, jnp.exp(vals[k] - vals[0]) / denom, gates)
            eids = jnp.where(lane == k, ids[k], eids)
            poss = jnp.where(lane == k, pk.astype(jnp.int32), poss)
        gate_ref[...] = gates
        eid_ref[...] = eids
        pos_ref[...] = poss
        run_ref[...] += tile_counts


def _route(logits):
    t = logits.shape[0]
    tile = pl.BlockSpec((ROW_TILE, LANES), lambda ph, i: (i, 0))
    out_tile = pl.BlockSpec((ROW_TILE, LANES), lambda ph, i: (i * ph, 0))
    gates, eids, poss, cnt = pl.pallas_call(
        _route_kernel, grid=(2, t // ROW_TILE), in_specs=[tile],
        out_specs=[out_tile, out_tile, out_tile, pl.BlockSpec((SUBLANES, LANES), lambda ph, i: (0, 0))],
        out_shape=[jax.ShapeDtypeStruct((t, LANES), F32), jax.ShapeDtypeStruct((t, LANES), jnp.int32),
                   jax.ShapeDtypeStruct((t, LANES), jnp.int32), jax.ShapeDtypeStruct((SUBLANES, LANES), F32)],
        scratch_shapes=[pltpu.VMEM((SUBLANES, LANES), F32), pltpu.VMEM((SUBLANES, LANES), F32)],
        compiler_params=_cparams(("arbitrary", "arbitrary")), name="router_topk",
    )(logits)
    n = t * TOP_K
    p = n + N_EXPERTS * MOE_BLOCK
    nb = p // MOE_BLOCK
    sizes = cnt[0, :N_EXPERTS].astype(jnp.int32)
    psizes = (sizes + MOE_BLOCK - 1) // MOE_BLOCK * MOE_BLOCK
    pends = jnp.cumsum(psizes)
    pstarts = pends - psizes
    starts = jnp.cumsum(sizes) - sizes
    block_e = jnp.minimum(jnp.searchsorted(pends, jnp.arange(nb, dtype=jnp.int32) * MOE_BLOCK, side='right'),
                          N_EXPERTS - 1).astype(jnp.int32)
    n_blocks = (pends[-1] // MOE_BLOCK).astype(jnp.int32).reshape(1)
    order = jnp.argsort(eids[:, :TOP_K].reshape(n), stable=True).astype(jnp.int32)
    per_row = lambda per_block: jnp.repeat(per_block, MOE_BLOCK)
    local = jnp.arange(p, dtype=jnp.int32) - per_row(pstarts[block_e])
    size_r = per_row(sizes[block_e])
    pair = jnp.take(order, jnp.clip(per_row(starts[block_e]) + local, 0, n - 1))
    src_tok = jnp.where(local < size_r, pair // TOP_K, 0)
    return gates, poss[:, :TOP_K], src_tok, block_e, n_blocks


def _combine_kernel(alpha, x_ref, y_ref, gate_ref, g_ref, b_ref, o_ref, ob_ref):
    gates = gate_ref[...]
    moe = gates[:, 0:1] * y_ref[0].astype(F32)
    for k in range(1, TOP_K):
        moe = moe + gates[:, k:k + 1] * y_ref[k].astype(F32)
    y = _layernorm_rows(alpha * x_ref[...] + moe, g_ref[...], b_ref[...])
    o_ref[...] = y
    ob_ref[...] = y.astype(BF16)


def _combine(x, yk, gates, g, b, alpha):
    t, d = x.shape
    row = pl.BlockSpec((ROW_TILE, d), lambda i: (i, 0))
    vec = pl.BlockSpec((1, d), lambda i: (0, 0))
    return pl.pallas_call(
        functools.partial(_combine_kernel, alpha), grid=(t // ROW_TILE,),
        in_specs=[row, pl.BlockSpec((TOP_K, ROW_TILE, d), lambda i: (0, i, 0)),
                  pl.BlockSpec((ROW_TILE, LANES), lambda i: (i, 0)), vec, vec],
        out_specs=[row, row],
        out_shape=[jax.ShapeDtypeStruct((t, d), F32), jax.ShapeDtypeStruct((t, d), BF16)],
        compiler_params=_cparams(("parallel",)), name="expert_combine_layernorm",
    )(x, yk, gates, g.reshape(1, d), b.reshape(1, d))


def kernel(x, emb_ln_g, emb_ln_b, w_in, b_in, conv_w, ml_norm_w, ret_norm_w, w_out, ln1_g, ln1_b,
           w_router, b_router, w1, b1, w2, b2, ln2_g, ln2_b):
    batch, seq, d = x.shape
    depth = w_in.shape[0]
    assert d == D_MODEL and seq % ROW_TILE == 0
    t = batch * seq
    alpha = (2.0 * depth) ** 0.25

    w_in_p = _layout_proj(w_in).astype(BF16)
    den_cols = jnp.array([MV_OFFSET + DVP * h + DEN_COL + h for h in range(HEADS)])
    b_in_p = _layout_proj(b_in).at[:, den_cols].set(1.0)[:, None, :]
    wg_p = jnp.swapaxes(_layout_gates(w_in), -1, -2).astype(BF16)
    bg_p = _layout_gates(b_in)[:, :, None]
    hk = HEADS * DK
    conv_p = jnp.concatenate([_pad_last(conv_w[..., :hk], QKP), _pad_last(conv_w[..., hk:], QKP)], axis=-1)
    mlw_p = _pad_heads(ml_norm_w)[:, None, :]
    rtw_p = _pad_heads(ret_norm_w)[:, None, :]
    w_out_p = _layout_wout(w_out).astype(BF16)
    wr_hi, wr_lo = _split_bf16(_pad_last(w_router, LANES))
    br_p = _pad_last(b_router, LANES)[:, None, :]
    w1_r = w1.reshape(depth * N_EXPERTS, d, 2 * D_FF)
    w2_r = w2.reshape(depth * N_EXPERTS, D_FF, d)
    b1_r = b1.reshape(depth * N_EXPERTS, 1, 2 * D_FF)
    b2_r = b2.reshape(depth * N_EXPERTS, 1, d)

    stage1, stage2, chan = _dft_tables(seq)
    rc, rs1, rs2 = _rotary_tables(seq)
    ret_tiles, ret_rows = _retention_tables()

    xf, xb = _ln(x.reshape(t, d), emb_ln_g, emb_ln_b)
    for l in range(depth):
        pf, mqk, mv, mo, rqk, rv, rg, gates_t = _inproj(xb, w_in_p[l], b_in_p[l], wg_p[l], bg_p[l])
        zv, zw = _fourier_channels(pf, chan)
        yf = _fourier_sequence(stage1, stage2, zv, zw, batch, seq)
        mq, mkt, rq, rkt, gw = _mixer_prep(mqk, rqk, rc, rs1, rs2, conv_p[l], gates_t, batch, seq)
        hf, hb = _mlstm(mq, mkt, mv, gw, batch, seq)
        rf, rb = _retention(rq, rkt, rv, ret_tiles, ret_rows, batch, seq)
        x1, x1b, logits = _outproj(
            xf, yf, hf.reshape(t, HW), hb.reshape(t, HW), mo, rf.reshape(t, HW), rb.reshape(t, HW), rg,
            mlw_p[l], rtw_p[l], w_out_p[l], ln1_g[l][None, :], ln1_b[l][None, :],
            wr_hi[l], wr_lo[l], br_p[l], alpha)
        gates, pos, src_tok, block_e, n_blocks = _route(logits)
        xs = jnp.take(x1b, src_tok, axis=0)
        ys = _experts(block_e, n_blocks, xs, w1_r, b1_r, w2_r, b2_r, l)
        yk = jnp.take(ys, pos.T.reshape(TOP_K * t), axis=0).reshape(TOP_K, t, d)
        xf, xb = _combine(x1, yk, gates, ln2_g[l], ln2_b[l], alpha)
    return xf.reshape(batch, seq, d)
```

```python
import functools

import jax
import jax.numpy as jnp
from jax import lax
from jax.experimental import pallas as pl
from jax.experimental.pallas import tpu as pltpu

F32 = jnp.float32
BF16 = jnp.bfloat16

D_MODEL = 1024
CHUNK = 128
FOURIER_W = D_MODEL // 4
N_FGROUPS = 4
FG_W = FOURIER_W // N_FGROUPS
ML_W = 3 * D_MODEL // 8
RET_W = D_MODEL - FOURIER_W - ML_W
HEADS = 4
DV = ML_W // HEADS
DK = DV // 2
CONV_W = 3
ROPE_BASE = 10000.0
RET_GAMMA_EXP0 = 5.0
RET_BWD_EXP_OFFSET = 0.5
N_EXPERTS = 32
TOP_K = 4
D_FF = D_MODEL
SWIGLU_ALPHA = 1.702
SWIGLU_LIMIT = 7.0
LN_EPS = 1e-5

COL_F = 0
COL_MQK = COL_F + FOURIER_W
COL_MV = COL_MQK + 2 * HEADS * DK
COL_MO = COL_MV + ML_W
COL_MG = COL_MO + ML_W
COL_RQ = COL_MG + 4 * HEADS
COL_RK = COL_RQ + HEADS * DK
COL_RV = COL_RK + HEADS * DK
COL_RG = COL_RV + RET_W
PROJ_W = COL_RG + RET_W

LANES = 128
SUBLANES = 8
DVP = LANES
QKP = 2 * LANES
HW = HEADS * DVP
DEN_COL = DV
GATE_ROWS = 4 * SUBLANES
VMEM_LIMIT = 52 * 1024 * 1024

PROJ_GROUPS = (FOURIER_W, 2 * QKP, HW, HW, 2 * QKP, HW, HW)
PROJ_DTYPES = (BF16, F32, BF16, BF16, F32, BF16, BF16)
MV_OFFSET = FOURIER_W + 2 * QKP
MIX_P = FOURIER_W + 2 * HW

FFT_N1 = 128
ROW_TILE = 512
MOE_BLOCK = 256
MIXER_BATCH = 2

def _cparams(sem):
    return pltpu.CompilerParams(dimension_semantics=sem, vmem_limit_bytes=VMEM_LIMIT)


def _pad_last(w, n):
    return jnp.pad(w, [(0, 0)] * (w.ndim - 1) + [(0, n - w.shape[-1])])


def _pad_heads(w):
    lead = w.shape[:-1]
    w = w.reshape(*lead, HEADS, DV)
    w = jnp.pad(w, [(0, 0)] * (len(lead) + 1) + [(0, DVP - DV)])
    return w.reshape(*lead, HW)


def _pair_split(w):
    lead = w.shape[:-1]
    w = w.reshape(*lead, HEADS, DK // 2, 2)
    w = jnp.swapaxes(w, -1, -2).reshape(*lead, HEADS * DK)
    return _pad_last(w, QKP)


def _layout_proj(w):
    hk = HEADS * DK
    return jnp.concatenate([
        w[..., COL_F:COL_MQK],
        _pad_last(w[..., COL_MQK:COL_MQK + hk], QKP), _pad_last(w[..., COL_MQK + hk:COL_MV], QKP),
        _pad_heads(w[..., COL_MV:COL_MO]), _pad_heads(w[..., COL_MO:COL_MG]),
        _pair_split(w[..., COL_RQ:COL_RK]), _pair_split(w[..., COL_RK:COL_RV]),
        _pad_heads(w[..., COL_RV:COL_RG]), _pad_heads(w[..., COL_RG:PROJ_W]),
    ], axis=-1)


def _layout_gates(w):
    g = w[..., COL_MG:COL_RQ]
    pick = lambda j: _pad_last(g[..., HEADS * j:HEADS * (j + 1)], SUBLANES)
    return jnp.concatenate([pick(1), pick(0), pick(3), pick(2)], axis=-1)


def _layout_wout(w):
    wt = jnp.swapaxes(w, -1, -2)
    wt = jnp.concatenate([wt[..., :FOURIER_W], _pad_heads(wt[..., FOURIER_W:FOURIER_W + ML_W]),
                          _pad_heads(wt[..., FOURIER_W + ML_W:])], axis=-1)
    return jnp.swapaxes(wt, -1, -2)


def _split_bf16(w):
    hi = w.astype(BF16)
    return hi, (w - hi.astype(F32)).astype(BF16)


def _dft_tables(seq):
    n1c, n2c, g = FFT_N1, seq // FFT_N1, SUBLANES
    eye = jnp.eye(g, dtype=F32)
    ar = lambda n: jnp.arange(n, dtype=jnp.int32)
    n1 = (ar(n1c // g)[:, None] * g + ar(g)[None, :])[:, None, None, :]
    kn = (ar(n2c)[None, :, None, None] * (n1 + n1c * ar(n2c)[None, None, :, None])) % seq
    ang = kn.astype(F32) * (2.0 * jnp.pi / seq)
    spread = lambda t: jnp.einsum('aknj,jl->akjnl', t, eye).reshape(n1c // g, n2c * g, n2c * g)
    gr, gi = spread(jnp.cos(ang) * n2c ** -0.5), spread(-jnp.sin(ang) * n2c ** -0.5)
    stage1 = jnp.concatenate([jnp.concatenate([gr, gi], axis=2),
                              jnp.concatenate([gi, -gr], axis=2)], axis=1).astype(BF16)
    phi = ((ar(n1c)[:, None] * ar(n1c)[None, :]) % n1c).astype(F32) * (2.0 * jnp.pi / n1c)
    spread2 = lambda t: jnp.einsum('kn,jl->kjln', t, eye).reshape(n1c * g, g * n1c)
    stage2 = jnp.concatenate([spread2(jnp.cos(phi) * n1c ** -0.5),
                              spread2(jnp.sin(phi) * n1c ** -0.5)], axis=1).astype(BF16)
    c = jnp.arange(FG_W, dtype=jnp.int32)
    cc = ((c[:, None] * c[None, :]) % FG_W).astype(F32) * (2.0 * jnp.pi / FG_W)
    eye = jnp.eye(N_FGROUPS, dtype=F32)
    bd_c = jnp.kron(eye, jnp.cos(cc) * FG_W ** -0.5)
    bd_s = jnp.kron(eye, jnp.sin(cc) * FG_W ** -0.5)
    chan = jnp.concatenate([bd_c, bd_s], axis=1).astype(BF16)
    return stage1, stage2, chan


def _rotary_tables(seq):
    inv = 1.0 / (ROPE_BASE ** (jnp.arange(0, DK, 2, dtype=F32) / DK))
    ang = jnp.arange(seq, dtype=F32)[:, None] * inv[None, :]
    cos, sin = jnp.cos(ang), jnp.sin(ang)
    zero = jnp.zeros_like(sin)
    heads = lambda a, b: _pad_last(jnp.tile(jnp.concatenate([a, b], axis=1), (1, HEADS)), QKP)
    return heads(cos, cos), heads(-sin, zero), heads(zero, sin)


def _retention_tables():
    idx = jnp.arange(CHUNK, dtype=F32)
    diff = idx[:, None] - idx[None, :]
    tiles, rows = [], []
    for rev in (False, True):
        offset = RET_BWD_EXP_OFFSET if rev else 0.0
        lg = jnp.log1p(-jnp.exp2(-(RET_GAMMA_EXP0 + offset) - jnp.arange(HEADS, dtype=F32)))
        lg3 = lg[:, None, None]
        if rev:
            decay = jnp.where((diff < 0)[None], jnp.exp(lg3 * jnp.maximum(-diff, 0.0)[None]), 0.0)
            w_inter = jnp.exp(lg[:, None] * (CHUNK - idx)[None, :])
            w_key = jnp.exp(lg[:, None] * idx[None, :])
        else:
            decay = jnp.where((diff >= 0)[None], jnp.exp(lg3 * jnp.maximum(diff, 0.0)[None]), 0.0)
            w_inter = jnp.exp(lg[:, None] * (idx + 1.0)[None, :])
            w_key = jnp.exp(lg[:, None] * (CHUNK - 1 - idx)[None, :])
        g_chunk = jnp.broadcast_to(jnp.exp(lg * CHUNK)[:, None], (HEADS, CHUNK))
        tiles.append(jnp.stack([decay, jnp.broadcast_to(w_inter[:, :, None], (HEADS, CHUNK, LANES))], axis=1))
        pad = lambda t: jnp.pad(t, ((0, SUBLANES - HEADS), (0, 0)))
        rows.append(jnp.stack([pad(w_key), pad(g_chunk)], axis=0))
    return jnp.stack(tiles, axis=0), jnp.stack(rows, axis=0)


def _layernorm_rows(z, g, b):
    mu = jnp.mean(z, axis=-1, keepdims=True)
    d = z - mu
    var = jnp.mean(d * d, axis=-1, keepdims=True)
    return d * lax.rsqrt(var + LN_EPS) * g + b


def _ln_kernel(x_ref, g_ref, b_ref, o_ref, ob_ref):
    y = _layernorm_rows(x_ref[...], g_ref[...], b_ref[...])
    o_ref[...] = y
    ob_ref[...] = y.astype(BF16)


def _ln(x, g, b):
    t, d = x.shape
    row = pl.BlockSpec((ROW_TILE, d), lambda i: (i, 0))
    vec = pl.BlockSpec((1, d), lambda i: (0, 0))
    return pl.pallas_call(
        _ln_kernel, grid=(t // ROW_TILE,), in_specs=[row, vec, vec], out_specs=[row, row],
        out_shape=[jax.ShapeDtypeStruct((t, d), F32), jax.ShapeDtypeStruct((t, d), BF16)],
        compiler_params=_cparams(("parallel",)), name="input_layernorm",
    )(x, g.reshape(1, d), b.reshape(1, d))


def _inproj_kernel(x_ref, w_ref, b_ref, wg_ref, bg_ref, *o_refs):
    x = x_ref[...]
    c0 = 0
    for o_ref in o_refs[:-1]:
        n = o_ref.shape[-1]
        y = jnp.dot(x, w_ref[:, c0:c0 + n], preferred_element_type=F32) + b_ref[:, c0:c0 + n]
        o_ref[...] = y.astype(o_ref.dtype)
        c0 += n
    gt = lax.dot_general(wg_ref[...], x, (((1,), (1,)), ((), ())), preferred_element_type=F32)
    o_refs[-1][...] = gt + bg_ref[...]


def _inproj(xb, w, b, wg, bg):
    t, d = xb.shape
    n_p = w.shape[1]
    const = lambda a: pl.BlockSpec(a.shape, lambda i: (0, 0))
    return pl.pallas_call(
        _inproj_kernel, grid=(t // ROW_TILE,),
        in_specs=[pl.BlockSpec((ROW_TILE, d), lambda i: (i, 0)), const(w), const(b), const(wg), const(bg)],
        out_specs=[pl.BlockSpec((ROW_TILE, n), lambda i: (i, 0)) for n in PROJ_GROUPS]
        + [pl.BlockSpec((GATE_ROWS, ROW_TILE), lambda i: (0, i))],
        out_shape=[jax.ShapeDtypeStruct((t, n), dt) for n, dt in zip(PROJ_GROUPS, PROJ_DTYPES)]
        + [jax.ShapeDtypeStruct((GATE_ROWS, t), F32)],
        compiler_params=_cparams(("parallel",)), name="input_projection",
    )(xb, w, b, wg, bg)


def _fchan_kernel(u_ref, m_ref, v_ref, w_ref):
    vw = jnp.dot(u_ref[...], m_ref[...], preferred_element_type=F32)
    v_ref[...] = vw[:, :FOURIER_W]
    w_ref[...] = vw[:, FOURIER_W:]


def _fourier_channels(u, chan):
    t = u.shape[0]
    row = pl.BlockSpec((ROW_TILE, FOURIER_W), lambda i: (i, 0))
    return pl.pallas_call(
        _fchan_kernel, grid=(t // ROW_TILE,),
        in_specs=[row, pl.BlockSpec((FOURIER_W, 2 * FOURIER_W), lambda i: (0, 0))],
        out_specs=[row, row], out_shape=[jax.ShapeDtypeStruct((t, FOURIER_W), F32)] * 2,
        compiler_params=_cparams(("parallel",)), name="fourier_channel_dft",
    )(u, chan)


def _fstage1_kernel(m_ref, v_ref, w_ref, br_ref, bi_ref):
    rows = v_ref.shape[1] * SUBLANES
    x = jnp.concatenate([v_ref[0].reshape(rows, FOURIER_W), w_ref[0].reshape(rows, FOURIER_W)], axis=0)
    y = jnp.dot(m_ref[0], x.astype(BF16), preferred_element_type=F32)
    br_ref[0] = y[:rows].reshape(br_ref.shape[1:])
    bi_ref[0] = y[rows:].reshape(bi_ref.shape[1:])


def _fstage2_kernel(m_ref, br_ref, bi_ref, o_ref):
    rows = br_ref.shape[2] * SUBLANES
    x = jnp.concatenate([br_ref[0].reshape(rows, FOURIER_W), bi_ref[0].reshape(rows, FOURIER_W)], axis=0)
    y = jnp.dot(m_ref[...], x.astype(BF16), preferred_element_type=F32)
    o_ref[0] = y.reshape(o_ref.shape[1:])


def _fourier_sequence(stage1, stage2, v, w, batch, seq):
    n1c, n2c, g = FFT_N1, seq // FFT_N1, SUBLANES
    view = lambda a: a.reshape(batch, n2c, n1c, FOURIER_W)
    blk1 = pl.BlockSpec((1, n2c, g, FOURIER_W), lambda a, b: (b, 0, a, 0))
    shape1 = jax.ShapeDtypeStruct((batch, n2c, n1c, FOURIER_W), F32)
    br, bi = pl.pallas_call(
        _fstage1_kernel, grid=(n1c // g, batch),
        in_specs=[pl.BlockSpec((1,) + stage1.shape[1:], lambda a, b: (a, 0, 0)), blk1, blk1],
        out_specs=[blk1, blk1], out_shape=[shape1, shape1],
        compiler_params=_cparams(("parallel", "parallel")), name="fourier_sequence_stage1",
    )(stage1, view(v), view(w))
    blk2 = pl.BlockSpec((1, g, n1c, FOURIER_W), lambda kb, b: (b, kb, 0, 0))
    y = pl.pallas_call(
        _fstage2_kernel, grid=(n2c // g, batch),
        in_specs=[pl.BlockSpec(stage2.shape, lambda kb, b: (0, 0)), blk2, blk2],
        out_specs=pl.BlockSpec((1, n1c, g, FOURIER_W), lambda kb, b: (b, 0, kb, 0)),
        out_shape=jax.ShapeDtypeStruct((batch, n1c, n2c, FOURIER_W), F32),
        compiler_params=_cparams(("parallel", "parallel")), name="fourier_sequence_stage2",
    )(stage2, br, bi)
    return y.reshape(batch * seq, FOURIER_W)


def _dot(a, b):
    return jnp.dot(a, b, preferred_element_type=F32)


def _cumsum_dot(tri, x, tri_left):
    hi = x.astype(BF16)
    r1 = x - hi.astype(F32)
    mid = r1.astype(BF16)
    lo = (r1 - mid.astype(F32)).astype(BF16)
    if tri_left:
        return _dot(tri, hi) + _dot(tri, mid) + _dot(tri, lo)
    return _dot(hi, tri) + _dot(mid, tri) + _dot(lo, tri)


def _log_sigmoid(x):
    return jnp.minimum(x, 0.0) - jnp.log1p(jnp.exp(-jnp.abs(x)))


def _mixer_prep_kernel(mqk_ref, prev_ref, next_ref, rqk_ref, rc_ref, rs1_ref, rs2_ref, convw_ref, g_ref,
                       mq_ref, mkt_ref, rq_ref, rkt_ref, gw_ref):
    i = pl.program_id(1)
    rows_n = mqk_ref.shape[1]
    chunks = rows_n // CHUNK
    n = CHUNK
    r2 = lax.broadcasted_iota(jnp.int32, (n, n), 0)
    c2 = lax.broadcasted_iota(jnp.int32, (n, n), 1)
    for c in range(chunks):
        lanes = slice(c * n, (c + 1) * n)
        groups = []
        for rev in (False, True):
            r0 = 2 * SUBLANES if rev else 0
            tri = jnp.where((c2 <= r2) if rev else (c2 >= r2), 1.0, 0.0).astype(BF16)
            a = _cumsum_dot(tri, _log_sigmoid(g_ref[r0:r0 + SUBLANES, lanes]), False)
            groups += [g_ref[r0 + SUBLANES:r0 + 2 * SUBLANES, lanes] - a, a]
        gw_ref[c] = jnp.concatenate(groups, axis=0)

    qk = mqk_ref[0]
    rows = lax.broadcasted_iota(jnp.int32, (rows_n, 2 * QKP), 0)
    prev_row = prev_ref[0, 0, SUBLANES - 1:SUBLANES, :] * jnp.where(i == 0, 0.0, 1.0)
    next_row = next_ref[0, 0, 0:1, :] * jnp.where(i == pl.num_programs(1) - 1, 0.0, 1.0)
    xm1 = jnp.where(rows == 0, prev_row, pltpu.roll(qk, 1, 0))
    xp1 = jnp.where(rows == rows_n - 1, next_row, pltpu.roll(qk, rows_n - 1, 0))
    conv = xm1 * convw_ref[0:1, :] + qk * convw_ref[1:2, :] + xp1 * convw_ref[2:3, :]
    act = conv * jax.nn.sigmoid(conv)
    mq_ref[0] = act[:, :QKP].astype(BF16)

    rc, rs1, rs2 = rc_ref[...], rs1_ref[...], rs2_ref[...]
    half = DK // 2

    def rotate(t):
        return t * rc + pltpu.roll(t, QKP - half, 1) * rs1 + pltpu.roll(t, half, 1) * rs2

    rqk = rqk_ref[0]
    rq_ref[0] = rotate(rqk[:, :QKP]).astype(BF16)
    mk = act[:, QKP:] * DK ** -0.5
    rk = rotate(rqk[:, QKP:]) * DK ** -0.5
    for c in range(chunks):
        mkt_ref[0, c] = mk[c * n:(c + 1) * n, :].T.astype(BF16)
        rkt_ref[0, c] = rk[c * n:(c + 1) * n, :].T.astype(BF16)


def _mixer_prep(mqk, rqk, rc, rs1, rs2, convw, gates, batch, seq):
    n = seq // CHUNK
    tiles = seq // ROW_TILE
    chunks = ROW_TILE // CHUNK
    g8 = ROW_TILE // SUBLANES
    mqk3 = mqk.reshape(batch, seq, 2 * QKP)
    mqk8 = mqk.reshape(batch, seq // SUBLANES, SUBLANES, 2 * QKP)
    rqk3 = rqk.reshape(batch, seq, 2 * QKP)
    tile = lambda w: pl.BlockSpec((1, ROW_TILE, w), lambda b, i: (b, i, 0))
    halo = lambda index: pl.BlockSpec((1, 1, SUBLANES, 2 * QKP), index)
    pos = pl.BlockSpec((ROW_TILE, QKP), lambda b, i: (i, 0))
    kt = pl.BlockSpec((1, chunks, QKP, CHUNK), lambda b, i: (b, i, 0, 0))
    q_shape = jax.ShapeDtypeStruct((batch, seq, QKP), BF16)
    kt_shape = jax.ShapeDtypeStruct((batch, n, QKP, CHUNK), BF16)
    return pl.pallas_call(
        _mixer_prep_kernel, grid=(batch, tiles),
        in_specs=[tile(2 * QKP),
                  halo(lambda b, i: (b, jnp.maximum(i * g8 - 1, 0), 0, 0)),
                  halo(lambda b, i: (b, jnp.minimum((i + 1) * g8, seq // SUBLANES - 1), 0, 0)),
                  tile(2 * QKP), pos, pos, pos,
                  pl.BlockSpec((CONV_W, 2 * QKP), lambda b, i: (0, 0)),
                  pl.BlockSpec((GATE_ROWS, ROW_TILE), lambda b, i: (0, b * tiles + i))],
        out_specs=[tile(QKP), kt, tile(QKP), kt,
                   pl.BlockSpec((chunks, GATE_ROWS, CHUNK), lambda b, i: (b * tiles + i, 0, 0))],
        out_shape=[q_shape, kt_shape, q_shape, kt_shape,
                   jax.ShapeDtypeStruct((batch * n, GATE_ROWS, CHUNK), F32)],
        compiler_params=_cparams(("parallel", "parallel")), name="mixer_qk_prep",
    )(mqk3, mqk8, mqk8, rqk3, rc, rs1, rs2, convw, gates)


def _head_key_blocks(kt):
    rows = lax.broadcasted_iota(jnp.int32, kt.shape, 0)
    zero = jnp.zeros_like(kt)
    return jnp.concatenate([jnp.where((rows >= DK * h) & (rows < DK * (h + 1)), kt, zero)
                            for h in range(HEADS)], axis=1)


def _mlstm_direction(rev, q, kt, v, gw, c_ref, cb_ref, m_ref, d):
    n = CHUNK
    r0 = 2 * SUBLANES if rev else 0
    key_w = gw[r0:r0 + SUBLANES, :]
    a = gw[r0 + SUBLANES:r0 + 2 * SUBLANES, :]
    r2 = lax.broadcasted_iota(jnp.int32, (n, n), 0)
    c2 = lax.broadcasted_iota(jnp.int32, (n, n), 1)
    causal = (c2 >= r2) if rev else (c2 <= r2)
    lane = lax.broadcasted_iota(jnp.int32, (n, LANES), 1)

    cmax = jnp.zeros((n, LANES), F32)
    for h in range(HEADS):
        cm_h = jnp.max(jnp.where(causal, key_w[h:h + 1, :], -jnp.inf), axis=1, keepdims=True)
        cmax = jnp.where(lane == DEN_COL + h, cm_h, cmax)
    zero = jnp.zeros((SUBLANES, n), F32)
    den_group = DEN_COL // SUBLANES
    a_col = jnp.concatenate([zero] * den_group + [a] + [zero] * (n // SUBLANES - den_group - 1), axis=0).T
    m_lane = m_ref[d, 1, 0:1, :]
    mm = jnp.maximum(m_lane, cmax)
    s_inter = jnp.exp(m_lane - mm)
    e_negm = jnp.exp(-a_col - mm)

    s_all = _dot(q, _head_key_blocks(kt))
    inter_all = _dot(q, cb_ref[d])
    rs = []
    for h in range(HEADS):
        blk = slice(DVP * h, DVP * (h + 1))
        col = slice(DEN_COL + h, DEN_COL + h + 1)
        p = jnp.exp(jnp.where(causal, key_w[h:h + 1, :] - mm[:, col], -jnp.inf))
        scores = (s_all[:, blk] * p).astype(BF16)
        rs.append(_dot(scores, v[:, blk]) + s_inter[:, col] * inter_all[:, blk])
    den = rs[0]
    for h in range(1, HEADS):
        den = den + rs[h]
    rden = 1.0 / jnp.maximum(jnp.abs(den), e_negm)
    out = jnp.concatenate([rs[h] * rden[:, DEN_COL + h:DEN_COL + h + 1] for h in range(HEADS)], axis=1)

    m_prev = m_ref[d, 0]
    a_end = jnp.broadcast_to(a[:, 0:1] if rev else a[:, n - 1:n], (SUBLANES, n))
    w_key = a_end + key_w
    m_chunk = jnp.broadcast_to(jnp.max(w_key, axis=1, keepdims=True), (SUBLANES, n))
    m_new = jnp.maximum(a_end + m_prev, m_chunk)
    s_prev = jnp.exp(a_end + m_prev - m_new)
    p_key = jnp.exp(w_key - m_chunk) * jnp.exp(m_chunk - m_new)
    m_ref[d, 0] = m_new
    row8 = lax.broadcasted_iota(jnp.int32, (SUBLANES, n), 0)
    lane8 = lax.broadcasted_iota(jnp.int32, (SUBLANES, n), 1)
    on_diag = (lane8 == row8 + DEN_COL) & (row8 < HEADS)
    m_ref[d, 1] = jnp.broadcast_to(jnp.sum(jnp.where(on_diag, m_new, 0.0), axis=0, keepdims=True),
                                   (SUBLANES, n))
    for h in range(HEADS):
        blk = slice(DVP * h, DVP * (h + 1))
        keys = slice(DK * h, DK * (h + 1))
        kp = (kt[keys, :].astype(F32) * p_key[h:h + 1, :]).astype(BF16)
        c_new = s_prev[h:h + 1, :] * c_ref[d, h] + _dot(kp, v[:, blk])
        c_ref[d, h] = c_new
        cb_ref[d, keys, blk] = c_new.astype(BF16)
    return out


def _mlstm_kernel(q_f, kt_f, v_f, g_f, q_b, kt_b, v_b, g_b, hf_ref, hb_ref, c_ref, cb_ref, m_ref):
    @pl.when(pl.program_id(1) == 0)
    def _():
        c_ref[...] = jnp.zeros_like(c_ref)
        cb_ref[...] = jnp.zeros_like(cb_ref)
        m_ref[...] = jnp.zeros_like(m_ref)

    for j in range(q_f.shape[0]):
        state = (c_ref.at[j], cb_ref.at[j], m_ref.at[j])
        hf_ref[j] = _mlstm_direction(False, q_f[j], kt_f[j, 0], v_f[j], g_f[j, 0], *state, 0).astype(hf_ref.dtype)
        hb_ref[j] = _mlstm_direction(True, q_b[j], kt_b[j, 0], v_b[j], g_b[j, 0], *state, 1).astype(hb_ref.dtype)


def _mixer_specs(batch, seq):
    n = seq // CHUNK
    bs = MIXER_BATCH if batch % MIXER_BATCH == 0 else 1
    fwd = lambda b, i: (b, i, 0)
    bwd = lambda b, i: (b, n - 1 - i, 0)
    blk = lambda w, im: pl.BlockSpec((bs, CHUNK, w), im)
    per_chunk = lambda rows, cols: (
        pl.BlockSpec((bs, 1, rows, cols), lambda b, i: (b, i, 0, 0)),
        pl.BlockSpec((bs, 1, rows, cols), lambda b, i: (b, n - 1 - i, 0, 0)))
    return n, bs, fwd, bwd, blk, per_chunk


def _mlstm(q, kt, v, gw, batch, seq):
    n, bs, fwd, bwd, blk, per_chunk = _mixer_specs(batch, seq)
    v3 = v.reshape(batch, seq, HW)
    gw4 = gw.reshape(batch, n, GATE_ROWS, CHUNK)
    ktf, ktb = per_chunk(QKP, CHUNK)
    gf, gb = per_chunk(GATE_ROWS, CHUNK)
    return pl.pallas_call(
        _mlstm_kernel, grid=(batch // bs, n),
        in_specs=[blk(QKP, fwd), ktf, blk(HW, fwd), gf, blk(QKP, bwd), ktb, blk(HW, bwd), gb],
        out_specs=[blk(HW, fwd), blk(HW, bwd)],
        out_shape=[jax.ShapeDtypeStruct((batch, seq, HW), BF16)] * 2,
        scratch_shapes=[pltpu.VMEM((bs, 2, HEADS, DK, DVP), F32), pltpu.VMEM((bs, 2, QKP, HW), BF16),
                        pltpu.VMEM((bs, 2, 2, SUBLANES, CHUNK), F32)],
        compiler_params=_cparams(("parallel", "arbitrary")), name="mlstm_mixer",
    )(q, kt, v3, gw4, q, kt, v3, gw4)


def _retention_direction(q, kt, v, tile_ref, row_ref, s_ref, sb_ref, d):
    s_all = _dot(q, _head_key_blocks(kt))
    inter_all = _dot(q, sb_ref[d])
    outs = []
    for h in range(HEADS):
        blk = slice(DVP * h, DVP * (h + 1))
        scores = (s_all[:, blk] * tile_ref[d, h, 0]).astype(BF16)
        outs.append(_dot(scores, v[:, blk]) + tile_ref[d, h, 1] * inter_all[:, blk])
    for h in range(HEADS):
        blk = slice(DVP * h, DVP * (h + 1))
        keys = slice(DK * h, DK * (h + 1))
        kp = (kt[keys, :].astype(F32) * row_ref[d, 0, h:h + 1, :]).astype(BF16)
        s_new = row_ref[d, 1, h:h + 1, :] * s_ref[d, h] + _dot(kp, v[:, blk])
        s_ref[d, h] = s_new
        sb_ref[d, keys, blk] = s_new.astype(BF16)
    return jnp.concatenate(outs, axis=1)


def _retention_kernel(q_f, kt_f, v_f, q_b, kt_b, v_b, tile_ref, row_ref, yf_ref, yb_ref, s_ref, sb_ref):
    @pl.when(pl.program_id(1) == 0)
    def _():
        s_ref[...] = jnp.zeros_like(s_ref)
        sb_ref[...] = jnp.zeros_like(sb_ref)

    for j in range(q_f.shape[0]):
        state = (s_ref.at[j], sb_ref.at[j])
        yf_ref[j] = _retention_direction(q_f[j], kt_f[j, 0], v_f[j], tile_ref, row_ref, *state, 0
                                         ).astype(yf_ref.dtype)
        yb_ref[j] = _retention_direction(q_b[j], kt_b[j, 0], v_b[j], tile_ref, row_ref, *state, 1
                                         ).astype(yb_ref.dtype)


def _retention(q, kt, v, tiles, rows, batch, seq):
    n, bs, fwd, bwd, blk, per_chunk = _mixer_specs(batch, seq)
    v3 = v.reshape(batch, seq, HW)
    ktf, ktb = per_chunk(QKP, CHUNK)
    return pl.pallas_call(
        _retention_kernel, grid=(batch // bs, n),
        in_specs=[blk(QKP, fwd), ktf, blk(HW, fwd), blk(QKP, bwd), ktb, blk(HW, bwd),
                  pl.BlockSpec(tiles.shape, lambda b, i: (0, 0, 0, 0, 0)),
                  pl.BlockSpec(rows.shape, lambda b, i: (0, 0, 0, 0))],
        out_specs=[blk(HW, fwd), blk(HW, bwd)],
        out_shape=[jax.ShapeDtypeStruct((batch, seq, HW), BF16)] * 2,
        scratch_shapes=[pltpu.VMEM((bs, 2, HEADS, DK, DVP), F32), pltpu.VMEM((bs, 2, QKP, HW), BF16)],
        compiler_params=_cparams(("parallel", "arbitrary")), name="retention_mixer",
    )(q, kt, v3, q, kt, v3, tiles, rows)


def _head_norm(z, w):
    valid = lax.broadcasted_iota(jnp.int32, (1, DVP), 1) < DV
    outs = []
    for h in range(HEADS):
        zh = z[:, DVP * h:DVP * (h + 1)]
        mu = jnp.sum(jnp.where(valid, zh, 0.0), axis=1, keepdims=True) * (1.0 / DV)
        d = jnp.where(valid, zh - mu, 0.0)
        var = jnp.sum(d * d, axis=1, keepdims=True) * (1.0 / DV)
        outs.append(d * lax.rsqrt(var + LN_EPS))
    return jnp.concatenate(outs, axis=1) * w


def _outproj_kernel(alpha, x_ref, yf_ref, hf_ref, hb_ref, o_ref, rf_ref, rb_ref, rg_ref, mlw_ref, rtw_ref,
                    w_ref, g_ref, b_ref, wrh_ref, wrl_ref, br_ref, x1_ref, x1b_ref, lg_ref):
    f32 = lambda ref: ref[...].astype(F32)
    y_m = _head_norm(jax.nn.sigmoid(f32(o_ref)) * (f32(hf_ref) + f32(hb_ref)), mlw_ref[...])
    rg = f32(rg_ref)
    y_r = rg * jax.nn.sigmoid(rg) * _head_norm(f32(rf_ref) + f32(rb_ref), rtw_ref[...])
    cat = jnp.concatenate([yf_ref[...], y_m, y_r], axis=1).astype(BF16)
    mix = _dot(cat, w_ref[...])
    x1 = _layernorm_rows(alpha * x_ref[...] + mix, g_ref[...], b_ref[...])
    x1_ref[...] = x1
    hi = x1.astype(BF16)
    x1b_ref[...] = hi
    lo = (x1 - hi.astype(F32)).astype(BF16)
    lg_ref[...] = _dot(hi, wrh_ref[...]) + _dot(lo, wrh_ref[...]) + _dot(hi, wrl_ref[...]) + br_ref[...]


def _outproj(x, yf, hf, hb, mo, rf, rb, rg, mlw, rtw, w, g, b, wrh, wrl, br, alpha):
    t, d = x.shape
    row = lambda n: pl.BlockSpec((ROW_TILE, n), lambda i: (i, 0))
    const = lambda a: pl.BlockSpec(a.shape, lambda i: (0, 0))
    args = (x, yf, hf, hb, mo, rf, rb, rg, mlw, rtw, w, g, b, wrh, wrl, br)
    in_specs = [row(d), row(FOURIER_W),
                row(HW), row(HW), row(HW), row(HW), row(HW), row(HW)] + [const(a) for a in args[8:]]
    return pl.pallas_call(
        functools.partial(_outproj_kernel, alpha), grid=(t // ROW_TILE,), in_specs=in_specs,
        out_specs=[row(d), row(d), row(LANES)],
        out_shape=[jax.ShapeDtypeStruct((t, d), F32), jax.ShapeDtypeStruct((t, d), BF16),
                   jax.ShapeDtypeStruct((t, LANES), F32)],
        compiler_params=_cparams(("parallel",)), name="mixer_output_projection",
    )(*args)


def _expert_kernel(be_ref, nb_ref, x_ref, w1_ref, b1_ref, w2_ref, b2_ref, o_ref, w1b_ref, w2b_ref):
    i = pl.program_id(0)

    @pl.when(jnp.logical_or(i == 0, be_ref[i] != be_ref[jnp.maximum(i - 1, 0)]))
    def _():
        w1b_ref[...] = w1_ref[0].astype(BF16)
        w2b_ref[...] = w2_ref[0].astype(BF16)

    @pl.when(i < nb_ref[0])
    def _():
        hc = _dot(x_ref[...], w1b_ref[...]) + b1_ref[0]
        gate = jnp.minimum(hc[:, :D_FF], SWIGLU_LIMIT)
        up = jnp.clip(hc[:, D_FF:], -SWIGLU_LIMIT, SWIGLU_LIMIT)
        glu = gate * jax.nn.sigmoid(SWIGLU_ALPHA * gate)
        y = _dot(((up + 1.0) * glu).astype(BF16), w2b_ref[...]) + b2_ref[0]
        o_ref[...] = y.astype(o_ref.dtype)

    @pl.when(i >= nb_ref[0])
    def _():
        o_ref[...] = jnp.zeros_like(o_ref)


def _experts(block_e, n_blocks, xs, w1, b1, w2, b2, layer):
    p, d = xs.shape
    off = layer * N_EXPERTS
    wmap = lambda i, be, nb: (off + be[i], 0, 0)
    xmap = lambda i, be, nb: (jnp.minimum(i, nb[0] - 1), 0)
    grid_spec = pltpu.PrefetchScalarGridSpec(
        num_scalar_prefetch=2, grid=(p // MOE_BLOCK,),
        in_specs=[pl.BlockSpec((MOE_BLOCK, d), xmap),
                  pl.BlockSpec((1, d, 2 * D_FF), wmap), pl.BlockSpec((1, 1, 2 * D_FF), wmap),
                  pl.BlockSpec((1, D_FF, d), wmap), pl.BlockSpec((1, 1, d), wmap)],
        out_specs=pl.BlockSpec((MOE_BLOCK, d), lambda i, be, nb: (i, 0)),
        scratch_shapes=[pltpu.VMEM((d, 2 * D_FF), BF16), pltpu.VMEM((D_FF, d), BF16)])
    return pl.pallas_call(
        _expert_kernel, grid_spec=grid_spec, out_shape=jax.ShapeDtypeStruct((p, d), BF16),
        compiler_params=_cparams(("arbitrary",)), name="routed_experts",
    )(block_e, n_blocks, xs, w1, b1, w2, b2)


def _route_kernel(lg_ref, gate_ref, eid_ref, pos_ref, cnt_ref, run_ref, pst_ref):
    ph = pl.program_id(0)
    i = pl.program_id(1)
    tm = lg_ref.shape[0]
    lane = lax.broadcasted_iota(jnp.int32, (tm, LANES), 1)
    x = jnp.where(lane < N_EXPERTS, lg_ref[...], -jnp.inf)
    vals, ids = [], []
    onehot = jnp.zeros((tm, LANES), F32)
    for _ in range(TOP_K):
        m = jnp.max(x, axis=1, keepdims=True)
        idx = jnp.min(jnp.where(x == m, lane, LANES), axis=1, keepdims=True)
        sel = lane == idx
        onehot = jnp.where(sel, 1.0, onehot)
        x = jnp.where(sel, -jnp.inf, x)
        vals.append(m)
        ids.append(idx)
    tile_counts = jnp.sum(onehot, axis=0, keepdims=True)

    @pl.when(jnp.logical_and(ph == 0, i == 0))
    def _():
        run_ref[...] = jnp.zeros_like(run_ref)

    @pl.when(ph == 0)
    def _():
        run_ref[...] += tile_counts

    @pl.when(jnp.logical_and(ph == 1, i == 0))
    def _():
        sizes = run_ref[...]
        cnt_ref[...] = sizes
        psizes = jnp.ceil(sizes * (1.0 / MOE_BLOCK)) * MOE_BLOCK
        r2 = lax.broadcasted_iota(jnp.int32, (LANES, LANES), 0)
        c2 = lax.broadcasted_iota(jnp.int32, (LANES, LANES), 1)
        incl = jnp.where(r2 <= c2, 1.0, 0.0).astype(BF16)
        pst_ref[...] = _cumsum_dot(incl, psizes, False) - psizes
        run_ref[...] = jnp.zeros_like(run_ref)

    @pl.when(ph == 1)
    def _():
        rr = lax.broadcasted_iota(jnp.int32, (tm, tm), 0)
        cc = lax.broadcasted_iota(jnp.int32, (tm, tm), 1)
        strict = jnp.where(cc < rr, 1.0, 0.0).astype(BF16)
        base = _dot(strict, onehot.astype(BF16)) + run_ref[0:1, :] + pst_ref[0:1, :]
        denom = jnp.ones_like(vals[0])
        for k in range(1, TOP_K):
            denom = denom + jnp.exp(vals[k] - vals[0])
        gates = jnp.zeros((tm, LANES), F32)
        eids = jnp.zeros((tm, LANES), jnp.int32)
        poss = jnp.zeros((tm, LANES), jnp.int32)
        for k in range(TOP_K):
            pk = jnp.sum(jnp.where(lane == ids[k], base, 0.0), axis=1, keepdims=True)
            gates = jnp.where(lane == k, jnp.exp(vals[k] - vals[0]) / denom, gates)
            eids = jnp.where(lane == k, ids[k], eids)
            poss = jnp.where(lane == k, pk.astype(jnp.int32), poss)
        gate_ref[...] = gates
        eid_ref[...] = eids
        pos_ref[...] = poss
        run_ref[...] += tile_counts


def _route(logits):
    t = logits.shape[0]
    tile = pl.BlockSpec((ROW_TILE, LANES), lambda ph, i: (i, 0))
    out_tile = pl.BlockSpec((ROW_TILE, LANES), lambda ph, i: (i * ph, 0))
    gates, eids, poss, cnt = pl.pallas_call(
        _route_kernel, grid=(2, t // ROW_TILE), in_specs=[tile],
        out_specs=[out_tile, out_tile, out_tile, pl.BlockSpec((SUBLANES, LANES), lambda ph, i: (0, 0))],
        out_shape=[jax.ShapeDtypeStruct((t, LANES), F32), jax.ShapeDtypeStruct((t, LANES), jnp.int32),
                   jax.ShapeDtypeStruct((t, LANES), jnp.int32), jax.ShapeDtypeStruct((SUBLANES, LANES), F32)],
        scratch_shapes=[pltpu.VMEM((SUBLANES, LANES), F32), pltpu.VMEM((SUBLANES, LANES), F32)],
        compiler_params=_cparams(("arbitrary", "arbitrary")), name="router_topk",
    )(logits)
    n = t * TOP_K
    p = n + N_EXPERTS * MOE_BLOCK
    nb = p // MOE_BLOCK
    sizes = cnt[0, :N_EXPERTS].astype(jnp.int32)
    psizes = (sizes + MOE_BLOCK - 1) // MOE_BLOCK * MOE_BLOCK
    pends = jnp.cumsum(psizes)
    pstarts = pends - psizes
    starts = jnp.cumsum(sizes) - sizes
    block_e = jnp.minimum(jnp.searchsorted(pends, jnp.arange(nb, dtype=jnp.int32) * MOE_BLOCK, side='right'),
                          N_EXPERTS - 1).astype(jnp.int32)
    n_blocks = (pends[-1] // MOE_BLOCK).astype(jnp.int32).reshape(1)
    keys = eids[:, :TOP_K].reshape(n) * n + jnp.arange(n, dtype=jnp.int32)
    order = jnp.sort(keys) % n
    per_row = lambda per_block: jnp.repeat(per_block, MOE_BLOCK)
    local = jnp.arange(p, dtype=jnp.int32) - per_row(pstarts[block_e])
    size_r = per_row(sizes[block_e])
    pair = jnp.take(order, per_row(starts[block_e]) + local, mode="clip")
    src_tok = jnp.where(local < size_r, pair // TOP_K, 0)
    return gates, poss[:, :TOP_K], src_tok, block_e, n_blocks


def _combine_kernel(alpha, x_ref, y_ref, gate_ref, g_ref, b_ref, o_ref, ob_ref):
    gates = gate_ref[...]
    moe = gates[:, 0:1] * y_ref[0].astype(F32)
    for k in range(1, TOP_K):
        moe = moe + gates[:, k:k + 1] * y_ref[k].astype(F32)
    y = _layernorm_rows(alpha * x_ref[...] + moe, g_ref[...], b_ref[...])
    o_ref[...] = y
    ob_ref[...] = y.astype(BF16)


def _combine(x, yk, gates, g, b, alpha):
    t, d = x.shape
    row = pl.BlockSpec((ROW_TILE, d), lambda i: (i, 0))
    vec = pl.BlockSpec((1, d), lambda i: (0, 0))
    return pl.pallas_call(
        functools.partial(_combine_kernel, alpha), grid=(t // ROW_TILE,),
        in_specs=[row, pl.BlockSpec((TOP_K, ROW_TILE, d), lambda i: (0, i, 0)),
                  pl.BlockSpec((ROW_TILE, LANES), lambda i: (i, 0)), vec, vec],
        out_specs=[row, row],
        out_shape=[jax.ShapeDtypeStruct((t, d), F32), jax.ShapeDtypeStruct((t, d), BF16)],
        compiler_params=_cparams(("parallel",)), name="expert_combine_layernorm",
    )(x, yk, gates, g.reshape(1, d), b.reshape(1, d))


def kernel(x, emb_ln_g, emb_ln_b, w_in, b_in, conv_w, ml_norm_w, ret_norm_w, w_out, ln1_g, ln1_b,
           w_router, b_router, w1, b1, w2, b2, ln2_g, ln2_b):
    batch, seq, d = x.shape
    depth = w_in.shape[0]
    assert d == D_MODEL and seq % ROW_TILE == 0
    t = batch * seq
    alpha = (2.0 * depth) ** 0.25

    w_in_p = _layout_proj(w_in).astype(BF16)
    den_cols = jnp.array([MV_OFFSET + DVP * h + DEN_COL + h for h in range(HEADS)])
    b_in_p = _layout_proj(b_in).at[:, den_cols].set(1.0)[:, None, :]
    wg_p = jnp.swapaxes(_layout_gates(w_in), -1, -2).astype(BF16)
    bg_p = _layout_gates(b_in)[:, :, None]
    hk = HEADS * DK
    conv_p = jnp.concatenate([_pad_last(conv_w[..., :hk], QKP), _pad_last(conv_w[..., hk:], QKP)], axis=-1)
    mlw_p = _pad_heads(ml_norm_w)[:, None, :]
    rtw_p = _pad_heads(ret_norm_w)[:, None, :]
    w_out_p = _layout_wout(w_out).astype(BF16)
    wr_hi, wr_lo = _split_bf16(_pad_last(w_router, LANES))
    br_p = _pad_last(b_router, LANES)[:, None, :]
    w1_r = w1.reshape(depth * N_EXPERTS, d, 2 * D_FF)
    w2_r = w2.reshape(depth * N_EXPERTS, D_FF, d)
    b1_r = b1.reshape(depth * N_EXPERTS, 1, 2 * D_FF)
    b2_r = b2.reshape(depth * N_EXPERTS, 1, d)

    stage1, stage2, chan = _dft_tables(seq)
    rc, rs1, rs2 = _rotary_tables(seq)
    ret_tiles, ret_rows = _retention_tables()

    xf, xb = _ln(x.reshape(t, d), emb_ln_g, emb_ln_b)
    for l in range(depth):
        pf, mqk, mv, mo, rqk, rv, rg, gates_t = _inproj(xb, w_in_p[l], b_in_p[l], wg_p[l], bg_p[l])
        zv, zw = _fourier_channels(pf, chan)
        yf = _fourier_sequence(stage1, stage2, zv, zw, batch, seq)
        mq, mkt, rq, rkt, gw = _mixer_prep(mqk, rqk, rc, rs1, rs2, conv_p[l], gates_t, batch, seq)
        hf, hb = _mlstm(mq, mkt, mv, gw, batch, seq)
        rf, rb = _retention(rq, rkt, rv, ret_tiles, ret_rows, batch, seq)
        x1, x1b, logits = _outproj(
            xf, yf, hf.reshape(t, HW), hb.reshape(t, HW), mo, rf.reshape(t, HW), rb.reshape(t, HW), rg,
            mlw_p[l], rtw_p[l], w_out_p[l], ln1_g[l][None, :], ln1_b[l][None, :],
            wr_hi[l], wr_lo[l], br_p[l], alpha)
        gates, pos, src_tok, block_e, n_blocks = _route(logits)
        xs = jnp.take(x1b, src_tok, axis=0, mode="clip")
        ys = _experts(block_e, n_blocks, xs, w1_r, b1_r, w2_r, b2_r, l)
        yk = jnp.take(ys, pos.T.reshape(TOP_K * t), axis=0, mode="clip").reshape(TOP_K, t, d)
        xf, xb = _combine(x1, yk, gates, ln2_g[l], ln2_b[l], alpha)
    return xf.reshape(batch, seq, d)
```

```python
import functools

import jax
import jax.numpy as jnp
from jax import lax
from jax.experimental import pallas as pl
from jax.experimental.pallas import tpu as pltpu

F32 = jnp.float32
BF16 = jnp.bfloat16

D_MODEL = 1024
CHUNK = 128
FOURIER_W = D_MODEL // 4
N_FGROUPS = 4
FG_W = FOURIER_W // N_FGROUPS
ML_W = 3 * D_MODEL // 8
RET_W = D_MODEL - FOURIER_W - ML_W
HEADS = 4
DV = ML_W // HEADS
DK = DV // 2
CONV_W = 3
ROPE_BASE = 10000.0
RET_GAMMA_EXP0 = 5.0
RET_BWD_EXP_OFFSET = 0.5
N_EXPERTS = 32
TOP_K = 4
D_FF = D_MODEL
SWIGLU_ALPHA = 1.702
SWIGLU_LIMIT = 7.0
LN_EPS = 1e-5

COL_F = 0
COL_MQK = COL_F + FOURIER_W
COL_MV = COL_MQK + 2 * HEADS * DK
COL_MO = COL_MV + ML_W
COL_MG = COL_MO + ML_W
COL_RQ = COL_MG + 4 * HEADS
COL_RK = COL_RQ + HEADS * DK
COL_RV = COL_RK + HEADS * DK
COL_RG = COL_RV + RET_W
PROJ_W = COL_RG + RET_W

LANES = 128
SUBLANES = 8
DVP = LANES
QKP = 2 * LANES
HW = HEADS * DVP
DEN_COL = DV
GATE_ROWS = 4 * SUBLANES
VMEM_LIMIT = 52 * 1024 * 1024

PROJ_GROUPS = (FOURIER_W, 2 * QKP, HW, HW, 2 * QKP, HW, HW)
PROJ_DTYPES = (BF16, F32, BF16, BF16, F32, BF16, BF16)
MV_OFFSET = FOURIER_W + 2 * QKP
MIX_P = FOURIER_W + 2 * HW

FFT_N1 = 128
ROW_TILE = 512
MOE_BLOCK = 256
MIXER_BATCH = 2

def _cparams(sem):
    return pltpu.CompilerParams(dimension_semantics=sem, vmem_limit_bytes=VMEM_LIMIT)


def _pad_last(w, n):
    return jnp.pad(w, [(0, 0)] * (w.ndim - 1) + [(0, n - w.shape[-1])])


def _pad_heads(w):
    lead = w.shape[:-1]
    w = w.reshape(*lead, HEADS, DV)
    w = jnp.pad(w, [(0, 0)] * (len(lead) + 1) + [(0, DVP - DV)])
    return w.reshape(*lead, HW)


def _pair_split(w):
    lead = w.shape[:-1]
    w = w.reshape(*lead, HEADS, DK // 2, 2)
    w = jnp.swapaxes(w, -1, -2).reshape(*lead, HEADS * DK)
    return _pad_last(w, QKP)


def _layout_proj(w):
    hk = HEADS * DK
    return jnp.concatenate([
        w[..., COL_F:COL_MQK],
        _pad_last(w[..., COL_MQK:COL_MQK + hk], QKP), _pad_last(w[..., COL_MQK + hk:COL_MV], QKP),
        _pad_heads(w[..., COL_MV:COL_MO]), _pad_heads(w[..., COL_MO:COL_MG]),
        _pair_split(w[..., COL_RQ:COL_RK]), _pair_split(w[..., COL_RK:COL_RV]),
        _pad_heads(w[..., COL_RV:COL_RG]), _pad_heads(w[..., COL_RG:PROJ_W]),
    ], axis=-1)


def _layout_gates(w):
    g = w[..., COL_MG:COL_RQ]
    pick = lambda j: _pad_last(g[..., HEADS * j:HEADS * (j + 1)], SUBLANES)
    return jnp.concatenate([pick(1), pick(0), pick(3), pick(2)], axis=-1)


def _layout_wout(w):
    wt = jnp.swapaxes(w, -1, -2)
    wt = jnp.concatenate([wt[..., :FOURIER_W], _pad_heads(wt[..., FOURIER_W:FOURIER_W + ML_W]),
                          _pad_heads(wt[..., FOURIER_W + ML_W:])], axis=-1)
    return jnp.swapaxes(wt, -1, -2)


def _split_bf16(w):
    hi = w.astype(BF16)
    return hi, (w - hi.astype(F32)).astype(BF16)


def _dft_tables(seq):
    n1c, n2c, g = FFT_N1, seq // FFT_N1, SUBLANES
    eye = jnp.eye(g, dtype=F32)
    ar = lambda n: jnp.arange(n, dtype=jnp.int32)
    n1 = (ar(n1c // g)[:, None] * g + ar(g)[None, :])[:, None, None, :]
    kn = (ar(n2c)[None, :, None, None] * (n1 + n1c * ar(n2c)[None, None, :, None])) % seq
    ang = kn.astype(F32) * (2.0 * jnp.pi / seq)
    spread = lambda t: jnp.einsum('aknj,jl->akjnl', t, eye).reshape(n1c // g, n2c * g, n2c * g)
    gr, gi = spread(jnp.cos(ang) * n2c ** -0.5), spread(-jnp.sin(ang) * n2c ** -0.5)
    stage1 = jnp.concatenate([jnp.concatenate([gr, gi], axis=2),
                              jnp.concatenate([gi, -gr], axis=2)], axis=1).astype(BF16)
    phi = ((ar(n1c)[:, None] * ar(n1c)[None, :]) % n1c).astype(F32) * (2.0 * jnp.pi / n1c)
    spread2 = lambda t: jnp.einsum('kn,jl->kjln', t, eye).reshape(n1c * g, g * n1c)
    stage2 = jnp.concatenate([spread2(jnp.cos(phi) * n1c ** -0.5),
                              spread2(jnp.sin(phi) * n1c ** -0.5)], axis=1).astype(BF16)
    c = jnp.arange(FG_W, dtype=jnp.int32)
    cc = ((c[:, None] * c[None, :]) % FG_W).astype(F32) * (2.0 * jnp.pi / FG_W)
    eye = jnp.eye(N_FGROUPS, dtype=F32)
    bd_c = jnp.kron(eye, jnp.cos(cc) * FG_W ** -0.5)
    bd_s = jnp.kron(eye, jnp.sin(cc) * FG_W ** -0.5)
    chan = jnp.concatenate([bd_c, bd_s], axis=1).astype(BF16)
    return stage1, stage2, chan


def _rotary_tables(seq):
    inv = 1.0 / (ROPE_BASE ** (jnp.arange(0, DK, 2, dtype=F32) / DK))
    ang = jnp.arange(seq, dtype=F32)[:, None] * inv[None, :]
    cos, sin = jnp.cos(ang), jnp.sin(ang)
    zero = jnp.zeros_like(sin)
    heads = lambda a, b: _pad_last(jnp.tile(jnp.concatenate([a, b], axis=1), (1, HEADS)), QKP)
    return heads(cos, cos), heads(-sin, zero), heads(zero, sin)


def _retention_tables():
    idx = jnp.arange(CHUNK, dtype=F32)
    diff = idx[:, None] - idx[None, :]
    tiles, rows = [], []
    for rev in (False, True):
        offset = RET_BWD_EXP_OFFSET if rev else 0.0
        lg = jnp.log1p(-jnp.exp2(-(RET_GAMMA_EXP0 + offset) - jnp.arange(HEADS, dtype=F32)))
        lg3 = lg[:, None, None]
        if rev:
            decay = jnp.where((diff < 0)[None], jnp.exp(lg3 * jnp.maximum(-diff, 0.0)[None]), 0.0)
            w_inter = jnp.exp(lg[:, None] * (CHUNK - idx)[None, :])
            w_key = jnp.exp(lg[:, None] * idx[None, :])
        else:
            decay = jnp.where((diff >= 0)[None], jnp.exp(lg3 * jnp.maximum(diff, 0.0)[None]), 0.0)
            w_inter = jnp.exp(lg[:, None] * (idx + 1.0)[None, :])
            w_key = jnp.exp(lg[:, None] * (CHUNK - 1 - idx)[None, :])
        g_chunk = jnp.broadcast_to(jnp.exp(lg * CHUNK)[:, None], (HEADS, CHUNK))
        tiles.append(jnp.stack([decay, jnp.broadcast_to(w_inter[:, :, None], (HEADS, CHUNK, LANES))], axis=1))
        pad = lambda t: jnp.pad(t, ((0, SUBLANES - HEADS), (0, 0)))
        rows.append(jnp.stack([pad(w_key), pad(g_chunk)], axis=0))
    return jnp.stack(tiles, axis=0), jnp.stack(rows, axis=0)


def _layernorm_rows(z, g, b):
    mu = jnp.mean(z, axis=-1, keepdims=True)
    d = z - mu
    var = jnp.mean(d * d, axis=-1, keepdims=True)
    return d * lax.rsqrt(var + LN_EPS) * g + b


def _ln_kernel(x_ref, g_ref, b_ref, o_ref, ob_ref):
    y = _layernorm_rows(x_ref[...], g_ref[...], b_ref[...])
    o_ref[...] = y
    ob_ref[...] = y.astype(BF16)


def _ln(x, g, b):
    t, d = x.shape
    row = pl.BlockSpec((ROW_TILE, d), lambda i: (i, 0))
    vec = pl.BlockSpec((1, d), lambda i: (0, 0))
    return pl.pallas_call(
        _ln_kernel, grid=(t // ROW_TILE,), in_specs=[row, vec, vec], out_specs=[row, row],
        out_shape=[jax.ShapeDtypeStruct((t, d), F32), jax.ShapeDtypeStruct((t, d), BF16)],
        compiler_params=_cparams(("parallel",)), name="input_layernorm",
    )(x, g.reshape(1, d), b.reshape(1, d))


def _inproj_kernel(x_ref, w_ref, b_ref, wg_ref, bg_ref, *o_refs):
    x = x_ref[...]
    c0 = 0
    for o_ref in o_refs[:-1]:
        n = o_ref.shape[-1]
        y = jnp.dot(x, w_ref[:, c0:c0 + n], preferred_element_type=F32) + b_ref[:, c0:c0 + n]
        o_ref[...] = y.astype(o_ref.dtype)
        c0 += n
    gt = lax.dot_general(wg_ref[...], x, (((1,), (1,)), ((), ())), preferred_element_type=F32)
    o_refs[-1][...] = gt + bg_ref[...]


def _inproj(xb, w, b, wg, bg):
    t, d = xb.shape
    n_p = w.shape[1]
    const = lambda a: pl.BlockSpec(a.shape, lambda i: (0, 0))
    return pl.pallas_call(
        _inproj_kernel, grid=(t // ROW_TILE,),
        in_specs=[pl.BlockSpec((ROW_TILE, d), lambda i: (i, 0)), const(w), const(b), const(wg), const(bg)],
        out_specs=[pl.BlockSpec((ROW_TILE, n), lambda i: (i, 0)) for n in PROJ_GROUPS]
        + [pl.BlockSpec((GATE_ROWS, ROW_TILE), lambda i: (0, i))],
        out_shape=[jax.ShapeDtypeStruct((t, n), dt) for n, dt in zip(PROJ_GROUPS, PROJ_DTYPES)]
        + [jax.ShapeDtypeStruct((GATE_ROWS, t), F32)],
        compiler_params=_cparams(("parallel",)), name="input_projection",
    )(xb, w, b, wg, bg)


def _fchan_kernel(u_ref, m_ref, v_ref, w_ref):
    vw = jnp.dot(u_ref[...], m_ref[...], preferred_element_type=F32)
    v_ref[...] = vw[:, :FOURIER_W]
    w_ref[...] = vw[:, FOURIER_W:]


def _fourier_channels(u, chan):
    t = u.shape[0]
    row = pl.BlockSpec((ROW_TILE, FOURIER_W), lambda i: (i, 0))
    return pl.pallas_call(
        _fchan_kernel, grid=(t // ROW_TILE,),
        in_specs=[row, pl.BlockSpec((FOURIER_W, 2 * FOURIER_W), lambda i: (0, 0))],
        out_specs=[row, row], out_shape=[jax.ShapeDtypeStruct((t, FOURIER_W), F32)] * 2,
        compiler_params=_cparams(("parallel",)), name="fourier_channel_dft",
    )(u, chan)


def _fstage1_kernel(m_ref, v_ref, w_ref, br_ref, bi_ref):
    rows = v_ref.shape[1] * SUBLANES
    x = jnp.concatenate([v_ref[0].reshape(rows, FOURIER_W), w_ref[0].reshape(rows, FOURIER_W)], axis=0)
    y = jnp.dot(m_ref[0], x.astype(BF16), preferred_element_type=F32)
    br_ref[0] = y[:rows].reshape(br_ref.shape[1:])
    bi_ref[0] = y[rows:].reshape(bi_ref.shape[1:])


def _fstage2_kernel(m_ref, br_ref, bi_ref, o_ref):
    rows = br_ref.shape[2] * SUBLANES
    x = jnp.concatenate([br_ref[0].reshape(rows, FOURIER_W), bi_ref[0].reshape(rows, FOURIER_W)], axis=0)
    y = jnp.dot(m_ref[...], x.astype(BF16), preferred_element_type=F32)
    o_ref[0] = y.reshape(o_ref.shape[1:])


def _fourier_sequence(stage1, stage2, v, w, batch, seq):
    n1c, n2c, g = FFT_N1, seq // FFT_N1, SUBLANES
    view = lambda a: a.reshape(batch, n2c, n1c, FOURIER_W)
    blk1 = pl.BlockSpec((1, n2c, g, FOURIER_W), lambda a, b: (b, 0, a, 0))
    shape1 = jax.ShapeDtypeStruct((batch, n2c, n1c, FOURIER_W), F32)
    br, bi = pl.pallas_call(
        _fstage1_kernel, grid=(n1c // g, batch),
        in_specs=[pl.BlockSpec((1,) + stage1.shape[1:], lambda a, b: (a, 0, 0)), blk1, blk1],
        out_specs=[blk1, blk1], out_shape=[shape1, shape1],
        compiler_params=_cparams(("parallel", "parallel")), name="fourier_sequence_stage1",
    )(stage1, view(v), view(w))
    blk2 = pl.BlockSpec((1, g, n1c, FOURIER_W), lambda kb, b: (b, kb, 0, 0))
    y = pl.pallas_call(
        _fstage2_kernel, grid=(n2c // g, batch),
        in_specs=[pl.BlockSpec(stage2.shape, lambda kb, b: (0, 0)), blk2, blk2],
        out_specs=pl.BlockSpec((1, n1c, g, FOURIER_W), lambda kb, b: (b, 0, kb, 0)),
        out_shape=jax.ShapeDtypeStruct((batch, n1c, n2c, FOURIER_W), F32),
        compiler_params=_cparams(("parallel", "parallel")), name="fourier_sequence_stage2",
    )(stage2, br, bi)
    return y.reshape(batch * seq, FOURIER_W)


def _dot(a, b):
    return jnp.dot(a, b, preferred_element_type=F32)


def _cumsum_dot(tri, x, tri_left):
    hi = x.astype(BF16)
    r1 = x - hi.astype(F32)
    mid = r1.astype(BF16)
    lo = (r1 - mid.astype(F32)).astype(BF16)
    if tri_left:
        return _dot(tri, hi) + _dot(tri, mid) + _dot(tri, lo)
    return _dot(hi, tri) + _dot(mid, tri) + _dot(lo, tri)


def _log_sigmoid(x):
    return jnp.minimum(x, 0.0) - jnp.log1p(jnp.exp(-jnp.abs(x)))


def _mixer_prep_kernel(mqk_ref, prev_ref, next_ref, rqk_ref, rc_ref, rs1_ref, rs2_ref, convw_ref, g_ref,
                       mq_ref, mkt_ref, rq_ref, rkt_ref, gw_ref):
    i = pl.program_id(1)
    rows_n = mqk_ref.shape[1]
    chunks = rows_n // CHUNK
    n = CHUNK
    r2 = lax.broadcasted_iota(jnp.int32, (n, n), 0)
    c2 = lax.broadcasted_iota(jnp.int32, (n, n), 1)
    for c in range(chunks):
        lanes = slice(c * n, (c + 1) * n)
        groups = []
        for rev in (False, True):
            r0 = 2 * SUBLANES if rev else 0
            tri = jnp.where((c2 <= r2) if rev else (c2 >= r2), 1.0, 0.0).astype(BF16)
            a = _cumsum_dot(tri, _log_sigmoid(g_ref[r0:r0 + SUBLANES, lanes]), False)
            groups += [g_ref[r0 + SUBLANES:r0 + 2 * SUBLANES, lanes] - a, a]
        gw_ref[c] = jnp.concatenate(groups, axis=0)

    qk = mqk_ref[0]
    rows = lax.broadcasted_iota(jnp.int32, (rows_n, 2 * QKP), 0)
    prev_row = prev_ref[0, 0, SUBLANES - 1:SUBLANES, :] * jnp.where(i == 0, 0.0, 1.0)
    next_row = next_ref[0, 0, 0:1, :] * jnp.where(i == pl.num_programs(1) - 1, 0.0, 1.0)
    xm1 = jnp.where(rows == 0, prev_row, pltpu.roll(qk, 1, 0))
    xp1 = jnp.where(rows == rows_n - 1, next_row, pltpu.roll(qk, rows_n - 1, 0))
    conv = xm1 * convw_ref[0:1, :] + qk * convw_ref[1:2, :] + xp1 * convw_ref[2:3, :]
    act = conv * jax.nn.sigmoid(conv)
    mq_ref[0] = act[:, :QKP].astype(BF16)

    rc, rs1, rs2 = rc_ref[...], rs1_ref[...], rs2_ref[...]
    half = DK // 2

    def rotate(t):
        return t * rc + pltpu.roll(t, QKP - half, 1) * rs1 + pltpu.roll(t, half, 1) * rs2

    rqk = rqk_ref[0]
    rq_ref[0] = rotate(rqk[:, :QKP]).astype(BF16)
    mk = act[:, QKP:] * DK ** -0.5
    rk = rotate(rqk[:, QKP:]) * DK ** -0.5
    for c in range(chunks):
        mkt_ref[0, c] = mk[c * n:(c + 1) * n, :].T.astype(BF16)
        rkt_ref[0, c] = rk[c * n:(c + 1) * n, :].T.astype(BF16)


def _mixer_prep(mqk, rqk, rc, rs1, rs2, convw, gates, batch, seq):
    n = seq // CHUNK
    tiles = seq // ROW_TILE
    chunks = ROW_TILE // CHUNK
    g8 = ROW_TILE // SUBLANES
    mqk3 = mqk.reshape(batch, seq, 2 * QKP)
    mqk8 = mqk.reshape(batch, seq // SUBLANES, SUBLANES, 2 * QKP)
    rqk3 = rqk.reshape(batch, seq, 2 * QKP)
    tile = lambda w: pl.BlockSpec((1, ROW_TILE, w), lambda b, i: (b, i, 0))
    halo = lambda index: pl.BlockSpec((1, 1, SUBLANES, 2 * QKP), index)
    pos = pl.BlockSpec((ROW_TILE, QKP), lambda b, i: (i, 0))
    kt = pl.BlockSpec((1, chunks, QKP, CHUNK), lambda b, i: (b, i, 0, 0))
    q_shape = jax.ShapeDtypeStruct((batch, seq, QKP), BF16)
    kt_shape = jax.ShapeDtypeStruct((batch, n, QKP, CHUNK), BF16)
    return pl.pallas_call(
        _mixer_prep_kernel, grid=(batch, tiles),
        in_specs=[tile(2 * QKP),
                  halo(lambda b, i: (b, jnp.maximum(i * g8 - 1, 0), 0, 0)),
                  halo(lambda b, i: (b, jnp.minimum((i + 1) * g8, seq // SUBLANES - 1), 0, 0)),
                  tile(2 * QKP), pos, pos, pos,
                  pl.BlockSpec((CONV_W, 2 * QKP), lambda b, i: (0, 0)),
                  pl.BlockSpec((GATE_ROWS, ROW_TILE), lambda b, i: (0, b * tiles + i))],
        out_specs=[tile(QKP), kt, tile(QKP), kt,
                   pl.BlockSpec((chunks, GATE_ROWS, CHUNK), lambda b, i: (b * tiles + i, 0, 0))],
        out_shape=[q_shape, kt_shape, q_shape, kt_shape,
                   jax.ShapeDtypeStruct((batch * n, GATE_ROWS, CHUNK), F32)],
        compiler_params=_cparams(("parallel", "parallel")), name="mixer_qk_prep",
    )(mqk3, mqk8, mqk8, rqk3, rc, rs1, rs2, convw, gates)


def _head_key_blocks(kt):
    rows = lax.broadcasted_iota(jnp.int32, kt.shape, 0)
    zero = jnp.zeros_like(kt)
    return jnp.concatenate([jnp.where((rows >= DK * h) & (rows < DK * (h + 1)), kt, zero)
                            for h in range(HEADS)], axis=1)


def _mlstm_direction(rev, q, kt, v, gw, c_ref, cb_ref, m_ref, d):
    n = CHUNK
    r0 = 2 * SUBLANES if rev else 0
    key_w = gw[r0:r0 + SUBLANES, :]
    a = gw[r0 + SUBLANES:r0 + 2 * SUBLANES, :]
    r2 = lax.broadcasted_iota(jnp.int32, (n, n), 0)
    c2 = lax.broadcasted_iota(jnp.int32, (n, n), 1)
    causal = (c2 >= r2) if rev else (c2 <= r2)
    lane = lax.broadcasted_iota(jnp.int32, (n, LANES), 1)

    cmax = jnp.zeros((n, LANES), F32)
    for h in range(HEADS):
        cm_h = jnp.max(jnp.where(causal, key_w[h:h + 1, :], -jnp.inf), axis=1, keepdims=True)
        cmax = jnp.where(lane == DEN_COL + h, cm_h, cmax)
    zero = jnp.zeros((SUBLANES, n), F32)
    den_group = DEN_COL // SUBLANES
    a_col = jnp.concatenate([zero] * den_group + [a] + [zero] * (n // SUBLANES - den_group - 1), axis=0).T
    m_lane = m_ref[d, 1, 0:1, :]
    mm = jnp.maximum(m_lane, cmax)
    s_inter = jnp.exp(m_lane - mm)
    e_negm = jnp.exp(-a_col - mm)

    s_all = _dot(q, _head_key_blocks(kt))
    inter_all = _dot(q, cb_ref[d])
    rs = []
    for h in range(HEADS):
        blk = slice(DVP * h, DVP * (h + 1))
        col = slice(DEN_COL + h, DEN_COL + h + 1)
        p = jnp.exp(jnp.where(causal, key_w[h:h + 1, :] - mm[:, col], -jnp.inf))
        scores = (s_all[:, blk] * p).astype(BF16)
        rs.append(_dot(scores, v[:, blk]) + s_inter[:, col] * inter_all[:, blk])
    den = rs[0]
    for h in range(1, HEADS):
        den = den + rs[h]
    rden = 1.0 / jnp.maximum(jnp.abs(den), e_negm)
    out = jnp.concatenate([rs[h] * rden[:, DEN_COL + h:DEN_COL + h + 1] for h in range(HEADS)], axis=1)

    m_prev = m_ref[d, 0]
    a_end = jnp.broadcast_to(a[:, 0:1] if rev else a[:, n - 1:n], (SUBLANES, n))
    w_key = a_end + key_w
    m_chunk = jnp.broadcast_to(jnp.max(w_key, axis=1, keepdims=True), (SUBLANES, n))
    m_new = jnp.maximum(a_end + m_prev, m_chunk)
    s_prev = jnp.exp(a_end + m_prev - m_new)
    p_key = jnp.exp(w_key - m_chunk) * jnp.exp(m_chunk - m_new)
    m_ref[d, 0] = m_new
    row8 = lax.broadcasted_iota(jnp.int32, (SUBLANES, n), 0)
    lane8 = lax.broadcasted_iota(jnp.int32, (SUBLANES, n), 1)
    on_diag = (lane8 == row8 + DEN_COL) & (row8 < HEADS)
    m_ref[d, 1] = jnp.broadcast_to(jnp.sum(jnp.where(on_diag, m_new, 0.0), axis=0, keepdims=True),
                                   (SUBLANES, n))
    for h in range(HEADS):
        blk = slice(DVP * h, DVP * (h + 1))
        keys = slice(DK * h, DK * (h + 1))
        kp = (kt[keys, :].astype(F32) * p_key[h:h + 1, :]).astype(BF16)
        c_new = s_prev[h:h + 1, :] * c_ref[d, h] + _dot(kp, v[:, blk])
        c_ref[d, h] = c_new
        cb_ref[d, keys, blk] = c_new.astype(BF16)
    return out


def _mlstm_kernel(q_f, kt_f, v_f, g_f, q_b, kt_b, v_b, g_b, hf_ref, hb_ref, c_ref, cb_ref, m_ref):
    @pl.when(pl.program_id(1) == 0)
    def _():
        c_ref[...] = jnp.zeros_like(c_ref)
        cb_ref[...] = jnp.zeros_like(cb_ref)
        m_ref[...] = jnp.zeros_like(m_ref)

    for j in range(q_f.shape[0]):
        state = (c_ref.at[j], cb_ref.at[j], m_ref.at[j])
        hf_ref[j] = _mlstm_direction(False, q_f[j], kt_f[j, 0], v_f[j], g_f[j, 0], *state, 0).astype(hf_ref.dtype)
        hb_ref[j] = _mlstm_direction(True, q_b[j], kt_b[j, 0], v_b[j], g_b[j, 0], *state, 1).astype(hb_ref.dtype)


def _mixer_specs(batch, seq):
    n = seq // CHUNK
    bs = MIXER_BATCH if batch % MIXER_BATCH == 0 else 1
    fwd = lambda b, i: (b, i, 0)
    bwd = lambda b, i: (b, n - 1 - i, 0)
    blk = lambda w, im: pl.BlockSpec((bs, CHUNK, w), im)
    per_chunk = lambda rows, cols: (
        pl.BlockSpec((bs, 1, rows, cols), lambda b, i: (b, i, 0, 0)),
        pl.BlockSpec((bs, 1, rows, cols), lambda b, i: (b, n - 1 - i, 0, 0)))
    return n, bs, fwd, bwd, blk, per_chunk


def _mlstm(q, kt, v, gw, batch, seq):
    n, bs, fwd, bwd, blk, per_chunk = _mixer_specs(batch, seq)
    v3 = v.reshape(batch, seq, HW)
    gw4 = gw.reshape(batch, n, GATE_ROWS, CHUNK)
    ktf, ktb = per_chunk(QKP, CHUNK)
    gf, gb = per_chunk(GATE_ROWS, CHUNK)
    return pl.pallas_call(
        _mlstm_kernel, grid=(batch // bs, n),
        in_specs=[blk(QKP, fwd), ktf, blk(HW, fwd), gf, blk(QKP, bwd), ktb, blk(HW, bwd), gb],
        out_specs=[blk(HW, fwd), blk(HW, bwd)],
        out_shape=[jax.ShapeDtypeStruct((batch, seq, HW), BF16)] * 2,
        scratch_shapes=[pltpu.VMEM((bs, 2, HEADS, DK, DVP), F32), pltpu.VMEM((bs, 2, QKP, HW), BF16),
                        pltpu.VMEM((bs, 2, 2, SUBLANES, CHUNK), F32)],
        compiler_params=_cparams(("parallel", "arbitrary")), name="mlstm_mixer",
    )(q, kt, v3, gw4, q, kt, v3, gw4)


def _retention_direction(q, kt, v, tile_ref, row_ref, s_ref, sb_ref, d):
    s_all = _dot(q, _head_key_blocks(kt))
    inter_all = _dot(q, sb_ref[d])
    outs = []
    for h in range(HEADS):
        blk = slice(DVP * h, DVP * (h + 1))
        scores = (s_all[:, blk] * tile_ref[d, h, 0]).astype(BF16)
        outs.append(_dot(scores, v[:, blk]) + tile_ref[d, h, 1] * inter_all[:, blk])
    for h in range(HEADS):
        blk = slice(DVP * h, DVP * (h + 1))
        keys = slice(DK * h, DK * (h + 1))
        kp = (kt[keys, :].astype(F32) * row_ref[d, 0, h:h + 1, :]).astype(BF16)
        s_new = row_ref[d, 1, h:h + 1, :] * s_ref[d, h] + _dot(kp, v[:, blk])
        s_ref[d, h] = s_new
        sb_ref[d, keys, blk] = s_new.astype(BF16)
    return jnp.concatenate(outs, axis=1)


def _retention_kernel(q_f, kt_f, v_f, q_b, kt_b, v_b, tile_ref, row_ref, yf_ref, yb_ref, s_ref, sb_ref):
    @pl.when(pl.program_id(1) == 0)
    def _():
        s_ref[...] = jnp.zeros_like(s_ref)
        sb_ref[...] = jnp.zeros_like(sb_ref)

    for j in range(q_f.shape[0]):
        state = (s_ref.at[j], sb_ref.at[j])
        yf_ref[j] = _retention_direction(q_f[j], kt_f[j, 0], v_f[j], tile_ref, row_ref, *state, 0
                                         ).astype(yf_ref.dtype)
        yb_ref[j] = _retention_direction(q_b[j], kt_b[j, 0], v_b[j], tile_ref, row_ref, *state, 1
                                         ).astype(yb_ref.dtype)


def _retention(q, kt, v, tiles, rows, batch, seq):
    n, bs, fwd, bwd, blk, per_chunk = _mixer_specs(batch, seq)
    v3 = v.reshape(batch, seq, HW)
    ktf, ktb = per_chunk(QKP, CHUNK)
    return pl.pallas_call(
        _retention_kernel, grid=(batch // bs, n),
        in_specs=[blk(QKP, fwd), ktf, blk(HW, fwd), blk(QKP, bwd), ktb, blk(HW, bwd),
                  pl.BlockSpec(tiles.shape, lambda b, i: (0, 0, 0, 0, 0)),
                  pl.BlockSpec(rows.shape, lambda b, i: (0, 0, 0, 0))],
        out_specs=[blk(HW, fwd), blk(HW, bwd)],
        out_shape=[jax.ShapeDtypeStruct((batch, seq, HW), BF16)] * 2,
        scratch_shapes=[pltpu.VMEM((bs, 2, HEADS, DK, DVP), F32), pltpu.VMEM((bs, 2, QKP, HW), BF16)],
        compiler_params=_cparams(("parallel", "arbitrary")), name="retention_mixer",
    )(q, kt, v3, q, kt, v3, tiles, rows)


def _head_norm(z, w):
    valid = lax.broadcasted_iota(jnp.int32, (1, DVP), 1) < DV
    outs = []
    for h in range(HEADS):
        zh = z[:, DVP * h:DVP * (h + 1)]
        mu = jnp.sum(jnp.where(valid, zh, 0.0), axis=1, keepdims=True) * (1.0 / DV)
        d = jnp.where(valid, zh - mu, 0.0)
        var = jnp.sum(d * d, axis=1, keepdims=True) * (1.0 / DV)
        outs.append(d * lax.rsqrt(var + LN_EPS))
    return jnp.concatenate(outs, axis=1) * w


def _outproj_kernel(alpha, x_ref, yf_ref, hf_ref, hb_ref, o_ref, rf_ref, rb_ref, rg_ref, mlw_ref, rtw_ref,
                    w_ref, g_ref, b_ref, wrh_ref, wrl_ref, br_ref, x1_ref, x1b_ref, lg_ref):
    f32 = lambda ref: ref[...].astype(F32)
    y_m = _head_norm(jax.nn.sigmoid(f32(o_ref)) * (f32(hf_ref) + f32(hb_ref)), mlw_ref[...])
    rg = f32(rg_ref)
    y_r = rg * jax.nn.sigmoid(rg) * _head_norm(f32(rf_ref) + f32(rb_ref), rtw_ref[...])
    cat = jnp.concatenate([yf_ref[...], y_m, y_r], axis=1).astype(BF16)
    mix = _dot(cat, w_ref[...])
    x1 = _layernorm_rows(alpha * x_ref[...] + mix, g_ref[...], b_ref[...])
    x1_ref[...] = x1
    hi = x1.astype(BF16)
    x1b_ref[...] = hi
    lo = (x1 - hi.astype(F32)).astype(BF16)
    lg_ref[...] = _dot(hi, wrh_ref[...]) + _dot(lo, wrh_ref[...]) + _dot(hi, wrl_ref[...]) + br_ref[...]


def _outproj(x, yf, hf, hb, mo, rf, rb, rg, mlw, rtw, w, g, b, wrh, wrl, br, alpha):
    t, d = x.shape
    row = lambda n: pl.BlockSpec((ROW_TILE, n), lambda i: (i, 0))
    const = lambda a: pl.BlockSpec(a.shape, lambda i: (0, 0))
    args = (x, yf, hf, hb, mo, rf, rb, rg, mlw, rtw, w, g, b, wrh, wrl, br)
    in_specs = [row(d), row(FOURIER_W),
                row(HW), row(HW), row(HW), row(HW), row(HW), row(HW)] + [const(a) for a in args[8:]]
    return pl.pallas_call(
        functools.partial(_outproj_kernel, alpha), grid=(t // ROW_TILE,), in_specs=in_specs,
        out_specs=[row(d), row(d), row(LANES)],
        out_shape=[jax.ShapeDtypeStruct((t, d), F32), jax.ShapeDtypeStruct((t, d), BF16),
                   jax.ShapeDtypeStruct((t, LANES), F32)],
        compiler_params=_cparams(("parallel",)), name="mixer_output_projection",
    )(*args)


def _pack_halves(y):
    half = y.shape[1] // 2
    bits = lambda t: lax.bitcast_convert_type(t.astype(BF16).astype(F32), jnp.uint32)
    return (bits(y[:, :half]) >> 16) | (bits(y[:, half:]) & jnp.uint32(0xFFFF0000))


def _unpack_halves(w):
    lo = lax.bitcast_convert_type(w << 16, F32)
    hi = lax.bitcast_convert_type(w & jnp.uint32(0xFFFF0000), F32)
    return lo, hi


def _expert_kernel(be_ref, nb_ref, x_ref, w1_ref, b1_ref, w2_ref, b2_ref, o_ref, w1b_ref, w2b_ref):
    i = pl.program_id(0)

    @pl.when(jnp.logical_or(i == 0, be_ref[i] != be_ref[jnp.maximum(i - 1, 0)]))
    def _():
        w1b_ref[...] = w1_ref[0].astype(BF16)
        w2b_ref[...] = w2_ref[0].astype(BF16)

    @pl.when(i < nb_ref[0])
    def _():
        hc = _dot(x_ref[...], w1b_ref[...]) + b1_ref[0]
        gate = jnp.minimum(hc[:, :D_FF], SWIGLU_LIMIT)
        up = jnp.clip(hc[:, D_FF:], -SWIGLU_LIMIT, SWIGLU_LIMIT)
        glu = gate * jax.nn.sigmoid(SWIGLU_ALPHA * gate)
        y = _dot(((up + 1.0) * glu).astype(BF16), w2b_ref[...]) + b2_ref[0]
        o_ref[...] = _pack_halves(y)

    @pl.when(i >= nb_ref[0])
    def _():
        o_ref[...] = jnp.zeros_like(o_ref)


def _experts(block_e, n_blocks, xs, w1, b1, w2, b2, layer):
    p, d = xs.shape
    off = layer * N_EXPERTS
    wmap = lambda i, be, nb: (off + be[i], 0, 0)
    xmap = lambda i, be, nb: (jnp.minimum(i, nb[0] - 1), 0)
    grid_spec = pltpu.PrefetchScalarGridSpec(
        num_scalar_prefetch=2, grid=(p // MOE_BLOCK,),
        in_specs=[pl.BlockSpec((MOE_BLOCK, d), xmap),
                  pl.BlockSpec((1, d, 2 * D_FF), wmap), pl.BlockSpec((1, 1, 2 * D_FF), wmap),
                  pl.BlockSpec((1, D_FF, d), wmap), pl.BlockSpec((1, 1, d), wmap)],
        out_specs=pl.BlockSpec((MOE_BLOCK, d // 2), lambda i, be, nb: (i, 0)),
        scratch_shapes=[pltpu.VMEM((d, 2 * D_FF), BF16), pltpu.VMEM((D_FF, d), BF16)])
    return pl.pallas_call(
        _expert_kernel, grid_spec=grid_spec, out_shape=jax.ShapeDtypeStruct((p, d // 2), jnp.uint32),
        compiler_params=_cparams(("arbitrary",)), name="routed_experts",
    )(block_e, n_blocks, xs, w1, b1, w2, b2)


def _route_kernel(lg_ref, gate_ref, eid_ref, rank_ref, cnt_ref):
    i = pl.program_id(0)
    tm = lg_ref.shape[0]
    lane = lax.broadcasted_iota(jnp.int32, (tm, LANES), 1)
    x = jnp.where(lane < N_EXPERTS, lg_ref[...], -jnp.inf)
    vals, ids = [], []
    onehot = jnp.zeros((tm, LANES), F32)
    for _ in range(TOP_K):
        m = jnp.max(x, axis=1, keepdims=True)
        idx = jnp.min(jnp.where(x == m, lane, LANES), axis=1, keepdims=True)
        sel = lane == idx
        onehot = jnp.where(sel, 1.0, onehot)
        x = jnp.where(sel, -jnp.inf, x)
        vals.append(m)
        ids.append(idx)
    tile_counts = jnp.sum(onehot, axis=0, keepdims=True)

    @pl.when(i == 0)
    def _():
        cnt_ref[...] = jnp.zeros_like(cnt_ref)

    rr = lax.broadcasted_iota(jnp.int32, (tm, tm), 0)
    cc = lax.broadcasted_iota(jnp.int32, (tm, tm), 1)
    strict = jnp.where(cc < rr, 1.0, 0.0).astype(BF16)
    prior = _dot(strict, onehot.astype(BF16)) + cnt_ref[0:1, :]
    denom = jnp.ones_like(vals[0])
    for k in range(1, TOP_K):
        denom = denom + jnp.exp(vals[k] - vals[0])
    gates = jnp.zeros((tm, LANES), F32)
    eids = jnp.zeros((tm, LANES), jnp.int32)
    ranks = jnp.zeros((tm, LANES), jnp.int32)
    for k in range(TOP_K):
        rk = jnp.sum(jnp.where(lane == ids[k], prior, 0.0), axis=1, keepdims=True)
        gates = jnp.where(lane == k, jnp.exp(vals[k] - vals[0]) / denom, gates)
        eids = jnp.where(lane == k, ids[k], eids)
        ranks = jnp.where(lane == k, rk.astype(jnp.int32), ranks)
    gate_ref[...] = gates
    eid_ref[...] = eids
    rank_ref[...] = ranks
    cnt_ref[...] += tile_counts


def _route(logits):
    t = logits.shape[0]
    tile = pl.BlockSpec((ROW_TILE, LANES), lambda i: (i, 0))
    gates, eids, ranks, cnt = pl.pallas_call(
        _route_kernel, grid=(t // ROW_TILE,), in_specs=[tile],
        out_specs=[tile, tile, tile, pl.BlockSpec((SUBLANES, LANES), lambda i: (0, 0))],
        out_shape=[jax.ShapeDtypeStruct((t, LANES), F32), jax.ShapeDtypeStruct((t, LANES), jnp.int32),
                   jax.ShapeDtypeStruct((t, LANES), jnp.int32), jax.ShapeDtypeStruct((SUBLANES, LANES), F32)],
        compiler_params=_cparams(("arbitrary",)), name="router_topk",
    )(logits)
    n = t * TOP_K
    p = n + N_EXPERTS * MOE_BLOCK
    nb = p // MOE_BLOCK
    sizes = cnt[0, :N_EXPERTS].astype(jnp.int32)
    psizes = (sizes + MOE_BLOCK - 1) // MOE_BLOCK * MOE_BLOCK
    pends = jnp.cumsum(psizes)
    pstarts = pends - psizes
    starts = jnp.cumsum(sizes) - sizes
    block_e = jnp.minimum(jnp.searchsorted(pends, jnp.arange(nb, dtype=jnp.int32) * MOE_BLOCK, side='right'),
                          N_EXPERTS - 1).astype(jnp.int32)
    n_blocks = (pends[-1] // MOE_BLOCK).astype(jnp.int32).reshape(1)
    keys = eids[:, :TOP_K].reshape(n) * n + jnp.arange(n, dtype=jnp.int32)
    order = jnp.sort(keys) % n
    per_row = lambda per_block: jnp.repeat(per_block, MOE_BLOCK)
    local = jnp.arange(p, dtype=jnp.int32) - per_row(pstarts[block_e])
    size_r = per_row(sizes[block_e])
    pair = jnp.take(order, per_row(starts[block_e]) + local, mode="clip")
    src_tok = jnp.where(local < size_r, pair // TOP_K, 0)
    pos = jnp.take(pstarts, eids[:, :TOP_K], mode="clip") + ranks[:, :TOP_K]
    return gates, pos, src_tok, block_e, n_blocks


def _combine_kernel(alpha, x_ref, y_ref, gate_ref, g_ref, b_ref, o_ref, ob_ref):
    gates = gate_ref[...]
    lo = hi = None
    for k in range(TOP_K):
        lo_k, hi_k = _unpack_halves(y_ref[k])
        gk = gates[:, k:k + 1]
        lo = gk * lo_k if lo is None else lo + gk * lo_k
        hi = gk * hi_k if hi is None else hi + gk * hi_k
    moe = jnp.concatenate([lo, hi], axis=1)
    y = _layernorm_rows(alpha * x_ref[...] + moe, g_ref[...], b_ref[...])
    o_ref[...] = y
    ob_ref[...] = y.astype(BF16)


def _combine(x, yk, gates, g, b, alpha):
    t, d = x.shape
    row = pl.BlockSpec((ROW_TILE, d), lambda i: (i, 0))
    vec = pl.BlockSpec((1, d), lambda i: (0, 0))
    return pl.pallas_call(
        functools.partial(_combine_kernel, alpha), grid=(t // ROW_TILE,),
        in_specs=[row, pl.BlockSpec((TOP_K, ROW_TILE, d // 2), lambda i: (0, i, 0)),
                  pl.BlockSpec((ROW_TILE, LANES), lambda i: (i, 0)), vec, vec],
        out_specs=[row, row],
        out_shape=[jax.ShapeDtypeStruct((t, d), F32), jax.ShapeDtypeStruct((t, d), BF16)],
        compiler_params=_cparams(("parallel",)), name="expert_combine_layernorm",
    )(x, yk, gates, g.reshape(1, d), b.reshape(1, d))


def kernel(x, emb_ln_g, emb_ln_b, w_in, b_in, conv_w, ml_norm_w, ret_norm_w, w_out, ln1_g, ln1_b,
           w_router, b_router, w1, b1, w2, b2, ln2_g, ln2_b):
    batch, seq, d = x.shape
    depth = w_in.shape[0]
    assert d == D_MODEL and seq % ROW_TILE == 0
    t = batch * seq
    alpha = (2.0 * depth) ** 0.25

    w_in_p = _layout_proj(w_in).astype(BF16)
    den_cols = jnp.array([MV_OFFSET + DVP * h + DEN_COL + h for h in range(HEADS)])
    b_in_p = _layout_proj(b_in).at[:, den_cols].set(1.0)[:, None, :]
    wg_p = jnp.swapaxes(_layout_gates(w_in), -1, -2).astype(BF16)
    bg_p = _layout_gates(b_in)[:, :, None]
    hk = HEADS * DK
    conv_p = jnp.concatenate([_pad_last(conv_w[..., :hk], QKP), _pad_last(conv_w[..., hk:], QKP)], axis=-1)
    mlw_p = _pad_heads(ml_norm_w)[:, None, :]
    rtw_p = _pad_heads(ret_norm_w)[:, None, :]
    w_out_p = _layout_wout(w_out).astype(BF16)
    wr_hi, wr_lo = _split_bf16(_pad_last(w_router, LANES))
    br_p = _pad_last(b_router, LANES)[:, None, :]
    w1_r = w1.reshape(depth * N_EXPERTS, d, 2 * D_FF)
    w2_r = w2.reshape(depth * N_EXPERTS, D_FF, d)
    b1_r = b1.reshape(depth * N_EXPERTS, 1, 2 * D_FF)
    b2_r = b2.reshape(depth * N_EXPERTS, 1, d)

    stage1, stage2, chan = _dft_tables(seq)
    rc, rs1, rs2 = _rotary_tables(seq)
    ret_tiles, ret_rows = _retention_tables()

    xf, xb = _ln(x.reshape(t, d), emb_ln_g, emb_ln_b)
    for l in range(depth):
        pf, mqk, mv, mo, rqk, rv, rg, gates_t = _inproj(xb, w_in_p[l], b_in_p[l], wg_p[l], bg_p[l])
        zv, zw = _fourier_channels(pf, chan)
        yf = _fourier_sequence(stage1, stage2, zv, zw, batch, seq)
        mq, mkt, rq, rkt, gw = _mixer_prep(mqk, rqk, rc, rs1, rs2, conv_p[l], gates_t, batch, seq)
        hf, hb = _mlstm(mq, mkt, mv, gw, batch, seq)
        rf, rb = _retention(rq, rkt, rv, ret_tiles, ret_rows, batch, seq)
        x1, x1b, logits = _outproj(
            xf, yf, hf.reshape(t, HW), hb.reshape(t, HW), mo, rf.reshape(t, HW), rb.reshape(t, HW), rg,
            mlw_p[l], rtw_p[l], w_out_p[l], ln1_g[l][None, :], ln1_b[l][None, :],
            wr_hi[l], wr_lo[l], br_p[l], alpha)
        gates, pos, src_tok, block_e, n_blocks = _route(logits)
        xs = jnp.take(x1b, src_tok, axis=0, mode="clip")
        ys = _experts(block_e, n_blocks, xs, w1_r, b1_r, w2_r, b2_r, l)
        yk = jnp.take(ys, pos.T.reshape(TOP_K * t), axis=0, mode="clip").reshape(TOP_K, t, d // 2)
        xf, xb = _combine(x1, yk, gates, ln2_g[l], ln2_b[l], alpha)
    return xf.reshape(batch, seq, d)
```

```python
import functools

import jax
import jax.numpy as jnp
from jax import lax
from jax.experimental import pallas as pl
from jax.experimental.pallas import tpu as pltpu

F32 = jnp.float32
BF16 = jnp.bfloat16

D_MODEL = 1024
CHUNK = 128
FOURIER_W = D_MODEL // 4
N_FGROUPS = 4
FG_W = FOURIER_W // N_FGROUPS
ML_W = 3 * D_MODEL // 8
RET_W = D_MODEL - FOURIER_W - ML_W
HEADS = 4
DV = ML_W // HEADS
DK = DV // 2
CONV_W = 3
ROPE_BASE = 10000.0
RET_GAMMA_EXP0 = 5.0
RET_BWD_EXP_OFFSET = 0.5
N_EXPERTS = 32
TOP_K = 4
D_FF = D_MODEL
SWIGLU_ALPHA = 1.702
SWIGLU_LIMIT = 7.0
LN_EPS = 1e-5

COL_F = 0
COL_MQK = COL_F + FOURIER_W
COL_MV = COL_MQK + 2 * HEADS * DK
COL_MO = COL_MV + ML_W
COL_MG = COL_MO + ML_W
COL_RQ = COL_MG + 4 * HEADS
COL_RK = COL_RQ + HEADS * DK
COL_RV = COL_RK + HEADS * DK
COL_RG = COL_RV + RET_W
PROJ_W = COL_RG + RET_W

LANES = 128
SUBLANES = 8
DVP = LANES
QKP = 2 * LANES
HW = HEADS * DVP
DEN_COL = DV
GATE_ROWS = 4 * SUBLANES
VMEM_LIMIT = 52 * 1024 * 1024

PROJ_GROUPS = (FOURIER_W, 2 * QKP, HW, HW, 2 * QKP, HW, HW)
PROJ_DTYPES = (BF16, F32, BF16, BF16, F32, BF16, BF16)
MV_OFFSET = FOURIER_W + 2 * QKP
MIX_P = FOURIER_W + 2 * HW

FFT_N1 = 128
ROW_TILE = 512
MOE_BLOCK = 512
MIXER_BATCH = 2

def _cparams(sem):
    return pltpu.CompilerParams(dimension_semantics=sem, vmem_limit_bytes=VMEM_LIMIT)


def _pad_last(w, n):
    return jnp.pad(w, [(0, 0)] * (w.ndim - 1) + [(0, n - w.shape[-1])])


def _pad_heads(w):
    lead = w.shape[:-1]
    w = w.reshape(*lead, HEADS, DV)
    w = jnp.pad(w, [(0, 0)] * (len(lead) + 1) + [(0, DVP - DV)])
    return w.reshape(*lead, HW)


def _pair_split(w):
    lead = w.shape[:-1]
    w = w.reshape(*lead, HEADS, DK // 2, 2)
    w = jnp.swapaxes(w, -1, -2).reshape(*lead, HEADS * DK)
    return _pad_last(w, QKP)


def _layout_proj(w):
    hk = HEADS * DK
    return jnp.concatenate([
        w[..., COL_F:COL_MQK],
        _pad_last(w[..., COL_MQK:COL_MQK + hk], QKP), _pad_last(w[..., COL_MQK + hk:COL_MV], QKP),
        _pad_heads(w[..., COL_MV:COL_MO]), _pad_heads(w[..., COL_MO:COL_MG]),
        _pair_split(w[..., COL_RQ:COL_RK]), _pair_split(w[..., COL_RK:COL_RV]),
        _pad_heads(w[..., COL_RV:COL_RG]), _pad_heads(w[..., COL_RG:PROJ_W]),
    ], axis=-1)


def _layout_gates(w):
    g = w[..., COL_MG:COL_RQ]
    pick = lambda j: _pad_last(g[..., HEADS * j:HEADS * (j + 1)], SUBLANES)
    return jnp.concatenate([pick(1), pick(0), pick(3), pick(2)], axis=-1)


def _layout_wout(w):
    wt = jnp.swapaxes(w, -1, -2)
    wt = jnp.concatenate([wt[..., :FOURIER_W], _pad_heads(wt[..., FOURIER_W:FOURIER_W + ML_W]),
                          _pad_heads(wt[..., FOURIER_W + ML_W:])], axis=-1)
    return jnp.swapaxes(wt, -1, -2)


def _split_bf16(w):
    hi = w.astype(BF16)
    return hi, (w - hi.astype(F32)).astype(BF16)


def _dft_tables(seq):
    n1c, n2c, g = FFT_N1, seq // FFT_N1, SUBLANES
    eye = jnp.eye(g, dtype=F32)
    ar = lambda n: jnp.arange(n, dtype=jnp.int32)
    n1 = (ar(n1c // g)[:, None] * g + ar(g)[None, :])[:, None, None, :]
    kn = (ar(n2c)[None, :, None, None] * (n1 + n1c * ar(n2c)[None, None, :, None])) % seq
    ang = kn.astype(F32) * (2.0 * jnp.pi / seq)
    spread = lambda t: jnp.einsum('aknj,jl->akjnl', t, eye).reshape(n1c // g, n2c * g, n2c * g)
    gr, gi = spread(jnp.cos(ang) * n2c ** -0.5), spread(-jnp.sin(ang) * n2c ** -0.5)
    stage1 = jnp.concatenate([jnp.concatenate([gr, gi], axis=2),
                              jnp.concatenate([gi, -gr], axis=2)], axis=1).astype(BF16)
    phi = ((ar(n1c)[:, None] * ar(n1c)[None, :]) % n1c).astype(F32) * (2.0 * jnp.pi / n1c)
    spread2 = lambda t: jnp.einsum('kn,jl->kjln', t, eye).reshape(n1c * g, g * n1c)
    stage2 = jnp.concatenate([spread2(jnp.cos(phi) * n1c ** -0.5),
                              spread2(jnp.sin(phi) * n1c ** -0.5)], axis=1).astype(BF16)
    c = jnp.arange(FG_W, dtype=jnp.int32)
    cc = ((c[:, None] * c[None, :]) % FG_W).astype(F32) * (2.0 * jnp.pi / FG_W)
    eye = jnp.eye(N_FGROUPS, dtype=F32)
    bd_c = jnp.kron(eye, jnp.cos(cc) * FG_W ** -0.5)
    bd_s = jnp.kron(eye, jnp.sin(cc) * FG_W ** -0.5)
    chan = jnp.concatenate([bd_c, bd_s], axis=1).astype(BF16)
    return stage1, stage2, chan


def _rotary_tables(seq):
    inv = 1.0 / (ROPE_BASE ** (jnp.arange(0, DK, 2, dtype=F32) / DK))
    ang = jnp.arange(seq, dtype=F32)[:, None] * inv[None, :]
    cos, sin = jnp.cos(ang), jnp.sin(ang)
    zero = jnp.zeros_like(sin)
    heads = lambda a, b: _pad_last(jnp.tile(jnp.concatenate([a, b], axis=1), (1, HEADS)), QKP)
    return heads(cos, cos), heads(-sin, zero), heads(zero, sin)


def _retention_tables():
    idx = jnp.arange(CHUNK, dtype=F32)
    diff = idx[:, None] - idx[None, :]
    tiles, rows = [], []
    for rev in (False, True):
        offset = RET_BWD_EXP_OFFSET if rev else 0.0
        lg = jnp.log1p(-jnp.exp2(-(RET_GAMMA_EXP0 + offset) - jnp.arange(HEADS, dtype=F32)))
        lg3 = lg[:, None, None]
        if rev:
            decay = jnp.where((diff < 0)[None], jnp.exp(lg3 * jnp.maximum(-diff, 0.0)[None]), 0.0)
            w_inter = jnp.exp(lg[:, None] * (CHUNK - idx)[None, :])
            w_key = jnp.exp(lg[:, None] * idx[None, :])
        else:
            decay = jnp.where((diff >= 0)[None], jnp.exp(lg3 * jnp.maximum(diff, 0.0)[None]), 0.0)
            w_inter = jnp.exp(lg[:, None] * (idx + 1.0)[None, :])
            w_key = jnp.exp(lg[:, None] * (CHUNK - 1 - idx)[None, :])
        g_chunk = jnp.broadcast_to(jnp.exp(lg * CHUNK)[:, None], (HEADS, CHUNK))
        tiles.append(jnp.stack([decay, jnp.broadcast_to(w_inter[:, :, None], (HEADS, CHUNK, LANES))], axis=1))
        pad = lambda t: jnp.pad(t, ((0, SUBLANES - HEADS), (0, 0)))
        rows.append(jnp.stack([pad(w_key), pad(g_chunk)], axis=0))
    return jnp.stack(tiles, axis=0), jnp.stack(rows, axis=0)


def _layernorm_rows(z, g, b):
    mu = jnp.mean(z, axis=-1, keepdims=True)
    d = z - mu
    var = jnp.mean(d * d, axis=-1, keepdims=True)
    return d * lax.rsqrt(var + LN_EPS) * g + b


def _ln_kernel(x_ref, g_ref, b_ref, o_ref, ob_ref):
    y = _layernorm_rows(x_ref[...], g_ref[...], b_ref[...])
    o_ref[...] = y
    ob_ref[...] = y.astype(BF16)


def _ln(x, g, b):
    t, d = x.shape
    row = pl.BlockSpec((ROW_TILE, d), lambda i: (i, 0))
    vec = pl.BlockSpec((1, d), lambda i: (0, 0))
    return pl.pallas_call(
        _ln_kernel, grid=(t // ROW_TILE,), in_specs=[row, vec, vec], out_specs=[row, row],
        out_shape=[jax.ShapeDtypeStruct((t, d), F32), jax.ShapeDtypeStruct((t, d), BF16)],
        compiler_params=_cparams(("parallel",)), name="input_layernorm",
    )(x, g.reshape(1, d), b.reshape(1, d))


def _inproj_kernel(x_ref, w_ref, b_ref, wg_ref, bg_ref, *o_refs):
    x = x_ref[...]
    c0 = 0
    for o_ref in o_refs[:-1]:
        n = o_ref.shape[-1]
        y = jnp.dot(x, w_ref[:, c0:c0 + n], preferred_element_type=F32) + b_ref[:, c0:c0 + n]
        o_ref[...] = y.astype(o_ref.dtype)
        c0 += n
    gt = lax.dot_general(wg_ref[...], x, (((1,), (1,)), ((), ())), preferred_element_type=F32)
    o_refs[-1][...] = gt + bg_ref[...]


def _inproj(xb, w, b, wg, bg):
    t, d = xb.shape
    n_p = w.shape[1]
    const = lambda a: pl.BlockSpec(a.shape, lambda i: (0, 0))
    return pl.pallas_call(
        _inproj_kernel, grid=(t // ROW_TILE,),
        in_specs=[pl.BlockSpec((ROW_TILE, d), lambda i: (i, 0)), const(w), const(b), const(wg), const(bg)],
        out_specs=[pl.BlockSpec((ROW_TILE, n), lambda i: (i, 0)) for n in PROJ_GROUPS]
        + [pl.BlockSpec((GATE_ROWS, ROW_TILE), lambda i: (0, i))],
        out_shape=[jax.ShapeDtypeStruct((t, n), dt) for n, dt in zip(PROJ_GROUPS, PROJ_DTYPES)]
        + [jax.ShapeDtypeStruct((GATE_ROWS, t), F32)],
        compiler_params=_cparams(("parallel",)), name="input_projection",
    )(xb, w, b, wg, bg)


def _fchan_kernel(u_ref, m_ref, v_ref, w_ref):
    vw = jnp.dot(u_ref[...], m_ref[...], preferred_element_type=F32)
    v_ref[...] = vw[:, :FOURIER_W]
    w_ref[...] = vw[:, FOURIER_W:]


def _fourier_channels(u, chan):
    t = u.shape[0]
    row = pl.BlockSpec((ROW_TILE, FOURIER_W), lambda i: (i, 0))
    return pl.pallas_call(
        _fchan_kernel, grid=(t // ROW_TILE,),
        in_specs=[row, pl.BlockSpec((FOURIER_W, 2 * FOURIER_W), lambda i: (0, 0))],
        out_specs=[row, row], out_shape=[jax.ShapeDtypeStruct((t, FOURIER_W), F32)] * 2,
        compiler_params=_cparams(("parallel",)), name="fourier_channel_dft",
    )(u, chan)


def _fstage1_kernel(m_ref, v_ref, w_ref, br_ref, bi_ref):
    rows = v_ref.shape[1] * SUBLANES
    x = jnp.concatenate([v_ref[0].reshape(rows, FOURIER_W), w_ref[0].reshape(rows, FOURIER_W)], axis=0)
    y = jnp.dot(m_ref[0], x.astype(BF16), preferred_element_type=F32)
    br_ref[0] = y[:rows].reshape(br_ref.shape[1:])
    bi_ref[0] = y[rows:].reshape(bi_ref.shape[1:])


def _fstage2_kernel(m_ref, br_ref, bi_ref, o_ref):
    rows = br_ref.shape[2] * SUBLANES
    x = jnp.concatenate([br_ref[0].reshape(rows, FOURIER_W), bi_ref[0].reshape(rows, FOURIER_W)], axis=0)
    y = jnp.dot(m_ref[...], x.astype(BF16), preferred_element_type=F32)
    o_ref[0] = y.reshape(o_ref.shape[1:])


def _fourier_sequence(stage1, stage2, v, w, batch, seq):
    n1c, n2c, g = FFT_N1, seq // FFT_N1, SUBLANES
    view = lambda a: a.reshape(batch, n2c, n1c, FOURIER_W)
    blk1 = pl.BlockSpec((1, n2c, g, FOURIER_W), lambda a, b: (b, 0, a, 0))
    shape1 = jax.ShapeDtypeStruct((batch, n2c, n1c, FOURIER_W), F32)
    br, bi = pl.pallas_call(
        _fstage1_kernel, grid=(n1c // g, batch),
        in_specs=[pl.BlockSpec((1,) + stage1.shape[1:], lambda a, b: (a, 0, 0)), blk1, blk1],
        out_specs=[blk1, blk1], out_shape=[shape1, shape1],
        compiler_params=_cparams(("parallel", "parallel")), name="fourier_sequence_stage1",
    )(stage1, view(v), view(w))
    blk2 = pl.BlockSpec((1, g, n1c, FOURIER_W), lambda kb, b: (b, kb, 0, 0))
    y = pl.pallas_call(
        _fstage2_kernel, grid=(n2c // g, batch),
        in_specs=[pl.BlockSpec(stage2.shape, lambda kb, b: (0, 0)), blk2, blk2],
        out_specs=pl.BlockSpec((1, n1c, g, FOURIER_W), lambda kb, b: (b, 0, kb, 0)),
        out_shape=jax.ShapeDtypeStruct((batch, n1c, n2c, FOURIER_W), F32),
        compiler_params=_cparams(("parallel", "parallel")), name="fourier_sequence_stage2",
    )(stage2, br, bi)
    return y.reshape(batch * seq, FOURIER_W)


def _dot(a, b):
    return jnp.dot(a, b, preferred_element_type=F32)


def _cumsum_dot(tri, x, tri_left):
    hi = x.astype(BF16)
    r1 = x - hi.astype(F32)
    mid = r1.astype(BF16)
    lo = (r1 - mid.astype(F32)).astype(BF16)
    if tri_left:
        return _dot(tri, hi) + _dot(tri, mid) + _dot(tri, lo)
    return _dot(hi, tri) + _dot(mid, tri) + _dot(lo, tri)


def _log_sigmoid(x):
    return jnp.minimum(x, 0.0) - jnp.log1p(jnp.exp(-jnp.abs(x)))


def _mixer_prep_kernel(mqk_ref, prev_ref, next_ref, rqk_ref, rc_ref, rs1_ref, rs2_ref, convw_ref, g_ref,
                       mq_ref, mkt_ref, rq_ref, rkt_ref, gw_ref):
    i = pl.program_id(1)
    rows_n = mqk_ref.shape[1]
    chunks = rows_n // CHUNK
    n = CHUNK
    r2 = lax.broadcasted_iota(jnp.int32, (n, n), 0)
    c2 = lax.broadcasted_iota(jnp.int32, (n, n), 1)
    for c in range(chunks):
        lanes = slice(c * n, (c + 1) * n)
        groups = []
        for rev in (False, True):
            r0 = 2 * SUBLANES if rev else 0
            tri = jnp.where((c2 <= r2) if rev else (c2 >= r2), 1.0, 0.0).astype(BF16)
            a = _cumsum_dot(tri, _log_sigmoid(g_ref[r0:r0 + SUBLANES, lanes]), False)
            groups += [g_ref[r0 + SUBLANES:r0 + 2 * SUBLANES, lanes] - a, a]
        gw_ref[c] = jnp.concatenate(groups, axis=0)

    qk = mqk_ref[0]
    rows = lax.broadcasted_iota(jnp.int32, (rows_n, 2 * QKP), 0)
    prev_row = prev_ref[0, 0, SUBLANES - 1:SUBLANES, :] * jnp.where(i == 0, 0.0, 1.0)
    next_row = next_ref[0, 0, 0:1, :] * jnp.where(i == pl.num_programs(1) - 1, 0.0, 1.0)
    xm1 = jnp.where(rows == 0, prev_row, pltpu.roll(qk, 1, 0))
    xp1 = jnp.where(rows == rows_n - 1, next_row, pltpu.roll(qk, rows_n - 1, 0))
    conv = xm1 * convw_ref[0:1, :] + qk * convw_ref[1:2, :] + xp1 * convw_ref[2:3, :]
    act = conv * jax.nn.sigmoid(conv)
    mq_ref[0] = act[:, :QKP].astype(BF16)

    rc, rs1, rs2 = rc_ref[...], rs1_ref[...], rs2_ref[...]
    half = DK // 2

    def rotate(t):
        return t * rc + pltpu.roll(t, QKP - half, 1) * rs1 + pltpu.roll(t, half, 1) * rs2

    rqk = rqk_ref[0]
    rq_ref[0] = rotate(rqk[:, :QKP]).astype(BF16)
    mk = act[:, QKP:] * DK ** -0.5
    rk = rotate(rqk[:, QKP:]) * DK ** -0.5
    for c in range(chunks):
        mkt_ref[0, c] = mk[c * n:(c + 1) * n, :].T.astype(BF16)
        rkt_ref[0, c] = rk[c * n:(c + 1) * n, :].T.astype(BF16)


def _mixer_prep(mqk, rqk, rc, rs1, rs2, convw, gates, batch, seq):
    n = seq // CHUNK
    tiles = seq // ROW_TILE
    chunks = ROW_TILE // CHUNK
    g8 = ROW_TILE // SUBLANES
    mqk3 = mqk.reshape(batch, seq, 2 * QKP)
    mqk8 = mqk.reshape(batch, seq // SUBLANES, SUBLANES, 2 * QKP)
    rqk3 = rqk.reshape(batch, seq, 2 * QKP)
    tile = lambda w: pl.BlockSpec((1, ROW_TILE, w), lambda b, i: (b, i, 0))
    halo = lambda index: pl.BlockSpec((1, 1, SUBLANES, 2 * QKP), index)
    pos = pl.BlockSpec((ROW_TILE, QKP), lambda b, i: (i, 0))
    kt = pl.BlockSpec((1, chunks, QKP, CHUNK), lambda b, i: (b, i, 0, 0))
    q_shape = jax.ShapeDtypeStruct((batch, seq, QKP), BF16)
    kt_shape = jax.ShapeDtypeStruct((batch, n, QKP, CHUNK), BF16)
    return pl.pallas_call(
        _mixer_prep_kernel, grid=(batch, tiles),
        in_specs=[tile(2 * QKP),
                  halo(lambda b, i: (b, jnp.maximum(i * g8 - 1, 0), 0, 0)),
                  halo(lambda b, i: (b, jnp.minimum((i + 1) * g8, seq // SUBLANES - 1), 0, 0)),
                  tile(2 * QKP), pos, pos, pos,
                  pl.BlockSpec((CONV_W, 2 * QKP), lambda b, i: (0, 0)),
                  pl.BlockSpec((GATE_ROWS, ROW_TILE), lambda b, i: (0, b * tiles + i))],
        out_specs=[tile(QKP), kt, tile(QKP), kt,
                   pl.BlockSpec((chunks, GATE_ROWS, CHUNK), lambda b, i: (b * tiles + i, 0, 0))],
        out_shape=[q_shape, kt_shape, q_shape, kt_shape,
                   jax.ShapeDtypeStruct((batch * n, GATE_ROWS, CHUNK), F32)],
        compiler_params=_cparams(("parallel", "parallel")), name="mixer_qk_prep",
    )(mqk3, mqk8, mqk8, rqk3, rc, rs1, rs2, convw, gates)


def _head_key_blocks(kt):
    rows = lax.broadcasted_iota(jnp.int32, kt.shape, 0)
    zero = jnp.zeros_like(kt)
    return jnp.concatenate([jnp.where((rows >= DK * h) & (rows < DK * (h + 1)), kt, zero)
                            for h in range(HEADS)], axis=1)


def _mlstm_direction(rev, q, kt, v, gw, c_ref, cb_ref, m_ref, d):
    n = CHUNK
    r0 = 2 * SUBLANES if rev else 0
    key_w = gw[r0:r0 + SUBLANES, :]
    a = gw[r0 + SUBLANES:r0 + 2 * SUBLANES, :]
    r2 = lax.broadcasted_iota(jnp.int32, (n, n), 0)
    c2 = lax.broadcasted_iota(jnp.int32, (n, n), 1)
    causal = (c2 >= r2) if rev else (c2 <= r2)
    lane = lax.broadcasted_iota(jnp.int32, (n, LANES), 1)

    cmax = jnp.zeros((n, LANES), F32)
    for h in range(HEADS):
        cm_h = jnp.max(jnp.where(causal, key_w[h:h + 1, :], -jnp.inf), axis=1, keepdims=True)
        cmax = jnp.where(lane == DEN_COL + h, cm_h, cmax)
    zero = jnp.zeros((SUBLANES, n), F32)
    den_group = DEN_COL // SUBLANES
    a_col = jnp.concatenate([zero] * den_group + [a] + [zero] * (n // SUBLANES - den_group - 1), axis=0).T
    m_lane = m_ref[d, 1, 0:1, :]
    mm = jnp.maximum(m_lane, cmax)
    s_inter = jnp.exp(m_lane - mm)
    e_negm = jnp.exp(-a_col - mm)

    s_all = _dot(q, _head_key_blocks(kt))
    inter_all = _dot(q, cb_ref[d])
    sr = lax.broadcasted_iota(jnp.int32, (LANES, HW), 0)
    sc = lax.broadcasted_iota(jnp.int32, (LANES, HW), 1)
    spread = jnp.where(sr - DEN_COL == sc // DVP, 1.0, 0.0).astype(BF16)
    inter_all = _dot(s_inter.astype(BF16), spread) * inter_all
    rs = []
    for h in range(HEADS):
        blk = slice(DVP * h, DVP * (h + 1))
        col = slice(DEN_COL + h, DEN_COL + h + 1)
        p = jnp.exp(jnp.where(causal, key_w[h:h + 1, :] - mm[:, col], -jnp.inf))
        scores = (s_all[:, blk] * p).astype(BF16)
        rs.append(_dot(scores, v[:, blk]) + inter_all[:, blk])
    den = rs[0]
    for h in range(1, HEADS):
        den = den + rs[h]
    rden = 1.0 / jnp.maximum(jnp.abs(den), e_negm)
    out = jnp.concatenate(rs, axis=1) * _dot(rden.astype(BF16), spread)

    m_prev = m_ref[d, 0]
    a_end = jnp.broadcast_to(a[:, 0:1] if rev else a[:, n - 1:n], (SUBLANES, n))
    w_key = a_end + key_w
    m_chunk = jnp.broadcast_to(jnp.max(w_key, axis=1, keepdims=True), (SUBLANES, n))
    m_new = jnp.maximum(a_end + m_prev, m_chunk)
    s_prev = jnp.exp(a_end + m_prev - m_new)
    p_key = jnp.exp(w_key - m_chunk) * jnp.exp(m_chunk - m_new)
    m_ref[d, 0] = m_new
    row8 = lax.broadcasted_iota(jnp.int32, (SUBLANES, n), 0)
    lane8 = lax.broadcasted_iota(jnp.int32, (SUBLANES, n), 1)
    on_diag = (lane8 == row8 + DEN_COL) & (row8 < HEADS)
    m_ref[d, 1] = jnp.broadcast_to(jnp.sum(jnp.where(on_diag, m_new, 0.0), axis=0, keepdims=True),
                                   (SUBLANES, n))
    for h in range(HEADS):
        blk = slice(DVP * h, DVP * (h + 1))
        keys = slice(DK * h, DK * (h + 1))
        kp = (kt[keys, :].astype(F32) * p_key[h:h + 1, :]).astype(BF16)
        c_new = s_prev[h:h + 1, :] * c_ref[d, h] + _dot(kp, v[:, blk])
        c_ref[d, h] = c_new
        cb_ref[d, keys, blk] = c_new.astype(BF16)
    return out


def _mlstm_kernel(q_f, kt_f, v_f, g_f, q_b, kt_b, v_b, g_b, hf_ref, hb_ref, c_ref, cb_ref, m_ref):
    @pl.when(pl.program_id(1) == 0)
    def _():
        c_ref[...] = jnp.zeros_like(c_ref)
        cb_ref[...] = jnp.zeros_like(cb_ref)
        m_ref[...] = jnp.zeros_like(m_ref)

    for j in range(q_f.shape[0]):
        state = (c_ref.at[j], cb_ref.at[j], m_ref.at[j])
        hf_ref[j] = _mlstm_direction(False, q_f[j], kt_f[j, 0], v_f[j], g_f[j, 0], *state, 0).astype(hf_ref.dtype)
        hb_ref[j] = _mlstm_direction(True, q_b[j], kt_b[j, 0], v_b[j], g_b[j, 0], *state, 1).astype(hb_ref.dtype)


def _mixer_specs(batch, seq):
    n = seq // CHUNK
    bs = MIXER_BATCH if batch % MIXER_BATCH == 0 else 1
    fwd = lambda b, i: (b, i, 0)
    bwd = lambda b, i: (b, n - 1 - i, 0)
    blk = lambda w, im: pl.BlockSpec((bs, CHUNK, w), im)
    per_chunk = lambda rows, cols: (
        pl.BlockSpec((bs, 1, rows, cols), lambda b, i: (b, i, 0, 0)),
        pl.BlockSpec((bs, 1, rows, cols), lambda b, i: (b, n - 1 - i, 0, 0)))
    return n, bs, fwd, bwd, blk, per_chunk


def _mlstm(q, kt, v, gw, batch, seq):
    n, bs, fwd, bwd, blk, per_chunk = _mixer_specs(batch, seq)
    v3 = v.reshape(batch, seq, HW)
    gw4 = gw.reshape(batch, n, GATE_ROWS, CHUNK)
    ktf, ktb = per_chunk(QKP, CHUNK)
    gf, gb = per_chunk(GATE_ROWS, CHUNK)
    return pl.pallas_call(
        _mlstm_kernel, grid=(batch // bs, n),
        in_specs=[blk(QKP, fwd), ktf, blk(HW, fwd), gf, blk(QKP, bwd), ktb, blk(HW, bwd), gb],
        out_specs=[blk(HW, fwd), blk(HW, bwd)],
        out_shape=[jax.ShapeDtypeStruct((batch, seq, HW), BF16)] * 2,
        scratch_shapes=[pltpu.VMEM((bs, 2, HEADS, DK, DVP), F32), pltpu.VMEM((bs, 2, QKP, HW), BF16),
                        pltpu.VMEM((bs, 2, 2, SUBLANES, CHUNK), F32)],
        compiler_params=_cparams(("parallel", "arbitrary")), name="mlstm_mixer",
    )(q, kt, v3, gw4, q, kt, v3, gw4)


def _retention_direction(q, kt, v, tile_ref, row_ref, s_ref, sb_ref, d):
    s_all = _dot(q, _head_key_blocks(kt))
    inter_all = _dot(q, sb_ref[d])
    outs = []
    for h in range(HEADS):
        blk = slice(DVP * h, DVP * (h + 1))
        scores = (s_all[:, blk] * tile_ref[d, h, 0]).astype(BF16)
        outs.append(_dot(scores, v[:, blk]) + tile_ref[d, h, 1] * inter_all[:, blk])
    for h in range(HEADS):
        blk = slice(DVP * h, DVP * (h + 1))
        keys = slice(DK * h, DK * (h + 1))
        kp = (kt[keys, :].astype(F32) * row_ref[d, 0, h:h + 1, :]).astype(BF16)
        s_new = row_ref[d, 1, h:h + 1, :] * s_ref[d, h] + _dot(kp, v[:, blk])
        s_ref[d, h] = s_new
        sb_ref[d, keys, blk] = s_new.astype(BF16)
    return jnp.concatenate(outs, axis=1)


def _retention_kernel(q_f, kt_f, v_f, q_b, kt_b, v_b, tile_ref, row_ref, yf_ref, yb_ref, s_ref, sb_ref):
    @pl.when(pl.program_id(1) == 0)
    def _():
        s_ref[...] = jnp.zeros_like(s_ref)
        sb_ref[...] = jnp.zeros_like(sb_ref)

    for j in range(q_f.shape[0]):
        state = (s_ref.at[j], sb_ref.at[j])
        yf_ref[j] = _retention_direction(q_f[j], kt_f[j, 0], v_f[j], tile_ref, row_ref, *state, 0
                                         ).astype(yf_ref.dtype)
        yb_ref[j] = _retention_direction(q_b[j], kt_b[j, 0], v_b[j], tile_ref, row_ref, *state, 1
                                         ).astype(yb_ref.dtype)


def _retention(q, kt, v, tiles, rows, batch, seq):
    n, bs, fwd, bwd, blk, per_chunk = _mixer_specs(batch, seq)
    v3 = v.reshape(batch, seq, HW)
    ktf, ktb = per_chunk(QKP, CHUNK)
    return pl.pallas_call(
        _retention_kernel, grid=(batch // bs, n),
        in_specs=[blk(QKP, fwd), ktf, blk(HW, fwd), blk(QKP, bwd), ktb, blk(HW, bwd),
                  pl.BlockSpec(tiles.shape, lambda b, i: (0, 0, 0, 0, 0)),
                  pl.BlockSpec(rows.shape, lambda b, i: (0, 0, 0, 0))],
        out_specs=[blk(HW, fwd), blk(HW, bwd)],
        out_shape=[jax.ShapeDtypeStruct((batch, seq, HW), BF16)] * 2,
        scratch_shapes=[pltpu.VMEM((bs, 2, HEADS, DK, DVP), F32), pltpu.VMEM((bs, 2, QKP, HW), BF16)],
        compiler_params=_cparams(("parallel", "arbitrary")), name="retention_mixer",
    )(q, kt, v3, q, kt, v3, tiles, rows)


def _head_norm(z, w):
    valid = lax.broadcasted_iota(jnp.int32, (1, DVP), 1) < DV
    outs = []
    for h in range(HEADS):
        zh = z[:, DVP * h:DVP * (h + 1)]
        mu = jnp.sum(jnp.where(valid, zh, 0.0), axis=1, keepdims=True) * (1.0 / DV)
        d = jnp.where(valid, zh - mu, 0.0)
        var = jnp.sum(d * d, axis=1, keepdims=True) * (1.0 / DV)
        outs.append(d * lax.rsqrt(var + LN_EPS))
    return jnp.concatenate(outs, axis=1) * w


def _outproj_kernel(alpha, x_ref, yf_ref, hf_ref, hb_ref, o_ref, rf_ref, rb_ref, rg_ref, mlw_ref, rtw_ref,
                    w_ref, g_ref, b_ref, wrh_ref, wrl_ref, br_ref, x1_ref, x1b_ref, lg_ref):
    f32 = lambda ref: ref[...].astype(F32)
    y_m = _head_norm(jax.nn.sigmoid(f32(o_ref)) * (f32(hf_ref) + f32(hb_ref)), mlw_ref[...])
    rg = f32(rg_ref)
    y_r = rg * jax.nn.sigmoid(rg) * _head_norm(f32(rf_ref) + f32(rb_ref), rtw_ref[...])
    cat = jnp.concatenate([yf_ref[...], y_m, y_r], axis=1).astype(BF16)
    mix = _dot(cat, w_ref[...])
    x1 = _layernorm_rows(alpha * x_ref[...] + mix, g_ref[...], b_ref[...])
    x1_ref[...] = x1
    hi = x1.astype(BF16)
    x1b_ref[...] = hi
    lo = (x1 - hi.astype(F32)).astype(BF16)
    lg_ref[...] = _dot(hi, wrh_ref[...]) + _dot(lo, wrh_ref[...]) + _dot(hi, wrl_ref[...]) + br_ref[...]


def _outproj(x, yf, hf, hb, mo, rf, rb, rg, mlw, rtw, w, g, b, wrh, wrl, br, alpha):
    t, d = x.shape
    row = lambda n: pl.BlockSpec((ROW_TILE, n), lambda i: (i, 0))
    const = lambda a: pl.BlockSpec(a.shape, lambda i: (0, 0))
    args = (x, yf, hf, hb, mo, rf, rb, rg, mlw, rtw, w, g, b, wrh, wrl, br)
    in_specs = [row(d), row(FOURIER_W),
                row(HW), row(HW), row(HW), row(HW), row(HW), row(HW)] + [const(a) for a in args[8:]]
    return pl.pallas_call(
        functools.partial(_outproj_kernel, alpha), grid=(t // ROW_TILE,), in_specs=in_specs,
        out_specs=[row(d), row(d), row(LANES)],
        out_shape=[jax.ShapeDtypeStruct((t, d), F32), jax.ShapeDtypeStruct((t, d), BF16),
                   jax.ShapeDtypeStruct((t, LANES), F32)],
        compiler_params=_cparams(("parallel",)), name="mixer_output_projection",
    )(*args)


def _pack_halves(y):
    half = y.shape[1] // 2
    bits = lambda t: lax.bitcast_convert_type(t.astype(BF16).astype(F32), jnp.uint32)
    return (bits(y[:, :half]) >> 16) | (bits(y[:, half:]) & jnp.uint32(0xFFFF0000))


def _unpack_halves(w):
    lo = lax.bitcast_convert_type(w << 16, F32)
    hi = lax.bitcast_convert_type(w & jnp.uint32(0xFFFF0000), F32)
    return lo, hi


def _expert_kernel(be_ref, nb_ref, x_ref, w1_ref, b1_ref, w2_ref, b2_ref, o_ref, w1b_ref, w2b_ref):
    i = pl.program_id(0)

    @pl.when(jnp.logical_or(i == 0, be_ref[i] != be_ref[jnp.maximum(i - 1, 0)]))
    def _():
        w1b_ref[...] = w1_ref[0].astype(BF16)
        w2b_ref[...] = w2_ref[0].astype(BF16)

    @pl.when(i < nb_ref[0])
    def _():
        hc = _dot(x_ref[...], w1b_ref[...]) + b1_ref[0]
        gate = jnp.minimum(hc[:, :D_FF], SWIGLU_LIMIT)
        up = jnp.clip(hc[:, D_FF:], -SWIGLU_LIMIT, SWIGLU_LIMIT)
        glu = gate * jax.nn.sigmoid(SWIGLU_ALPHA * gate)
        y = _dot(((up + 1.0) * glu).astype(BF16), w2b_ref[...]) + b2_ref[0]
        o_ref[...] = _pack_halves(y)

    @pl.when(i >= nb_ref[0])
    def _():
        o_ref[...] = jnp.zeros_like(o_ref)


def _experts(block_e, n_blocks, xs, w1, b1, w2, b2, layer):
    p, d = xs.shape
    off = layer * N_EXPERTS
    wmap = lambda i, be, nb: (off + be[i], 0, 0)
    xmap = lambda i, be, nb: (jnp.minimum(i, nb[0] - 1), 0)
    grid_spec = pltpu.PrefetchScalarGridSpec(
        num_scalar_prefetch=2, grid=(p // MOE_BLOCK,),
        in_specs=[pl.BlockSpec((MOE_BLOCK, d), xmap),
                  pl.BlockSpec((1, d, 2 * D_FF), wmap), pl.BlockSpec((1, 1, 2 * D_FF), wmap),
                  pl.BlockSpec((1, D_FF, d), wmap), pl.BlockSpec((1, 1, d), wmap)],
        out_specs=pl.BlockSpec((MOE_BLOCK, d // 2), lambda i, be, nb: (i, 0)),
        scratch_shapes=[pltpu.VMEM((d, 2 * D_FF), BF16), pltpu.VMEM((D_FF, d), BF16)])
    return pl.pallas_call(
        _expert_kernel, grid_spec=grid_spec, out_shape=jax.ShapeDtypeStruct((p, d // 2), jnp.uint32),
        compiler_params=_cparams(("arbitrary",)), name="routed_experts",
    )(block_e, n_blocks, xs, w1, b1, w2, b2)


def _route_kernel(lg_ref, gate_ref, eid_ref, rank_ref, cnt_ref):
    i = pl.program_id(0)
    tm = lg_ref.shape[0]
    lane = lax.broadcasted_iota(jnp.int32, (tm, LANES), 1)
    x = jnp.where(lane < N_EXPERTS, lg_ref[...], -jnp.inf)
    vals, ids = [], []
    onehot = jnp.zeros((tm, LANES), F32)
    for _ in range(TOP_K):
        m = jnp.max(x, axis=1, keepdims=True)
        idx = jnp.min(jnp.where(x == m, lane, LANES), axis=1, keepdims=True)
        sel = lane == idx
        onehot = jnp.where(sel, 1.0, onehot)
        x = jnp.where(sel, -jnp.inf, x)
        vals.append(m)
        ids.append(idx)
    tile_counts = jnp.sum(onehot, axis=0, keepdims=True)

    @pl.when(i == 0)
    def _():
        cnt_ref[...] = jnp.zeros_like(cnt_ref)

    rr = lax.broadcasted_iota(jnp.int32, (tm, tm), 0)
    cc = lax.broadcasted_iota(jnp.int32, (tm, tm), 1)
    strict = jnp.where(cc < rr, 1.0, 0.0).astype(BF16)
    prior = _dot(strict, onehot.astype(BF16)) + cnt_ref[0:1, :]
    denom = jnp.ones_like(vals[0])
    for k in range(1, TOP_K):
        denom = denom + jnp.exp(vals[k] - vals[0])
    gates = jnp.zeros((tm, LANES), F32)
    eids = jnp.zeros((tm, LANES), jnp.int32)
    ranks = jnp.zeros((tm, LANES), jnp.int32)
    for k in range(TOP_K):
        rk = jnp.sum(jnp.where(lane == ids[k], prior, 0.0), axis=1, keepdims=True)
        gates = jnp.where(lane == k, jnp.exp(vals[k] - vals[0]) / denom, gates)
        eids = jnp.where(lane == k, ids[k], eids)
        ranks = jnp.where(lane == k, rk.astype(jnp.int32), ranks)
    gate_ref[...] = gates
    eid_ref[...] = eids
    rank_ref[...] = ranks
    cnt_ref[...] += tile_counts


def _route(logits):
    t = logits.shape[0]
    tile = pl.BlockSpec((ROW_TILE, LANES), lambda i: (i, 0))
    gates, eids, ranks, cnt = pl.pallas_call(
        _route_kernel, grid=(t // ROW_TILE,), in_specs=[tile],
        out_specs=[tile, tile, tile, pl.BlockSpec((SUBLANES, LANES), lambda i: (0, 0))],
        out_shape=[jax.ShapeDtypeStruct((t, LANES), F32), jax.ShapeDtypeStruct((t, LANES), jnp.int32),
                   jax.ShapeDtypeStruct((t, LANES), jnp.int32), jax.ShapeDtypeStruct((SUBLANES, LANES), F32)],
        compiler_params=_cparams(("arbitrary",)), name="router_topk",
    )(logits)
    n = t * TOP_K
    p = n + N_EXPERTS * MOE_BLOCK
    nb = p // MOE_BLOCK
    sizes = cnt[0, :N_EXPERTS].astype(jnp.int32)
    psizes = (sizes + MOE_BLOCK - 1) // MOE_BLOCK * MOE_BLOCK
    pends = jnp.cumsum(psizes)
    pstarts = pends - psizes
    starts = jnp.cumsum(sizes) - sizes
    block_e = jnp.minimum(jnp.searchsorted(pends, jnp.arange(nb, dtype=jnp.int32) * MOE_BLOCK, side='right'),
                          N_EXPERTS - 1).astype(jnp.int32)
    n_blocks = (pends[-1] // MOE_BLOCK).astype(jnp.int32).reshape(1)
    keys = eids[:, :TOP_K].reshape(n) * n + jnp.arange(n, dtype=jnp.int32)
    order = jnp.sort(keys) % n
    per_row = lambda per_block: jnp.repeat(per_block, MOE_BLOCK)
    local = jnp.arange(p, dtype=jnp.int32) - per_row(pstarts[block_e])
    size_r = per_row(sizes[block_e])
    pair = jnp.take(order, per_row(starts[block_e]) + local, mode="clip")
    src_tok = jnp.where(local < size_r, pair // TOP_K, 0)
    pos = jnp.take(pstarts, eids[:, :TOP_K], mode="clip") + ranks[:, :TOP_K]
    return gates, pos, src_tok, block_e, n_blocks


def _combine_kernel(alpha, x_ref, y_ref, gate_ref, g_ref, b_ref, o_ref, ob_ref):
    gates = gate_ref[...]
    lo = hi = None
    for k in range(TOP_K):
        lo_k, hi_k = _unpack_halves(y_ref[k])
        gk = gates[:, k:k + 1]
        lo = gk * lo_k if lo is None else lo + gk * lo_k
        hi = gk * hi_k if hi is None else hi + gk * hi_k
    moe = jnp.concatenate([lo, hi], axis=1)
    y = _layernorm_rows(alpha * x_ref[...] + moe, g_ref[...], b_ref[...])
    o_ref[...] = y
    ob_ref[...] = y.astype(BF16)


def _combine(x, yk, gates, g, b, alpha):
    t, d = x.shape
    row = pl.BlockSpec((ROW_TILE, d), lambda i: (i, 0))
    vec = pl.BlockSpec((1, d), lambda i: (0, 0))
    return pl.pallas_call(
        functools.partial(_combine_kernel, alpha), grid=(t // ROW_TILE,),
        in_specs=[row, pl.BlockSpec((TOP_K, ROW_TILE, d // 2), lambda i: (0, i, 0)),
                  pl.BlockSpec((ROW_TILE, LANES), lambda i: (i, 0)), vec, vec],
        out_specs=[row, row],
        out_shape=[jax.ShapeDtypeStruct((t, d), F32), jax.ShapeDtypeStruct((t, d), BF16)],
        compiler_params=_cparams(("parallel",)), name="expert_combine_layernorm",
    )(x, yk, gates, g.reshape(1, d), b.reshape(1, d))


def kernel(x, emb_ln_g, emb_ln_b, w_in, b_in, conv_w, ml_norm_w, ret_norm_w, w_out, ln1_g, ln1_b,
           w_router, b_router, w1, b1, w2, b2, ln2_g, ln2_b):
    batch, seq, d = x.shape
    depth = w_in.shape[0]
    assert d == D_MODEL and seq % ROW_TILE == 0
    t = batch * seq
    alpha = (2.0 * depth) ** 0.25

    w_in_p = _layout_proj(w_in).astype(BF16)
    den_cols = jnp.array([MV_OFFSET + DVP * h + DEN_COL + h for h in range(HEADS)])
    b_in_p = _layout_proj(b_in).at[:, den_cols].set(1.0)[:, None, :]
    wg_p = jnp.swapaxes(_layout_gates(w_in), -1, -2).astype(BF16)
    bg_p = _layout_gates(b_in)[:, :, None]
    hk = HEADS * DK
    conv_p = jnp.concatenate([_pad_last(conv_w[..., :hk], QKP), _pad_last(conv_w[..., hk:], QKP)], axis=-1)
    mlw_p = _pad_heads(ml_norm_w)[:, None, :]
    rtw_p = _pad_heads(ret_norm_w)[:, None, :]
    w_out_p = _layout_wout(w_out).astype(BF16)
    wr_hi, wr_lo = _split_bf16(_pad_last(w_router, LANES))
    br_p = _pad_last(b_router, LANES)[:, None, :]
    w1_r = w1.reshape(depth * N_EXPERTS, d, 2 * D_FF)
    w2_r = w2.reshape(depth * N_EXPERTS, D_FF, d)
    b1_r = b1.reshape(depth * N_EXPERTS, 1, 2 * D_FF)
    b2_r = b2.reshape(depth * N_EXPERTS, 1, d)

    stage1, stage2, chan = _dft_tables(seq)
    rc, rs1, rs2 = _rotary_tables(seq)
    ret_tiles, ret_rows = _retention_tables()

    xf, xb = _ln(x.reshape(t, d), emb_ln_g, emb_ln_b)
    for l in range(depth):
        pf, mqk, mv, mo, rqk, rv, rg, gates_t = _inproj(xb, w_in_p[l], b_in_p[l], wg_p[l], bg_p[l])
        zv, zw = _fourier_channels(pf, chan)
        yf = _fourier_sequence(stage1, stage2, zv, zw, batch, seq)
        mq, mkt, rq, rkt, gw = _mixer_prep(mqk, rqk, rc, rs1, rs2, conv_p[l], gates_t, batch, seq)
        hf, hb = _mlstm(mq, mkt, mv, gw, batch, seq)
        rf, rb = _retention(rq, rkt, rv, ret_tiles, ret_rows, batch, seq)
        x1, x1b, logits = _outproj(
            xf, yf, hf.reshape(t, HW), hb.reshape(t, HW), mo, rf.reshape(t, HW), rb.reshape(t, HW), rg,
            mlw_p[l], rtw_p[l], w_out_p[l], ln1_g[l][None, :], ln1_b[l][None, :],
            wr_hi[l], wr_lo[l], br_p[l], alpha)
        gates, pos, src_tok, block_e, n_blocks = _route(logits)
        xs = jnp.take(x1b, src_tok, axis=0, mode="clip")
        ys = _experts(block_e, n_blocks, xs, w1_r, b1_r, w2_r, b2_r, l)
        yk = jnp.take(ys, pos.T.reshape(TOP_K * t), axis=0, mode="clip").reshape(TOP_K, t, d // 2)
        xf, xb = _combine(x1, yk, gates, ln2_g[l], ln2_b[l], alpha)
    return xf.reshape(batch, seq, d)
```

```python
import functools

import jax
import jax.numpy as jnp
from jax import lax
from jax.experimental import pallas as pl
from jax.experimental.pallas import tpu as pltpu

F32 = jnp.float32
BF16 = jnp.bfloat16

D_MODEL = 1024
CHUNK = 128
FOURIER_W = D_MODEL // 4
N_FGROUPS = 4
FG_W = FOURIER_W // N_FGROUPS
ML_W = 3 * D_MODEL // 8
RET_W = D_MODEL - FOURIER_W - ML_W
HEADS = 4
DV = ML_W // HEADS
DK = DV // 2
CONV_W = 3
ROPE_BASE = 10000.0
RET_GAMMA_EXP0 = 5.0
RET_BWD_EXP_OFFSET = 0.5
N_EXPERTS = 32
TOP_K = 4
D_FF = D_MODEL
SWIGLU_ALPHA = 1.702
SWIGLU_LIMIT = 7.0
LN_EPS = 1e-5

COL_F = 0
COL_MQK = COL_F + FOURIER_W
COL_MV = COL_MQK + 2 * HEADS * DK
COL_MO = COL_MV + ML_W
COL_MG = COL_MO + ML_W
COL_RQ = COL_MG + 4 * HEADS
COL_RK = COL_RQ + HEADS * DK
COL_RV = COL_RK + HEADS * DK
COL_RG = COL_RV + RET_W
PROJ_W = COL_RG + RET_W

LANES = 128
SUBLANES = 8
DVP = LANES
QKP = 2 * LANES
HW = HEADS * DVP
DEN_COL = DV
GATE_ROWS = 4 * SUBLANES
VMEM_LIMIT = 52 * 1024 * 1024

PROJ_GROUPS = (FOURIER_W, 2 * QKP, HW, HW, 2 * QKP, HW, HW)
PROJ_DTYPES = (BF16, F32, BF16, BF16, F32, BF16, BF16)
MV_OFFSET = FOURIER_W + 2 * QKP
MIX_P = FOURIER_W + 2 * HW

FFT_N1 = 128
ROW_TILE = 512
MOE_BLOCK = 512
MIXER_BATCH = 2

def _cparams(sem):
    return pltpu.CompilerParams(dimension_semantics=sem, vmem_limit_bytes=VMEM_LIMIT)


def _pad_last(w, n):
    return jnp.pad(w, [(0, 0)] * (w.ndim - 1) + [(0, n - w.shape[-1])])


def _pad_heads(w):
    lead = w.shape[:-1]
    w = w.reshape(*lead, HEADS, DV)
    w = jnp.pad(w, [(0, 0)] * (len(lead) + 1) + [(0, DVP - DV)])
    return w.reshape(*lead, HW)


def _pair_split(w):
    lead = w.shape[:-1]
    w = w.reshape(*lead, HEADS, DK // 2, 2)
    w = jnp.swapaxes(w, -1, -2).reshape(*lead, HEADS * DK)
    return _pad_last(w, QKP)


def _layout_proj(w):
    hk = HEADS * DK
    return jnp.concatenate([
        w[..., COL_F:COL_MQK],
        _pad_last(w[..., COL_MQK:COL_MQK + hk], QKP), _pad_last(w[..., COL_MQK + hk:COL_MV], QKP),
        _pad_heads(w[..., COL_MV:COL_MO]), _pad_heads(w[..., COL_MO:COL_MG]),
        _pair_split(w[..., COL_RQ:COL_RK]), _pair_split(w[..., COL_RK:COL_RV]),
        _pad_heads(w[..., COL_RV:COL_RG]), _pad_heads(w[..., COL_RG:PROJ_W]),
    ], axis=-1)


def _layout_gates(w):
    g = w[..., COL_MG:COL_RQ]
    pick = lambda j: _pad_last(g[..., HEADS * j:HEADS * (j + 1)], SUBLANES)
    return jnp.concatenate([pick(1), pick(0), pick(3), pick(2)], axis=-1)


def _layout_wout(w):
    wt = jnp.swapaxes(w, -1, -2)
    wt = jnp.concatenate([wt[..., :FOURIER_W], _pad_heads(wt[..., FOURIER_W:FOURIER_W + ML_W]),
                          _pad_heads(wt[..., FOURIER_W + ML_W:])], axis=-1)
    return jnp.swapaxes(wt, -1, -2)


def _split_bf16(w):
    hi = w.astype(BF16)
    return hi, (w - hi.astype(F32)).astype(BF16)


def _dft_tables(seq):
    n1c, n2c, g = FFT_N1, seq // FFT_N1, SUBLANES
    eye = jnp.eye(g, dtype=F32)
    ar = lambda n: jnp.arange(n, dtype=jnp.int32)
    n1 = (ar(n1c // g)[:, None] * g + ar(g)[None, :])[:, None, None, :]
    kn = (ar(n2c)[None, :, None, None] * (n1 + n1c * ar(n2c)[None, None, :, None])) % seq
    ang = kn.astype(F32) * (2.0 * jnp.pi / seq)
    spread = lambda t: jnp.einsum('aknj,jl->akjnl', t, eye).reshape(n1c // g, n2c * g, n2c * g)
    gr, gi = spread(jnp.cos(ang) * n2c ** -0.5), spread(-jnp.sin(ang) * n2c ** -0.5)
    stage1 = jnp.concatenate([jnp.concatenate([gr, gi], axis=2),
                              jnp.concatenate([gi, -gr], axis=2)], axis=1).astype(BF16)
    phi = ((ar(n1c)[:, None] * ar(n1c)[None, :]) % n1c).astype(F32) * (2.0 * jnp.pi / n1c)
    spread2 = lambda t: jnp.einsum('kn,jl->kjln', t, eye).reshape(n1c * g, g * n1c)
    stage2 = jnp.concatenate([spread2(jnp.cos(phi) * n1c ** -0.5),
                              spread2(jnp.sin(phi) * n1c ** -0.5)], axis=1).astype(BF16)
    c = jnp.arange(FG_W, dtype=jnp.int32)
    cc = ((c[:, None] * c[None, :]) % FG_W).astype(F32) * (2.0 * jnp.pi / FG_W)
    eye = jnp.eye(N_FGROUPS, dtype=F32)
    bd_c = jnp.kron(eye, jnp.cos(cc) * FG_W ** -0.5)
    bd_s = jnp.kron(eye, jnp.sin(cc) * FG_W ** -0.5)
    chan = jnp.concatenate([bd_c, bd_s], axis=1).astype(BF16)
    return stage1, stage2, chan


def _rotary_tables(seq):
    inv = 1.0 / (ROPE_BASE ** (jnp.arange(0, DK, 2, dtype=F32) / DK))
    ang = jnp.arange(seq, dtype=F32)[:, None] * inv[None, :]
    cos, sin = jnp.cos(ang), jnp.sin(ang)
    zero = jnp.zeros_like(sin)
    heads = lambda a, b: _pad_last(jnp.tile(jnp.concatenate([a, b], axis=1), (1, HEADS)), QKP)
    return heads(cos, cos), heads(-sin, zero), heads(zero, sin)


def _retention_tables():
    idx = jnp.arange(CHUNK, dtype=F32)
    diff = idx[:, None] - idx[None, :]
    tiles, rows = [], []
    for rev in (False, True):
        offset = RET_BWD_EXP_OFFSET if rev else 0.0
        lg = jnp.log1p(-jnp.exp2(-(RET_GAMMA_EXP0 + offset) - jnp.arange(HEADS, dtype=F32)))
        lg3 = lg[:, None, None]
        if rev:
            decay = jnp.where((diff < 0)[None], jnp.exp(lg3 * jnp.maximum(-diff, 0.0)[None]), 0.0)
            w_inter = jnp.exp(lg[:, None] * (CHUNK - idx)[None, :])
            w_key = jnp.exp(lg[:, None] * idx[None, :])
        else:
            decay = jnp.where((diff >= 0)[None], jnp.exp(lg3 * jnp.maximum(diff, 0.0)[None]), 0.0)
            w_inter = jnp.exp(lg[:, None] * (idx + 1.0)[None, :])
            w_key = jnp.exp(lg[:, None] * (CHUNK - 1 - idx)[None, :])
        g_chunk = jnp.broadcast_to(jnp.exp(lg * CHUNK)[:, None], (HEADS, CHUNK))
        tiles.append(jnp.stack([decay, jnp.broadcast_to(w_inter[:, :, None], (HEADS, CHUNK, LANES))], axis=1))
        pad = lambda t: jnp.pad(t, ((0, SUBLANES - HEADS), (0, 0)))
        rows.append(jnp.stack([pad(w_key), pad(g_chunk)], axis=0))
    return jnp.stack(tiles, axis=0), jnp.stack(rows, axis=0)


def _layernorm_rows(z, g, b):
    mu = jnp.mean(z, axis=-1, keepdims=True)
    d = z - mu
    var = jnp.mean(d * d, axis=-1, keepdims=True)
    return d * lax.rsqrt(var + LN_EPS) * g + b


def _ln_kernel(x_ref, g_ref, b_ref, o_ref, ob_ref):
    y = _layernorm_rows(x_ref[...], g_ref[...], b_ref[...])
    o_ref[...] = y
    ob_ref[...] = y.astype(BF16)


def _ln(x, g, b):
    t, d = x.shape
    row = pl.BlockSpec((ROW_TILE, d), lambda i: (i, 0))
    vec = pl.BlockSpec((1, d), lambda i: (0, 0))
    return pl.pallas_call(
        _ln_kernel, grid=(t // ROW_TILE,), in_specs=[row, vec, vec], out_specs=[row, row],
        out_shape=[jax.ShapeDtypeStruct((t, d), F32), jax.ShapeDtypeStruct((t, d), BF16)],
        compiler_params=_cparams(("parallel",)), name="input_layernorm",
    )(x, g.reshape(1, d), b.reshape(1, d))


def _inproj_kernel(x_ref, w_ref, b_ref, wg_ref, bg_ref, *o_refs):
    x = x_ref[...]
    c0 = 0
    for o_ref in o_refs[:-1]:
        n = o_ref.shape[-1]
        y = jnp.dot(x, w_ref[:, c0:c0 + n], preferred_element_type=F32) + b_ref[:, c0:c0 + n]
        o_ref[...] = y.astype(o_ref.dtype)
        c0 += n
    gt = lax.dot_general(wg_ref[...], x, (((1,), (1,)), ((), ())), preferred_element_type=F32)
    o_refs[-1][...] = gt + bg_ref[...]


def _inproj(xb, w, b, wg, bg):
    t, d = xb.shape
    n_p = w.shape[1]
    const = lambda a: pl.BlockSpec(a.shape, lambda i: (0, 0))
    return pl.pallas_call(
        _inproj_kernel, grid=(t // ROW_TILE,),
        in_specs=[pl.BlockSpec((ROW_TILE, d), lambda i: (i, 0)), const(w), const(b), const(wg), const(bg)],
        out_specs=[pl.BlockSpec((ROW_TILE, n), lambda i: (i, 0)) for n in PROJ_GROUPS]
        + [pl.BlockSpec((GATE_ROWS, ROW_TILE), lambda i: (0, i))],
        out_shape=[jax.ShapeDtypeStruct((t, n), dt) for n, dt in zip(PROJ_GROUPS, PROJ_DTYPES)]
        + [jax.ShapeDtypeStruct((GATE_ROWS, t), F32)],
        compiler_params=_cparams(("parallel",)), name="input_projection",
    )(xb, w, b, wg, bg)


def _fchan_kernel(u_ref, m_ref, v_ref, w_ref):
    vw = jnp.dot(u_ref[...], m_ref[...], preferred_element_type=F32)
    v_ref[...] = vw[:, :FOURIER_W]
    w_ref[...] = vw[:, FOURIER_W:]


def _fourier_channels(u, chan):
    t = u.shape[0]
    row = pl.BlockSpec((ROW_TILE, FOURIER_W), lambda i: (i, 0))
    return pl.pallas_call(
        _fchan_kernel, grid=(t // ROW_TILE,),
        in_specs=[row, pl.BlockSpec((FOURIER_W, 2 * FOURIER_W), lambda i: (0, 0))],
        out_specs=[row, row], out_shape=[jax.ShapeDtypeStruct((t, FOURIER_W), F32)] * 2,
        compiler_params=_cparams(("parallel",)), name="fourier_channel_dft",
    )(u, chan)


def _fstage1_kernel(m_ref, v_ref, w_ref, br_ref, bi_ref):
    rows = v_ref.shape[1] * SUBLANES
    x = jnp.concatenate([v_ref[0].reshape(rows, FOURIER_W), w_ref[0].reshape(rows, FOURIER_W)], axis=0)
    y = jnp.dot(m_ref[0], x.astype(BF16), preferred_element_type=F32)
    br_ref[0] = y[:rows].reshape(br_ref.shape[1:])
    bi_ref[0] = y[rows:].reshape(bi_ref.shape[1:])


def _fstage2_kernel(m_ref, br_ref, bi_ref, o_ref):
    rows = br_ref.shape[2] * SUBLANES
    x = jnp.concatenate([br_ref[0].reshape(rows, FOURIER_W), bi_ref[0].reshape(rows, FOURIER_W)], axis=0)
    y = jnp.dot(m_ref[...], x.astype(BF16), preferred_element_type=F32)
    o_ref[0] = y.reshape(o_ref.shape[1:])


def _fourier_sequence(stage1, stage2, v, w, batch, seq):
    n1c, n2c, g = FFT_N1, seq // FFT_N1, SUBLANES
    view = lambda a: a.reshape(batch, n2c, n1c, FOURIER_W)
    blk1 = pl.BlockSpec((1, n2c, g, FOURIER_W), lambda a, b: (b, 0, a, 0))
    shape1 = jax.ShapeDtypeStruct((batch, n2c, n1c, FOURIER_W), F32)
    br, bi = pl.pallas_call(
        _fstage1_kernel, grid=(n1c // g, batch),
        in_specs=[pl.BlockSpec((1,) + stage1.shape[1:], lambda a, b: (a, 0, 0)), blk1, blk1],
        out_specs=[blk1, blk1], out_shape=[shape1, shape1],
        compiler_params=_cparams(("parallel", "parallel")), name="fourier_sequence_stage1",
    )(stage1, view(v), view(w))
    blk2 = pl.BlockSpec((1, g, n1c, FOURIER_W), lambda kb, b: (b, kb, 0, 0))
    y = pl.pallas_call(
        _fstage2_kernel, grid=(n2c // g, batch),
        in_specs=[pl.BlockSpec(stage2.shape, lambda kb, b: (0, 0)), blk2, blk2],
        out_specs=pl.BlockSpec((1, n1c, g, FOURIER_W), lambda kb, b: (b, 0, kb, 0)),
        out_shape=jax.ShapeDtypeStruct((batch, n1c, n2c, FOURIER_W), F32),
        compiler_params=_cparams(("parallel", "parallel")), name="fourier_sequence_stage2",
    )(stage2, br, bi)
    return y.reshape(batch * seq, FOURIER_W)


def _dot(a, b):
    return jnp.dot(a, b, preferred_element_type=F32)


def _cumsum_dot(tri, x, tri_left):
    hi = x.astype(BF16)
    r1 = x - hi.astype(F32)
    mid = r1.astype(BF16)
    lo = (r1 - mid.astype(F32)).astype(BF16)
    if tri_left:
        return _dot(tri, hi) + _dot(tri, mid) + _dot(tri, lo)
    return _dot(hi, tri) + _dot(mid, tri) + _dot(lo, tri)


def _log_sigmoid(x):
    return jnp.minimum(x, 0.0) - jnp.log1p(jnp.exp(-jnp.abs(x)))


def _mixer_prep_kernel(mqk_ref, prev_ref, next_ref, rqk_ref, rc_ref, rs1_ref, rs2_ref, convw_ref, g_ref,
                       mq_ref, mkt_ref, rq_ref, rkt_ref, gw_ref):
    i = pl.program_id(1)
    rows_n = mqk_ref.shape[1]
    chunks = rows_n // CHUNK
    n = CHUNK
    r2 = lax.broadcasted_iota(jnp.int32, (n, n), 0)
    c2 = lax.broadcasted_iota(jnp.int32, (n, n), 1)
    for c in range(chunks):
        lanes = slice(c * n, (c + 1) * n)
        groups = []
        for rev in (False, True):
            r0 = 2 * SUBLANES if rev else 0
            tri = jnp.where((c2 <= r2) if rev else (c2 >= r2), 1.0, 0.0).astype(BF16)
            a = _cumsum_dot(tri, _log_sigmoid(g_ref[r0:r0 + SUBLANES, lanes]), False)
            groups += [g_ref[r0 + SUBLANES:r0 + 2 * SUBLANES, lanes] - a, a]
        gw_ref[c] = jnp.concatenate(groups, axis=0)

    qk = mqk_ref[0]
    rows = lax.broadcasted_iota(jnp.int32, (rows_n, 2 * QKP), 0)
    prev_row = prev_ref[0, 0, SUBLANES - 1:SUBLANES, :] * jnp.where(i == 0, 0.0, 1.0)
    next_row = next_ref[0, 0, 0:1, :] * jnp.where(i == pl.num_programs(1) - 1, 0.0, 1.0)
    xm1 = jnp.where(rows == 0, prev_row, pltpu.roll(qk, 1, 0))
    xp1 = jnp.where(rows == rows_n - 1, next_row, pltpu.roll(qk, rows_n - 1, 0))
    conv = xm1 * convw_ref[0:1, :] + qk * convw_ref[1:2, :] + xp1 * convw_ref[2:3, :]
    act = conv * jax.nn.sigmoid(conv)
    mq_ref[0] = act[:, :QKP].astype(BF16)

    rc, rs1, rs2 = rc_ref[...], rs1_ref[...], rs2_ref[...]
    half = DK // 2

    def rotate(t):
        return t * rc + pltpu.roll(t, QKP - half, 1) * rs1 + pltpu.roll(t, half, 1) * rs2

    rqk = rqk_ref[0]
    rq_ref[0] = rotate(rqk[:, :QKP]).astype(BF16)
    mk = act[:, QKP:] * DK ** -0.5
    rk = rotate(rqk[:, QKP:]) * DK ** -0.5
    for c in range(chunks):
        mkt_ref[0, c] = mk[c * n:(c + 1) * n, :].T.astype(BF16)
        rkt_ref[0, c] = rk[c * n:(c + 1) * n, :].T.astype(BF16)


def _mixer_prep(mqk, rqk, rc, rs1, rs2, convw, gates, batch, seq):
    n = seq // CHUNK
    tiles = seq // ROW_TILE
    chunks = ROW_TILE // CHUNK
    g8 = ROW_TILE // SUBLANES
    mqk3 = mqk.reshape(batch, seq, 2 * QKP)
    mqk8 = mqk.reshape(batch, seq // SUBLANES, SUBLANES, 2 * QKP)
    rqk3 = rqk.reshape(batch, seq, 2 * QKP)
    tile = lambda w: pl.BlockSpec((1, ROW_TILE, w), lambda b, i: (b, i, 0))
    halo = lambda index: pl.BlockSpec((1, 1, SUBLANES, 2 * QKP), index)
    pos = pl.BlockSpec((ROW_TILE, QKP), lambda b, i: (i, 0))
    kt = pl.BlockSpec((1, chunks, QKP, CHUNK), lambda b, i: (b, i, 0, 0))
    q_shape = jax.ShapeDtypeStruct((batch, seq, QKP), BF16)
    kt_shape = jax.ShapeDtypeStruct((batch, n, QKP, CHUNK), BF16)
    return pl.pallas_call(
        _mixer_prep_kernel, grid=(batch, tiles),
        in_specs=[tile(2 * QKP),
                  halo(lambda b, i: (b, jnp.maximum(i * g8 - 1, 0), 0, 0)),
                  halo(lambda b, i: (b, jnp.minimum((i + 1) * g8, seq // SUBLANES - 1), 0, 0)),
                  tile(2 * QKP), pos, pos, pos,
                  pl.BlockSpec((CONV_W, 2 * QKP), lambda b, i: (0, 0)),
                  pl.BlockSpec((GATE_ROWS, ROW_TILE), lambda b, i: (0, b * tiles + i))],
        out_specs=[tile(QKP), kt, tile(QKP), kt,
                   pl.BlockSpec((chunks, GATE_ROWS, CHUNK), lambda b, i: (b * tiles + i, 0, 0))],
        out_shape=[q_shape, kt_shape, q_shape, kt_shape,
                   jax.ShapeDtypeStruct((batch * n, GATE_ROWS, CHUNK), F32)],
        compiler_params=_cparams(("parallel", "parallel")), name="mixer_qk_prep",
    )(mqk3, mqk8, mqk8, rqk3, rc, rs1, rs2, convw, gates)


def _head_key_blocks(kt):
    rows = lax.broadcasted_iota(jnp.int32, kt.shape, 0)
    zero = jnp.zeros_like(kt)
    return jnp.concatenate([jnp.where((rows >= DK * h) & (rows < DK * (h + 1)), kt, zero)
                            for h in range(HEADS)], axis=1)


def _mlstm_direction(rev, q, kt, v, gw, c_ref, cb_ref, m_ref, d):
    n = CHUNK
    r0 = 2 * SUBLANES if rev else 0
    key_w = gw[r0:r0 + SUBLANES, :]
    a = gw[r0 + SUBLANES:r0 + 2 * SUBLANES, :]
    r2 = lax.broadcasted_iota(jnp.int32, (n, n), 0)
    c2 = lax.broadcasted_iota(jnp.int32, (n, n), 1)
    causal = (c2 >= r2) if rev else (c2 <= r2)
    lane = lax.broadcasted_iota(jnp.int32, (n, LANES), 1)

    cmax = jnp.zeros((n, LANES), F32)
    for h in range(HEADS):
        cm_h = jnp.max(jnp.where(causal, key_w[h:h + 1, :], -jnp.inf), axis=1, keepdims=True)
        cmax = jnp.where(lane == DEN_COL + h, cm_h, cmax)
    zero = jnp.zeros((SUBLANES, n), F32)
    den_group = DEN_COL // SUBLANES
    a_col = jnp.concatenate([zero] * den_group + [a] + [zero] * (n // SUBLANES - den_group - 1), axis=0).T
    m_lane = m_ref[d, 1, 0:1, :]
    mm = jnp.maximum(m_lane, cmax)
    s_inter = jnp.exp(m_lane - mm)
    e_negm = jnp.exp(-a_col - mm)

    s_all = _dot(q, _head_key_blocks(kt))
    inter_all = _dot(q, cb_ref[d])
    sr = lax.broadcasted_iota(jnp.int32, (LANES, HW), 0)
    sc = lax.broadcasted_iota(jnp.int32, (LANES, HW), 1)
    spread = jnp.where(sr - DEN_COL == sc // DVP, 1.0, 0.0).astype(BF16)
    inter_all = _dot(s_inter.astype(BF16), spread) * inter_all
    rs = []
    for h in range(HEADS):
        blk = slice(DVP * h, DVP * (h + 1))
        col = slice(DEN_COL + h, DEN_COL + h + 1)
        p = jnp.exp(jnp.where(causal, key_w[h:h + 1, :] - mm[:, col], -jnp.inf))
        scores = (s_all[:, blk] * p).astype(BF16)
        rs.append(_dot(scores, v[:, blk]) + inter_all[:, blk])
    den = rs[0]
    for h in range(1, HEADS):
        den = den + rs[h]
    rden = 1.0 / jnp.maximum(jnp.abs(den), e_negm)
    out = jnp.concatenate(rs, axis=1) * _dot(rden.astype(BF16), spread)

    m_prev = m_ref[d, 0]
    a_end = jnp.broadcast_to(a[:, 0:1] if rev else a[:, n - 1:n], (SUBLANES, n))
    w_key = a_end + key_w
    m_chunk = jnp.broadcast_to(jnp.max(w_key, axis=1, keepdims=True), (SUBLANES, n))
    m_new = jnp.maximum(a_end + m_prev, m_chunk)
    s_prev = jnp.exp(a_end + m_prev - m_new)
    p_key = jnp.exp(w_key - m_chunk) * jnp.exp(m_chunk - m_new)
    m_ref[d, 0] = m_new
    row8 = lax.broadcasted_iota(jnp.int32, (SUBLANES, n), 0)
    lane8 = lax.broadcasted_iota(jnp.int32, (SUBLANES, n), 1)
    on_diag = (lane8 == row8 + DEN_COL) & (row8 < HEADS)
    m_ref[d, 1] = jnp.broadcast_to(jnp.sum(jnp.where(on_diag, m_new, 0.0), axis=0, keepdims=True),
                                   (SUBLANES, n))
    for h in range(HEADS):
        blk = slice(DVP * h, DVP * (h + 1))
        keys = slice(DK * h, DK * (h + 1))
        kp = (kt[keys, :].astype(F32) * p_key[h:h + 1, :]).astype(BF16)
        c_new = s_prev[h:h + 1, :] * c_ref[d, h] + _dot(kp, v[:, blk])
        c_ref[d, h] = c_new
        cb_ref[d, keys, blk] = c_new.astype(BF16)
    return out


def _mlstm_kernel(q_f, kt_f, v_f, g_f, q_b, kt_b, v_b, g_b, hf_ref, hb_ref, c_ref, cb_ref, m_ref):
    @pl.when(pl.program_id(1) == 0)
    def _():
        c_ref[...] = jnp.zeros_like(c_ref)
        cb_ref[...] = jnp.zeros_like(cb_ref)
        m_ref[...] = jnp.zeros_like(m_ref)

    for j in range(q_f.shape[0]):
        state = (c_ref.at[j], cb_ref.at[j], m_ref.at[j])
        hf_ref[j] = _mlstm_direction(False, q_f[j], kt_f[j, 0], v_f[j], g_f[j, 0], *state, 0).astype(hf_ref.dtype)
        hb_ref[j] = _mlstm_direction(True, q_b[j], kt_b[j, 0], v_b[j], g_b[j, 0], *state, 1).astype(hb_ref.dtype)


def _mixer_specs(batch, seq):
    n = seq // CHUNK
    bs = MIXER_BATCH if batch % MIXER_BATCH == 0 else 1
    fwd = lambda b, i: (b, i, 0)
    bwd = lambda b, i: (b, n - 1 - i, 0)
    blk = lambda w, im: pl.BlockSpec((bs, CHUNK, w), im)
    per_chunk = lambda rows, cols: (
        pl.BlockSpec((bs, 1, rows, cols), lambda b, i: (b, i, 0, 0)),
        pl.BlockSpec((bs, 1, rows, cols), lambda b, i: (b, n - 1 - i, 0, 0)))
    return n, bs, fwd, bwd, blk, per_chunk


def _mlstm(q, kt, v, gw, batch, seq):
    n, bs, fwd, bwd, blk, per_chunk = _mixer_specs(batch, seq)
    v3 = v.reshape(batch, seq, HW)
    gw4 = gw.reshape(batch, n, GATE_ROWS, CHUNK)
    ktf, ktb = per_chunk(QKP, CHUNK)
    gf, gb = per_chunk(GATE_ROWS, CHUNK)
    return pl.pallas_call(
        _mlstm_kernel, grid=(batch // bs, n),
        in_specs=[blk(QKP, fwd), ktf, blk(HW, fwd), gf, blk(QKP, bwd), ktb, blk(HW, bwd), gb],
        out_specs=[blk(HW, fwd), blk(HW, bwd)],
        out_shape=[jax.ShapeDtypeStruct((batch, seq, HW), BF16)] * 2,
        scratch_shapes=[pltpu.VMEM((bs, 2, HEADS, DK, DVP), F32), pltpu.VMEM((bs, 2, QKP, HW), BF16),
                        pltpu.VMEM((bs, 2, 2, SUBLANES, CHUNK), F32)],
        compiler_params=_cparams(("parallel", "arbitrary")), name="mlstm_mixer",
    )(q, kt, v3, gw4, q, kt, v3, gw4)


def _retention_direction(q, kt, v, tile_ref, row_ref, s_ref, sb_ref, d):
    s_all = _dot(q, _head_key_blocks(kt))
    inter_all = _dot(q, sb_ref[d])
    outs = []
    for h in range(HEADS):
        blk = slice(DVP * h, DVP * (h + 1))
        scores = (s_all[:, blk] * tile_ref[d, h, 0]).astype(BF16)
        outs.append(_dot(scores, v[:, blk]) + tile_ref[d, h, 1] * inter_all[:, blk])
    for h in range(HEADS):
        blk = slice(DVP * h, DVP * (h + 1))
        keys = slice(DK * h, DK * (h + 1))
        kp = (kt[keys, :].astype(F32) * row_ref[d, 0, h:h + 1, :]).astype(BF16)
        s_new = row_ref[d, 1, h:h + 1, :] * s_ref[d, h] + _dot(kp, v[:, blk])
        s_ref[d, h] = s_new
        sb_ref[d, keys, blk] = s_new.astype(BF16)
    return jnp.concatenate(outs, axis=1)


def _retention_kernel(q_f, kt_f, v_f, q_b, kt_b, v_b, tile_ref, row_ref, yf_ref, yb_ref, s_ref, sb_ref):
    @pl.when(pl.program_id(1) == 0)
    def _():
        s_ref[...] = jnp.zeros_like(s_ref)
        sb_ref[...] = jnp.zeros_like(sb_ref)

    for j in range(q_f.shape[0]):
        state = (s_ref.at[j], sb_ref.at[j])
        yf_ref[j] = _retention_direction(q_f[j], kt_f[j, 0], v_f[j], tile_ref, row_ref, *state, 0
                                         ).astype(yf_ref.dtype)
        yb_ref[j] = _retention_direction(q_b[j], kt_b[j, 0], v_b[j], tile_ref, row_ref, *state, 1
                                         ).astype(yb_ref.dtype)


def _retention(q, kt, v, tiles, rows, batch, seq):
    n, bs, fwd, bwd, blk, per_chunk = _mixer_specs(batch, seq)
    v3 = v.reshape(batch, seq, HW)
    ktf, ktb = per_chunk(QKP, CHUNK)
    return pl.pallas_call(
        _retention_kernel, grid=(batch // bs, n),
        in_specs=[blk(QKP, fwd), ktf, blk(HW, fwd), blk(QKP, bwd), ktb, blk(HW, bwd),
                  pl.BlockSpec(tiles.shape, lambda b, i: (0, 0, 0, 0, 0)),
                  pl.BlockSpec(rows.shape, lambda b, i: (0, 0, 0, 0))],
        out_specs=[blk(HW, fwd), blk(HW, bwd)],
        out_shape=[jax.ShapeDtypeStruct((batch, seq, HW), BF16)] * 2,
        scratch_shapes=[pltpu.VMEM((bs, 2, HEADS, DK, DVP), F32), pltpu.VMEM((bs, 2, QKP, HW), BF16)],
        compiler_params=_cparams(("parallel", "arbitrary")), name="retention_mixer",
    )(q, kt, v3, q, kt, v3, tiles, rows)


def _head_norm(z, w):
    valid = lax.broadcasted_iota(jnp.int32, (1, DVP), 1) < DV
    outs = []
    for h in range(HEADS):
        zh = z[:, DVP * h:DVP * (h + 1)]
        mu = jnp.sum(jnp.where(valid, zh, 0.0), axis=1, keepdims=True) * (1.0 / DV)
        d = jnp.where(valid, zh - mu, 0.0)
        var = jnp.sum(d * d, axis=1, keepdims=True) * (1.0 / DV)
        outs.append(d * lax.rsqrt(var + LN_EPS))
    return jnp.concatenate(outs, axis=1) * w


def _outproj_kernel(alpha, x_ref, yf_ref, hf_ref, hb_ref, o_ref, rf_ref, rb_ref, rg_ref, mlw_ref, rtw_ref,
                    w_ref, g_ref, b_ref, wrh_ref, wrl_ref, br_ref, x1_ref, x1b_ref, lg_ref):
    f32 = lambda ref: ref[...].astype(F32)
    y_m = _head_norm(jax.nn.sigmoid(f32(o_ref)) * (f32(hf_ref) + f32(hb_ref)), mlw_ref[...])
    rg = f32(rg_ref)
    y_r = rg * jax.nn.sigmoid(rg) * _head_norm(f32(rf_ref) + f32(rb_ref), rtw_ref[...])
    cat = jnp.concatenate([yf_ref[...], y_m, y_r], axis=1).astype(BF16)
    mix = _dot(cat, w_ref[...])
    x1 = _layernorm_rows(alpha * x_ref[...] + mix, g_ref[...], b_ref[...])
    x1_ref[...] = x1
    hi = x1.astype(BF16)
    x1b_ref[...] = hi
    lo = (x1 - hi.astype(F32)).astype(BF16)
    lg_ref[...] = _dot(hi, wrh_ref[...]) + _dot(lo, wrh_ref[...]) + _dot(hi, wrl_ref[...]) + br_ref[...]


def _outproj(x, yf, hf, hb, mo, rf, rb, rg, mlw, rtw, w, g, b, wrh, wrl, br, alpha):
    t, d = x.shape
    row = lambda n: pl.BlockSpec((ROW_TILE, n), lambda i: (i, 0))
    const = lambda a: pl.BlockSpec(a.shape, lambda i: (0, 0))
    args = (x, yf, hf, hb, mo, rf, rb, rg, mlw, rtw, w, g, b, wrh, wrl, br)
    in_specs = [row(d), row(FOURIER_W),
                row(HW), row(HW), row(HW), row(HW), row(HW), row(HW)] + [const(a) for a in args[8:]]
    return pl.pallas_call(
        functools.partial(_outproj_kernel, alpha), grid=(t // ROW_TILE,), in_specs=in_specs,
        out_specs=[row(d), row(d), row(LANES)],
        out_shape=[jax.ShapeDtypeStruct((t, d), F32), jax.ShapeDtypeStruct((t, d), BF16),
                   jax.ShapeDtypeStruct((t, LANES), F32)],
        compiler_params=_cparams(("parallel",)), name="mixer_output_projection",
    )(*args)


def _pack_halves(y):
    half = y.shape[1] // 2
    bits = lambda t: lax.bitcast_convert_type(t.astype(BF16).astype(F32), jnp.uint32)
    return (bits(y[:, :half]) >> 16) | (bits(y[:, half:]) & jnp.uint32(0xFFFF0000))


def _unpack_halves(w):
    lo = lax.bitcast_convert_type(w << 16, F32)
    hi = lax.bitcast_convert_type(w & jnp.uint32(0xFFFF0000), F32)
    return lo, hi


def _expert_kernel(be_ref, nb_ref, x_ref, w1_ref, b1_ref, w2_ref, b2_ref, o_ref, w1b_ref, w2b_ref):
    i = pl.program_id(0)

    @pl.when(jnp.logical_or(i == 0, be_ref[i] != be_ref[jnp.maximum(i - 1, 0)]))
    def _():
        w1b_ref[...] = w1_ref[0].astype(BF16)
        w2b_ref[...] = w2_ref[0].astype(BF16)

    @pl.when(i < nb_ref[0])
    def _():
        hc = _dot(x_ref[...], w1b_ref[...]) + b1_ref[0]
        gate = jnp.minimum(hc[:, :D_FF], SWIGLU_LIMIT)
        up = jnp.clip(hc[:, D_FF:], -SWIGLU_LIMIT, SWIGLU_LIMIT)
        glu = gate * jax.nn.sigmoid(SWIGLU_ALPHA * gate)
        y = _dot(((up + 1.0) * glu).astype(BF16), w2b_ref[...]) + b2_ref[0]
        o_ref[...] = _pack_halves(y)

    @pl.when(i >= nb_ref[0])
    def _():
        o_ref[...] = jnp.zeros_like(o_ref)


def _experts(block_e, n_blocks, xs, w1, b1, w2, b2, layer):
    p, d = xs.shape
    off = layer * N_EXPERTS
    wmap = lambda i, be, nb: (off + be[i], 0, 0)
    xmap = lambda i, be, nb: (jnp.minimum(i, nb[0] - 1), 0)
    grid_spec = pltpu.PrefetchScalarGridSpec(
        num_scalar_prefetch=2, grid=(p // MOE_BLOCK,),
        in_specs=[pl.BlockSpec((MOE_BLOCK, d), xmap),
                  pl.BlockSpec((1, d, 2 * D_FF), wmap), pl.BlockSpec((1, 1, 2 * D_FF), wmap),
                  pl.BlockSpec((1, D_FF, d), wmap), pl.BlockSpec((1, 1, d), wmap)],
        out_specs=pl.BlockSpec((MOE_BLOCK, d // 2), lambda i, be, nb: (i, 0)),
        scratch_shapes=[pltpu.VMEM((d, 2 * D_FF), BF16), pltpu.VMEM((D_FF, d), BF16)])
    return pl.pallas_call(
        _expert_kernel, grid_spec=grid_spec, out_shape=jax.ShapeDtypeStruct((p, d // 2), jnp.uint32),
        compiler_params=_cparams(("arbitrary",)), name="routed_experts",
    )(block_e, n_blocks, xs, w1, b1, w2, b2)


def _route_kernel(lg_ref, gate_ref, eid_ref, rank_ref, cnt_ref):
    i = pl.program_id(0)
    tm = lg_ref.shape[0]
    lane = lax.broadcasted_iota(jnp.int32, (tm, LANES), 1)
    x = jnp.where(lane < N_EXPERTS, lg_ref[...], -jnp.inf)
    vals, ids = [], []
    onehot = jnp.zeros((tm, LANES), F32)
    for _ in range(TOP_K):
        m = jnp.max(x, axis=1, keepdims=True)
        idx = jnp.min(jnp.where(x == m, lane, LANES), axis=1, keepdims=True)
        sel = lane == idx
        onehot = jnp.where(sel, 1.0, onehot)
        x = jnp.where(sel, -jnp.inf, x)
        vals.append(m)
        ids.append(idx)
    tile_counts = jnp.sum(onehot, axis=0, keepdims=True)

    @pl.when(i == 0)
    def _():
        cnt_ref[...] = jnp.zeros_like(cnt_ref)

    rr = lax.broadcasted_iota(jnp.int32, (tm, tm), 0)
    cc = lax.broadcasted_iota(jnp.int32, (tm, tm), 1)
    strict = jnp.where(cc < rr, 1.0, 0.0).astype(BF16)
    prior = _dot(strict, onehot.astype(BF16)) + cnt_ref[0:1, :]
    denom = jnp.ones_like(vals[0])
    for k in range(1, TOP_K):
        denom = denom + jnp.exp(vals[k] - vals[0])
    gates = jnp.zeros((tm, LANES), F32)
    eids = jnp.zeros((tm, LANES), jnp.int32)
    ranks = jnp.zeros((tm, LANES), jnp.int32)
    for k in range(TOP_K):
        rk = jnp.sum(jnp.where(lane == ids[k], prior, 0.0), axis=1, keepdims=True)
        gates = jnp.where(lane == k, jnp.exp(vals[k] - vals[0]) / denom, gates)
        eids = jnp.where(lane == k, ids[k], eids)
        ranks = jnp.where(lane == k, rk.astype(jnp.int32), ranks)
    gate_ref[...] = gates
    eid_ref[...] = eids
    rank_ref[...] = ranks
    cnt_ref[...] += tile_counts


def _route(logits):
    t = logits.shape[0]
    tile = pl.BlockSpec((ROW_TILE, LANES), lambda i: (i, 0))
    gates, eids, ranks, cnt = pl.pallas_call(
        _route_kernel, grid=(t // ROW_TILE,), in_specs=[tile],
        out_specs=[tile, tile, tile, pl.BlockSpec((SUBLANES, LANES), lambda i: (0, 0))],
        out_shape=[jax.ShapeDtypeStruct((t, LANES), F32), jax.ShapeDtypeStruct((t, LANES), jnp.int32),
                   jax.ShapeDtypeStruct((t, LANES), jnp.int32), jax.ShapeDtypeStruct((SUBLANES, LANES), F32)],
        compiler_params=_cparams(("arbitrary",)), name="router_topk",
    )(logits)
    n = t * TOP_K
    p = n + N_EXPERTS * MOE_BLOCK
    nb = p // MOE_BLOCK
    sizes = cnt[0, :N_EXPERTS].astype(jnp.int32)
    psizes = (sizes + MOE_BLOCK - 1) // MOE_BLOCK * MOE_BLOCK
    pends = jnp.cumsum(psizes)
    pstarts = pends - psizes
    starts = jnp.cumsum(sizes) - sizes
    block_e = jnp.minimum(jnp.searchsorted(pends, jnp.arange(nb, dtype=jnp.int32) * MOE_BLOCK, side='right'),
                          N_EXPERTS - 1).astype(jnp.int32)
    n_blocks = (pends[-1] // MOE_BLOCK).astype(jnp.int32).reshape(1)
    keys = eids[:, :TOP_K].reshape(n) * n + jnp.arange(n, dtype=jnp.int32)
    order = jnp.sort(keys) % n
    per_row = lambda per_block: jnp.repeat(per_block, MOE_BLOCK)
    local = jnp.arange(p, dtype=jnp.int32) - per_row(pstarts[block_e])
    size_r = per_row(sizes[block_e])
    pair = jnp.take(order, per_row(starts[block_e]) + local, mode="clip")
    src_tok = jnp.where(local < size_r, pair // TOP_K, jnp.arange(p, dtype=jnp.int32) % t)
    pos = jnp.take(pstarts, eids[:, :TOP_K], mode="clip") + ranks[:, :TOP_K]
    return gates, pos, src_tok, block_e, n_blocks


def _combine_kernel(alpha, x_ref, y_ref, gate_ref, g_ref, b_ref, o_ref, ob_ref):
    gates = gate_ref[...]
    lo = hi = None
    for k in range(TOP_K):
        lo_k, hi_k = _unpack_halves(y_ref[k])
        gk = gates[:, k:k + 1]
        lo = gk * lo_k if lo is None else lo + gk * lo_k
        hi = gk * hi_k if hi is None else hi + gk * hi_k
    moe = jnp.concatenate([lo, hi], axis=1)
    y = _layernorm_rows(alpha * x_ref[...] + moe, g_ref[...], b_ref[...])
    o_ref[...] = y
    ob_ref[...] = y.astype(BF16)


def _combine(x, yk, gates, g, b, alpha):
    t, d = x.shape
    row = pl.BlockSpec((ROW_TILE, d), lambda i: (i, 0))
    vec = pl.BlockSpec((1, d), lambda i: (0, 0))
    return pl.pallas_call(
        functools.partial(_combine_kernel, alpha), grid=(t // ROW_TILE,),
        in_specs=[row, pl.BlockSpec((TOP_K, ROW_TILE, d // 2), lambda i: (0, i, 0)),
                  pl.BlockSpec((ROW_TILE, LANES), lambda i: (i, 0)), vec, vec],
        out_specs=[row, row],
        out_shape=[jax.ShapeDtypeStruct((t, d), F32), jax.ShapeDtypeStruct((t, d), BF16)],
        compiler_params=_cparams(("parallel",)), name="expert_combine_layernorm",
    )(x, yk, gates, g.reshape(1, d), b.reshape(1, d))


def kernel(x, emb_ln_g, emb_ln_b, w_in, b_in, conv_w, ml_norm_w, ret_norm_w, w_out, ln1_g, ln1_b,
           w_router, b_router, w1, b1, w2, b2, ln2_g, ln2_b):
    batch, seq, d = x.shape
    depth = w_in.shape[0]
    assert d == D_MODEL and seq % ROW_TILE == 0
    t = batch * seq
    alpha = (2.0 * depth) ** 0.25

    w_in_p = _layout_proj(w_in).astype(BF16)
    den_cols = jnp.array([MV_OFFSET + DVP * h + DEN_COL + h for h in range(HEADS)])
    b_in_p = _layout_proj(b_in).at[:, den_cols].set(1.0)[:, None, :]
    wg_p = jnp.swapaxes(_layout_gates(w_in), -1, -2).astype(BF16)
    bg_p = _layout_gates(b_in)[:, :, None]
    hk = HEADS * DK
    conv_p = jnp.concatenate([_pad_last(conv_w[..., :hk], QKP), _pad_last(conv_w[..., hk:], QKP)], axis=-1)
    mlw_p = _pad_heads(ml_norm_w)[:, None, :]
    rtw_p = _pad_heads(ret_norm_w)[:, None, :]
    w_out_p = _layout_wout(w_out).astype(BF16)
    wr_hi, wr_lo = _split_bf16(_pad_last(w_router, LANES))
    br_p = _pad_last(b_router, LANES)[:, None, :]
    w1_r = w1.reshape(depth * N_EXPERTS, d, 2 * D_FF)
    w2_r = w2.reshape(depth * N_EXPERTS, D_FF, d)
    b1_r = b1.reshape(depth * N_EXPERTS, 1, 2 * D_FF)
    b2_r = b2.reshape(depth * N_EXPERTS, 1, d)

    stage1, stage2, chan = _dft_tables(seq)
    rc, rs1, rs2 = _rotary_tables(seq)
    ret_tiles, ret_rows = _retention_tables()

    xf, xb = _ln(x.reshape(t, d), emb_ln_g, emb_ln_b)
    for l in range(depth):
        pf, mqk, mv, mo, rqk, rv, rg, gates_t = _inproj(xb, w_in_p[l], b_in_p[l], wg_p[l], bg_p[l])
        zv, zw = _fourier_channels(pf, chan)
        yf = _fourier_sequence(stage1, stage2, zv, zw, batch, seq)
        mq, mkt, rq, rkt, gw = _mixer_prep(mqk, rqk, rc, rs1, rs2, conv_p[l], gates_t, batch, seq)
        hf, hb = _mlstm(mq, mkt, mv, gw, batch, seq)
        rf, rb = _retention(rq, rkt, rv, ret_tiles, ret_rows, batch, seq)
        x1, x1b, logits = _outproj(
            xf, yf, hf.reshape(t, HW), hb.reshape(t, HW), mo, rf.reshape(t, HW), rb.reshape(t, HW), rg,
            mlw_p[l], rtw_p[l], w_out_p[l], ln1_g[l][None, :], ln1_b[l][None, :],
            wr_hi[l], wr_lo[l], br_p[l], alpha)
        gates, pos, src_tok, block_e, n_blocks = _route(logits)
        xs = jnp.take(x1b, src_tok, axis=0, mode="clip")
        ys = _experts(block_e, n_blocks, xs, w1_r, b1_r, w2_r, b2_r, l)
        yk = jnp.take(ys, pos.T.reshape(TOP_K * t), axis=0, mode="clip").reshape(TOP_K, t, d // 2)
        xf, xb = _combine(x1, yk, gates, ln2_g[l], ln2_b[l], alpha)
    return xf.reshape(batch, seq, d)
```

```python
import functools

import jax
import jax.numpy as jnp
from jax import lax
from jax.experimental import pallas as pl
from jax.experimental.pallas import tpu as pltpu

F32 = jnp.float32
BF16 = jnp.bfloat16

D_MODEL = 1024
CHUNK = 128
FOURIER_W = D_MODEL // 4
N_FGROUPS = 4
FG_W = FOURIER_W // N_FGROUPS
ML_W = 3 * D_MODEL // 8
RET_W = D_MODEL - FOURIER_W - ML_W
HEADS = 4
DV = ML_W // HEADS
DK = DV // 2
CONV_W = 3
ROPE_BASE = 10000.0
RET_GAMMA_EXP0 = 5.0
RET_BWD_EXP_OFFSET = 0.5
N_EXPERTS = 32
TOP_K = 4
D_FF = D_MODEL
SWIGLU_ALPHA = 1.702
SWIGLU_LIMIT = 7.0
LN_EPS = 1e-5

COL_F = 0
COL_MQK = COL_F + FOURIER_W
COL_MV = COL_MQK + 2 * HEADS * DK
COL_MO = COL_MV + ML_W
COL_MG = COL_MO + ML_W
COL_RQ = COL_MG + 4 * HEADS
COL_RK = COL_RQ + HEADS * DK
COL_RV = COL_RK + HEADS * DK
COL_RG = COL_RV + RET_W
PROJ_W = COL_RG + RET_W

LANES = 128
SUBLANES = 8
DVP = LANES
QKP = 2 * LANES
HW = HEADS * DVP
DEN_COL = DV
GATE_ROWS = 4 * SUBLANES
VMEM_LIMIT = 52 * 1024 * 1024

PROJ_GROUPS = (FOURIER_W, 2 * QKP, HW, HW, 2 * QKP, HW, HW)
PROJ_DTYPES = (BF16, F32, BF16, BF16, F32, BF16, BF16)
MV_OFFSET = FOURIER_W + 2 * QKP
MIX_P = FOURIER_W + 2 * HW

FFT_N1 = 128
ROW_TILE = 512
MOE_BLOCK = 512
MIXER_BATCH = 4

def _cparams(sem):
    return pltpu.CompilerParams(dimension_semantics=sem, vmem_limit_bytes=VMEM_LIMIT)


def _pad_last(w, n):
    return jnp.pad(w, [(0, 0)] * (w.ndim - 1) + [(0, n - w.shape[-1])])


def _pad_heads(w):
    lead = w.shape[:-1]
    w = w.reshape(*lead, HEADS, DV)
    w = jnp.pad(w, [(0, 0)] * (len(lead) + 1) + [(0, DVP - DV)])
    return w.reshape(*lead, HW)


def _pair_split(w):
    lead = w.shape[:-1]
    w = w.reshape(*lead, HEADS, DK // 2, 2)
    w = jnp.swapaxes(w, -1, -2).reshape(*lead, HEADS * DK)
    return _pad_last(w, QKP)


def _layout_proj(w):
    hk = HEADS * DK
    return jnp.concatenate([
        w[..., COL_F:COL_MQK],
        _pad_last(w[..., COL_MQK:COL_MQK + hk], QKP), _pad_last(w[..., COL_MQK + hk:COL_MV], QKP),
        _pad_heads(w[..., COL_MV:COL_MO]), _pad_heads(w[..., COL_MO:COL_MG]),
        _pair_split(w[..., COL_RQ:COL_RK]), _pair_split(w[..., COL_RK:COL_RV]),
        _pad_heads(w[..., COL_RV:COL_RG]), _pad_heads(w[..., COL_RG:PROJ_W]),
    ], axis=-1)


def _layout_gates(w):
    g = w[..., COL_MG:COL_RQ]
    pick = lambda j: _pad_last(g[..., HEADS * j:HEADS * (j + 1)], SUBLANES)
    return jnp.concatenate([pick(1), pick(0), pick(3), pick(2)], axis=-1)


def _layout_wout(w):
    wt = jnp.swapaxes(w, -1, -2)
    wt = jnp.concatenate([wt[..., :FOURIER_W], _pad_heads(wt[..., FOURIER_W:FOURIER_W + ML_W]),
                          _pad_heads(wt[..., FOURIER_W + ML_W:])], axis=-1)
    return jnp.swapaxes(wt, -1, -2)


def _split_bf16(w):
    hi = w.astype(BF16)
    return hi, (w - hi.astype(F32)).astype(BF16)


def _dft_tables(seq):
    n1c, n2c, g = FFT_N1, seq // FFT_N1, SUBLANES
    eye = jnp.eye(g, dtype=F32)
    ar = lambda n: jnp.arange(n, dtype=jnp.int32)
    n1 = (ar(n1c // g)[:, None] * g + ar(g)[None, :])[:, None, None, :]
    kn = (ar(n2c)[None, :, None, None] * (n1 + n1c * ar(n2c)[None, None, :, None])) % seq
    ang = kn.astype(F32) * (2.0 * jnp.pi / seq)
    spread = lambda t: jnp.einsum('aknj,jl->akjnl', t, eye).reshape(n1c // g, n2c * g, n2c * g)
    gr, gi = spread(jnp.cos(ang) * n2c ** -0.5), spread(-jnp.sin(ang) * n2c ** -0.5)
    stage1 = jnp.concatenate([jnp.concatenate([gr, gi], axis=2),
                              jnp.concatenate([gi, -gr], axis=2)], axis=1).astype(BF16)
    phi = ((ar(n1c)[:, None] * ar(n1c)[None, :]) % n1c).astype(F32) * (2.0 * jnp.pi / n1c)
    spread2 = lambda t: jnp.einsum('kn,jl->kjln', t, eye).reshape(n1c * g, g * n1c)
    stage2 = jnp.concatenate([spread2(jnp.cos(phi) * n1c ** -0.5),
                              spread2(jnp.sin(phi) * n1c ** -0.5)], axis=1).astype(BF16)
    c = jnp.arange(FG_W, dtype=jnp.int32)
    cc = ((c[:, None] * c[None, :]) % FG_W).astype(F32) * (2.0 * jnp.pi / FG_W)
    eye = jnp.eye(N_FGROUPS, dtype=F32)
    bd_c = jnp.kron(eye, jnp.cos(cc) * FG_W ** -0.5)
    bd_s = jnp.kron(eye, jnp.sin(cc) * FG_W ** -0.5)
    chan = jnp.concatenate([bd_c, bd_s], axis=1).astype(BF16)
    return stage1, stage2, chan


def _rotary_tables(seq):
    inv = 1.0 / (ROPE_BASE ** (jnp.arange(0, DK, 2, dtype=F32) / DK))
    ang = jnp.arange(seq, dtype=F32)[:, None] * inv[None, :]
    cos, sin = jnp.cos(ang), jnp.sin(ang)
    zero = jnp.zeros_like(sin)
    heads = lambda a, b: _pad_last(jnp.tile(jnp.concatenate([a, b], axis=1), (1, HEADS)), QKP)
    return heads(cos, cos), heads(-sin, zero), heads(zero, sin)


def _retention_tables():
    idx = jnp.arange(CHUNK, dtype=F32)
    diff = idx[:, None] - idx[None, :]
    tiles, rows = [], []
    for rev in (False, True):
        offset = RET_BWD_EXP_OFFSET if rev else 0.0
        lg = jnp.log1p(-jnp.exp2(-(RET_GAMMA_EXP0 + offset) - jnp.arange(HEADS, dtype=F32)))
        lg3 = lg[:, None, None]
        if rev:
            decay = jnp.where((diff < 0)[None], jnp.exp(lg3 * jnp.maximum(-diff, 0.0)[None]), 0.0)
            w_inter = jnp.exp(lg[:, None] * (CHUNK - idx)[None, :])
            w_key = jnp.exp(lg[:, None] * idx[None, :])
        else:
            decay = jnp.where((diff >= 0)[None], jnp.exp(lg3 * jnp.maximum(diff, 0.0)[None]), 0.0)
            w_inter = jnp.exp(lg[:, None] * (idx + 1.0)[None, :])
            w_key = jnp.exp(lg[:, None] * (CHUNK - 1 - idx)[None, :])
        g_chunk = jnp.broadcast_to(jnp.exp(lg * CHUNK)[:, None], (HEADS, CHUNK))
        tiles.append(jnp.stack([decay, jnp.broadcast_to(w_inter[:, :, None], (HEADS, CHUNK, LANES))], axis=1))
        pad = lambda t: jnp.pad(t, ((0, SUBLANES - HEADS), (0, 0)))
        rows.append(jnp.stack([pad(w_key), pad(g_chunk)], axis=0))
    return jnp.stack(tiles, axis=0), jnp.stack(rows, axis=0)


def _layernorm_rows(z, g, b):
    mu = jnp.mean(z, axis=-1, keepdims=True)
    d = z - mu
    var = jnp.mean(d * d, axis=-1, keepdims=True)
    return d * lax.rsqrt(var + LN_EPS) * g + b


def _ln_kernel(x_ref, g_ref, b_ref, o_ref, ob_ref):
    y = _layernorm_rows(x_ref[...], g_ref[...], b_ref[...])
    o_ref[...] = y
    ob_ref[...] = y.astype(BF16)


def _ln(x, g, b):
    t, d = x.shape
    row = pl.BlockSpec((ROW_TILE, d), lambda i: (i, 0))
    vec = pl.BlockSpec((1, d), lambda i: (0, 0))
    return pl.pallas_call(
        _ln_kernel, grid=(t // ROW_TILE,), in_specs=[row, vec, vec], out_specs=[row, row],
        out_shape=[jax.ShapeDtypeStruct((t, d), F32), jax.ShapeDtypeStruct((t, d), BF16)],
        compiler_params=_cparams(("parallel",)), name="input_layernorm",
    )(x, g.reshape(1, d), b.reshape(1, d))


def _inproj_kernel(x_ref, w_ref, b_ref, wg_ref, bg_ref, *o_refs):
    x = x_ref[...]
    c0 = 0
    for o_ref in o_refs[:-1]:
        n = o_ref.shape[-1]
        y = jnp.dot(x, w_ref[:, c0:c0 + n], preferred_element_type=F32) + b_ref[:, c0:c0 + n]
        o_ref[...] = y.astype(o_ref.dtype)
        c0 += n
    gt = lax.dot_general(wg_ref[...], x, (((1,), (1,)), ((), ())), preferred_element_type=F32)
    o_refs[-1][...] = gt + bg_ref[...]


def _inproj(xb, w, b, wg, bg):
    t, d = xb.shape
    n_p = w.shape[1]
    const = lambda a: pl.BlockSpec(a.shape, lambda i: (0, 0))
    return pl.pallas_call(
        _inproj_kernel, grid=(t // ROW_TILE,),
        in_specs=[pl.BlockSpec((ROW_TILE, d), lambda i: (i, 0)), const(w), const(b), const(wg), const(bg)],
        out_specs=[pl.BlockSpec((ROW_TILE, n), lambda i: (i, 0)) for n in PROJ_GROUPS]
        + [pl.BlockSpec((GATE_ROWS, ROW_TILE), lambda i: (0, i))],
        out_shape=[jax.ShapeDtypeStruct((t, n), dt) for n, dt in zip(PROJ_GROUPS, PROJ_DTYPES)]
        + [jax.ShapeDtypeStruct((GATE_ROWS, t), F32)],
        compiler_params=_cparams(("parallel",)), name="input_projection",
    )(xb, w, b, wg, bg)


def _fchan_kernel(u_ref, m_ref, v_ref, w_ref):
    vw = jnp.dot(u_ref[...], m_ref[...], preferred_element_type=F32)
    v_ref[...] = vw[:, :FOURIER_W]
    w_ref[...] = vw[:, FOURIER_W:]


def _fourier_channels(u, chan):
    t = u.shape[0]
    row = pl.BlockSpec((ROW_TILE, FOURIER_W), lambda i: (i, 0))
    return pl.pallas_call(
        _fchan_kernel, grid=(t // ROW_TILE,),
        in_specs=[row, pl.BlockSpec((FOURIER_W, 2 * FOURIER_W), lambda i: (0, 0))],
        out_specs=[row, row], out_shape=[jax.ShapeDtypeStruct((t, FOURIER_W), F32)] * 2,
        compiler_params=_cparams(("parallel",)), name="fourier_channel_dft",
    )(u, chan)


def _fstage1_kernel(m_ref, v_ref, w_ref, br_ref, bi_ref):
    rows = v_ref.shape[1] * SUBLANES
    x = jnp.concatenate([v_ref[0].reshape(rows, FOURIER_W), w_ref[0].reshape(rows, FOURIER_W)], axis=0)
    y = jnp.dot(m_ref[0], x.astype(BF16), preferred_element_type=F32)
    br_ref[0] = y[:rows].reshape(br_ref.shape[1:])
    bi_ref[0] = y[rows:].reshape(bi_ref.shape[1:])


def _fstage2_kernel(m_ref, br_ref, bi_ref, o_ref):
    rows = br_ref.shape[2] * SUBLANES
    x = jnp.concatenate([br_ref[0].reshape(rows, FOURIER_W), bi_ref[0].reshape(rows, FOURIER_W)], axis=0)
    y = jnp.dot(m_ref[...], x.astype(BF16), preferred_element_type=F32)
    o_ref[0] = y.reshape(o_ref.shape[1:])


def _fourier_sequence(stage1, stage2, v, w, batch, seq):
    n1c, n2c, g = FFT_N1, seq // FFT_N1, SUBLANES
    view = lambda a: a.reshape(batch, n2c, n1c, FOURIER_W)
    blk1 = pl.BlockSpec((1, n2c, g, FOURIER_W), lambda a, b: (b, 0, a, 0))
    shape1 = jax.ShapeDtypeStruct((batch, n2c, n1c, FOURIER_W), F32)
    br, bi = pl.pallas_call(
        _fstage1_kernel, grid=(n1c // g, batch),
        in_specs=[pl.BlockSpec((1,) + stage1.shape[1:], lambda a, b: (a, 0, 0)), blk1, blk1],
        out_specs=[blk1, blk1], out_shape=[shape1, shape1],
        compiler_params=_cparams(("parallel", "parallel")), name="fourier_sequence_stage1",
    )(stage1, view(v), view(w))
    blk2 = pl.BlockSpec((1, g, n1c, FOURIER_W), lambda kb, b: (b, kb, 0, 0))
    y = pl.pallas_call(
        _fstage2_kernel, grid=(n2c // g, batch),
        in_specs=[pl.BlockSpec(stage2.shape, lambda kb, b: (0, 0)), blk2, blk2],
        out_specs=pl.BlockSpec((1, n1c, g, FOURIER_W), lambda kb, b: (b, 0, kb, 0)),
        out_shape=jax.ShapeDtypeStruct((batch, n1c, n2c, FOURIER_W), F32),
        compiler_params=_cparams(("parallel", "parallel")), name="fourier_sequence_stage2",
    )(stage2, br, bi)
    return y.reshape(batch * seq, FOURIER_W)


def _dot(a, b):
    return jnp.dot(a, b, preferred_element_type=F32)


def _cumsum_dot(tri, x, tri_left):
    hi = x.astype(BF16)
    r1 = x - hi.astype(F32)
    mid = r1.astype(BF16)
    lo = (r1 - mid.astype(F32)).astype(BF16)
    if tri_left:
        return _dot(tri, hi) + _dot(tri, mid) + _dot(tri, lo)
    return _dot(hi, tri) + _dot(mid, tri) + _dot(lo, tri)


def _log_sigmoid(x):
    return jnp.minimum(x, 0.0) - jnp.log1p(jnp.exp(-jnp.abs(x)))


def _mixer_prep_kernel(mqk_ref, prev_ref, next_ref, rqk_ref, rc_ref, rs1_ref, rs2_ref, convw_ref, g_ref,
                       mq_ref, mkt_ref, rq_ref, rkt_ref, gw_ref):
    i = pl.program_id(1)
    rows_n = mqk_ref.shape[1]
    chunks = rows_n // CHUNK
    n = CHUNK
    r2 = lax.broadcasted_iota(jnp.int32, (n, n), 0)
    c2 = lax.broadcasted_iota(jnp.int32, (n, n), 1)
    for c in range(chunks):
        lanes = slice(c * n, (c + 1) * n)
        groups = []
        for rev in (False, True):
            r0 = 2 * SUBLANES if rev else 0
            tri = jnp.where((c2 <= r2) if rev else (c2 >= r2), 1.0, 0.0).astype(BF16)
            a = _cumsum_dot(tri, _log_sigmoid(g_ref[r0:r0 + SUBLANES, lanes]), False)
            groups += [g_ref[r0 + SUBLANES:r0 + 2 * SUBLANES, lanes] - a, a]
        gw_ref[c] = jnp.concatenate(groups, axis=0)

    qk = mqk_ref[0]
    rows = lax.broadcasted_iota(jnp.int32, (rows_n, 2 * QKP), 0)
    prev_row = prev_ref[0, 0, SUBLANES - 1:SUBLANES, :] * jnp.where(i == 0, 0.0, 1.0)
    next_row = next_ref[0, 0, 0:1, :] * jnp.where(i == pl.num_programs(1) - 1, 0.0, 1.0)
    xm1 = jnp.where(rows == 0, prev_row, pltpu.roll(qk, 1, 0))
    xp1 = jnp.where(rows == rows_n - 1, next_row, pltpu.roll(qk, rows_n - 1, 0))
    conv = xm1 * convw_ref[0:1, :] + qk * convw_ref[1:2, :] + xp1 * convw_ref[2:3, :]
    act = conv * jax.nn.sigmoid(conv)
    mq_ref[0] = act[:, :QKP].astype(BF16)

    rc, rs1, rs2 = rc_ref[...], rs1_ref[...], rs2_ref[...]
    half = DK // 2

    def rotate(t):
        return t * rc + pltpu.roll(t, QKP - half, 1) * rs1 + pltpu.roll(t, half, 1) * rs2

    rqk = rqk_ref[0]
    rq_ref[0] = rotate(rqk[:, :QKP]).astype(BF16)
    mk = act[:, QKP:] * DK ** -0.5
    rk = rotate(rqk[:, QKP:]) * DK ** -0.5
    for c in range(chunks):
        mkt_ref[0, c] = mk[c * n:(c + 1) * n, :].T.astype(BF16)
        rkt_ref[0, c] = rk[c * n:(c + 1) * n, :].T.astype(BF16)


def _mixer_prep(mqk, rqk, rc, rs1, rs2, convw, gates, batch, seq):
    n = seq // CHUNK
    tiles = seq // ROW_TILE
    chunks = ROW_TILE // CHUNK
    g8 = ROW_TILE // SUBLANES
    mqk3 = mqk.reshape(batch, seq, 2 * QKP)
    mqk8 = mqk.reshape(batch, seq // SUBLANES, SUBLANES, 2 * QKP)
    rqk3 = rqk.reshape(batch, seq, 2 * QKP)
    tile = lambda w: pl.BlockSpec((1, ROW_TILE, w), lambda b, i: (b, i, 0))
    halo = lambda index: pl.BlockSpec((1, 1, SUBLANES, 2 * QKP), index)
    pos = pl.BlockSpec((ROW_TILE, QKP), lambda b, i: (i, 0))
    kt = pl.BlockSpec((1, chunks, QKP, CHUNK), lambda b, i: (b, i, 0, 0))
    q_shape = jax.ShapeDtypeStruct((batch, seq, QKP), BF16)
    kt_shape = jax.ShapeDtypeStruct((batch, n, QKP, CHUNK), BF16)
    return pl.pallas_call(
        _mixer_prep_kernel, grid=(batch, tiles),
        in_specs=[tile(2 * QKP),
                  halo(lambda b, i: (b, jnp.maximum(i * g8 - 1, 0), 0, 0)),
                  halo(lambda b, i: (b, jnp.minimum((i + 1) * g8, seq // SUBLANES - 1), 0, 0)),
                  tile(2 * QKP), pos, pos, pos,
                  pl.BlockSpec((CONV_W, 2 * QKP), lambda b, i: (0, 0)),
                  pl.BlockSpec((GATE_ROWS, ROW_TILE), lambda b, i: (0, b * tiles + i))],
        out_specs=[tile(QKP), kt, tile(QKP), kt,
                   pl.BlockSpec((chunks, GATE_ROWS, CHUNK), lambda b, i: (b * tiles + i, 0, 0))],
        out_shape=[q_shape, kt_shape, q_shape, kt_shape,
                   jax.ShapeDtypeStruct((batch * n, GATE_ROWS, CHUNK), F32)],
        compiler_params=_cparams(("parallel", "parallel")), name="mixer_qk_prep",
    )(mqk3, mqk8, mqk8, rqk3, rc, rs1, rs2, convw, gates)


def _head_key_blocks(kt):
    rows = lax.broadcasted_iota(jnp.int32, kt.shape, 0)
    zero = jnp.zeros_like(kt)
    return jnp.concatenate([jnp.where((rows >= DK * h) & (rows < DK * (h + 1)), kt, zero)
                            for h in range(HEADS)], axis=1)


def _mlstm_direction(rev, q, kt, v, gw, c_ref, cb_ref, m_ref, d):
    n = CHUNK
    r0 = 2 * SUBLANES if rev else 0
    key_w = gw[r0:r0 + SUBLANES, :]
    a = gw[r0 + SUBLANES:r0 + 2 * SUBLANES, :]
    r2 = lax.broadcasted_iota(jnp.int32, (n, n), 0)
    c2 = lax.broadcasted_iota(jnp.int32, (n, n), 1)
    causal = (c2 >= r2) if rev else (c2 <= r2)
    lane = lax.broadcasted_iota(jnp.int32, (n, LANES), 1)

    cmax = jnp.zeros((n, LANES), F32)
    for h in range(HEADS):
        cm_h = jnp.max(jnp.where(causal, key_w[h:h + 1, :], -jnp.inf), axis=1, keepdims=True)
        cmax = jnp.where(lane == DEN_COL + h, cm_h, cmax)
    zero = jnp.zeros((SUBLANES, n), F32)
    den_group = DEN_COL // SUBLANES
    a_col = jnp.concatenate([zero] * den_group + [a] + [zero] * (n // SUBLANES - den_group - 1), axis=0).T
    m_lane = m_ref[d, 1, 0:1, :]
    mm = jnp.maximum(m_lane, cmax)
    s_inter = jnp.exp(m_lane - mm)
    e_negm = jnp.exp(-a_col - mm)

    s_all = _dot(q, _head_key_blocks(kt))
    inter_all = _dot(q, cb_ref[d])
    sr = lax.broadcasted_iota(jnp.int32, (LANES, HW), 0)
    sc = lax.broadcasted_iota(jnp.int32, (LANES, HW), 1)
    spread = jnp.where(sr - DEN_COL == sc // DVP, 1.0, 0.0).astype(BF16)
    inter_all = _dot(s_inter.astype(BF16), spread) * inter_all
    rs = []
    for h in range(HEADS):
        blk = slice(DVP * h, DVP * (h + 1))
        col = slice(DEN_COL + h, DEN_COL + h + 1)
        p = jnp.exp(jnp.where(causal, key_w[h:h + 1, :] - mm[:, col], -jnp.inf))
        scores = (s_all[:, blk] * p).astype(BF16)
        rs.append(_dot(scores, v[:, blk]) + inter_all[:, blk])
    den = rs[0]
    for h in range(1, HEADS):
        den = den + rs[h]
    rden = 1.0 / jnp.maximum(jnp.abs(den), e_negm)
    out = jnp.concatenate(rs, axis=1) * _dot(rden.astype(BF16), spread)

    m_prev = m_ref[d, 0]
    a_end = jnp.broadcast_to(a[:, 0:1] if rev else a[:, n - 1:n], (SUBLANES, n))
    w_key = a_end + key_w
    m_chunk = jnp.broadcast_to(jnp.max(w_key, axis=1, keepdims=True), (SUBLANES, n))
    m_new = jnp.maximum(a_end + m_prev, m_chunk)
    s_prev = jnp.exp(a_end + m_prev - m_new)
    p_key = jnp.exp(w_key - m_chunk) * jnp.exp(m_chunk - m_new)
    m_ref[d, 0] = m_new
    row8 = lax.broadcasted_iota(jnp.int32, (SUBLANES, n), 0)
    lane8 = lax.broadcasted_iota(jnp.int32, (SUBLANES, n), 1)
    on_diag = (lane8 == row8 + DEN_COL) & (row8 < HEADS)
    m_ref[d, 1] = jnp.broadcast_to(jnp.sum(jnp.where(on_diag, m_new, 0.0), axis=0, keepdims=True),
                                   (SUBLANES, n))
    for h in range(HEADS):
        blk = slice(DVP * h, DVP * (h + 1))
        keys = slice(DK * h, DK * (h + 1))
        kp = (kt[keys, :].astype(F32) * p_key[h:h + 1, :]).astype(BF16)
        c_new = s_prev[h:h + 1, :] * c_ref[d, h] + _dot(kp, v[:, blk])
        c_ref[d, h] = c_new
        cb_ref[d, keys, blk] = c_new.astype(BF16)
    return out


def _mlstm_kernel(q_f, kt_f, v_f, g_f, q_b, kt_b, v_b, g_b, hf_ref, hb_ref, c_ref, cb_ref, m_ref):
    @pl.when(pl.program_id(1) == 0)
    def _():
        c_ref[...] = jnp.zeros_like(c_ref)
        cb_ref[...] = jnp.zeros_like(cb_ref)
        m_ref[...] = jnp.zeros_like(m_ref)

    for j in range(q_f.shape[0]):
        state = (c_ref.at[j], cb_ref.at[j], m_ref.at[j])
        hf_ref[j] = _mlstm_direction(False, q_f[j], kt_f[j, 0], v_f[j], g_f[j, 0], *state, 0).astype(hf_ref.dtype)
        hb_ref[j] = _mlstm_direction(True, q_b[j], kt_b[j, 0], v_b[j], g_b[j, 0], *state, 1).astype(hb_ref.dtype)


def _mixer_specs(batch, seq):
    n = seq // CHUNK
    bs = MIXER_BATCH if batch % MIXER_BATCH == 0 else 1
    fwd = lambda b, i: (b, i, 0)
    bwd = lambda b, i: (b, n - 1 - i, 0)
    blk = lambda w, im: pl.BlockSpec((bs, CHUNK, w), im)
    per_chunk = lambda rows, cols: (
        pl.BlockSpec((bs, 1, rows, cols), lambda b, i: (b, i, 0, 0)),
        pl.BlockSpec((bs, 1, rows, cols), lambda b, i: (b, n - 1 - i, 0, 0)))
    return n, bs, fwd, bwd, blk, per_chunk


def _mlstm(q, kt, v, gw, batch, seq):
    n, bs, fwd, bwd, blk, per_chunk = _mixer_specs(batch, seq)
    v3 = v.reshape(batch, seq, HW)
    gw4 = gw.reshape(batch, n, GATE_ROWS, CHUNK)
    ktf, ktb = per_chunk(QKP, CHUNK)
    gf, gb = per_chunk(GATE_ROWS, CHUNK)
    return pl.pallas_call(
        _mlstm_kernel, grid=(batch // bs, n),
        in_specs=[blk(QKP, fwd), ktf, blk(HW, fwd), gf, blk(QKP, bwd), ktb, blk(HW, bwd), gb],
        out_specs=[blk(HW, fwd), blk(HW, bwd)],
        out_shape=[jax.ShapeDtypeStruct((batch, seq, HW), BF16)] * 2,
        scratch_shapes=[pltpu.VMEM((bs, 2, HEADS, DK, DVP), F32), pltpu.VMEM((bs, 2, QKP, HW), BF16),
                        pltpu.VMEM((bs, 2, 2, SUBLANES, CHUNK), F32)],
        compiler_params=_cparams(("parallel", "arbitrary")), name="mlstm_mixer",
    )(q, kt, v3, gw4, q, kt, v3, gw4)


def _retention_direction(q, kt, v, tile_ref, row_ref, s_ref, sb_ref, d):
    s_all = _dot(q, _head_key_blocks(kt))
    inter_all = _dot(q, sb_ref[d])
    outs = []
    for h in range(HEADS):
        blk = slice(DVP * h, DVP * (h + 1))
        scores = (s_all[:, blk] * tile_ref[d, h, 0]).astype(BF16)
        outs.append(_dot(scores, v[:, blk]) + tile_ref[d, h, 1] * inter_all[:, blk])
    for h in range(HEADS):
        blk = slice(DVP * h, DVP * (h + 1))
        keys = slice(DK * h, DK * (h + 1))
        kp = (kt[keys, :].astype(F32) * row_ref[d, 0, h:h + 1, :]).astype(BF16)
        s_new = row_ref[d, 1, h:h + 1, :] * s_ref[d, h] + _dot(kp, v[:, blk])
        s_ref[d, h] = s_new
        sb_ref[d, keys, blk] = s_new.astype(BF16)
    return jnp.concatenate(outs, axis=1)


def _retention_kernel(q_f, kt_f, v_f, q_b, kt_b, v_b, tile_ref, row_ref, yf_ref, yb_ref, s_ref, sb_ref):
    @pl.when(pl.program_id(1) == 0)
    def _():
        s_ref[...] = jnp.zeros_like(s_ref)
        sb_ref[...] = jnp.zeros_like(sb_ref)

    for j in range(q_f.shape[0]):
        state = (s_ref.at[j], sb_ref.at[j])
        yf_ref[j] = _retention_direction(q_f[j], kt_f[j, 0], v_f[j], tile_ref, row_ref, *state, 0
                                         ).astype(yf_ref.dtype)
        yb_ref[j] = _retention_direction(q_b[j], kt_b[j, 0], v_b[j], tile_ref, row_ref, *state, 1
                                         ).astype(yb_ref.dtype)


def _retention(q, kt, v, tiles, rows, batch, seq):
    n, bs, fwd, bwd, blk, per_chunk = _mixer_specs(batch, seq)
    v3 = v.reshape(batch, seq, HW)
    ktf, ktb = per_chunk(QKP, CHUNK)
    return pl.pallas_call(
        _retention_kernel, grid=(batch // bs, n),
        in_specs=[blk(QKP, fwd), ktf, blk(HW, fwd), blk(QKP, bwd), ktb, blk(HW, bwd),
                  pl.BlockSpec(tiles.shape, lambda b, i: (0, 0, 0, 0, 0)),
                  pl.BlockSpec(rows.shape, lambda b, i: (0, 0, 0, 0))],
        out_specs=[blk(HW, fwd), blk(HW, bwd)],
        out_shape=[jax.ShapeDtypeStruct((batch, seq, HW), BF16)] * 2,
        scratch_shapes=[pltpu.VMEM((bs, 2, HEADS, DK, DVP), F32), pltpu.VMEM((bs, 2, QKP, HW), BF16)],
        compiler_params=_cparams(("parallel", "arbitrary")), name="retention_mixer",
    )(q, kt, v3, q, kt, v3, tiles, rows)


def _head_norm(z, w):
    valid = lax.broadcasted_iota(jnp.int32, (1, DVP), 1) < DV
    outs = []
    for h in range(HEADS):
        zh = z[:, DVP * h:DVP * (h + 1)]
        mu = jnp.sum(jnp.where(valid, zh, 0.0), axis=1, keepdims=True) * (1.0 / DV)
        d = jnp.where(valid, zh - mu, 0.0)
        var = jnp.sum(d * d, axis=1, keepdims=True) * (1.0 / DV)
        outs.append(d * lax.rsqrt(var + LN_EPS))
    return jnp.concatenate(outs, axis=1) * w


def _outproj_kernel(alpha, x_ref, yf_ref, hf_ref, hb_ref, o_ref, rf_ref, rb_ref, rg_ref, mlw_ref, rtw_ref,
                    w_ref, g_ref, b_ref, wrh_ref, wrl_ref, br_ref, x1_ref, x1b_ref, lg_ref):
    f32 = lambda ref: ref[...].astype(F32)
    y_m = _head_norm(jax.nn.sigmoid(f32(o_ref)) * (f32(hf_ref) + f32(hb_ref)), mlw_ref[...])
    rg = f32(rg_ref)
    y_r = rg * jax.nn.sigmoid(rg) * _head_norm(f32(rf_ref) + f32(rb_ref), rtw_ref[...])
    cat = jnp.concatenate([yf_ref[...], y_m, y_r], axis=1).astype(BF16)
    mix = _dot(cat, w_ref[...])
    x1 = _layernorm_rows(alpha * x_ref[...] + mix, g_ref[...], b_ref[...])
    x1_ref[...] = x1
    hi = x1.astype(BF16)
    x1b_ref[...] = hi
    lo = (x1 - hi.astype(F32)).astype(BF16)
    lg_ref[...] = _dot(hi, wrh_ref[...]) + _dot(lo, wrh_ref[...]) + _dot(hi, wrl_ref[...]) + br_ref[...]


def _outproj(x, yf, hf, hb, mo, rf, rb, rg, mlw, rtw, w, g, b, wrh, wrl, br, alpha):
    t, d = x.shape
    row = lambda n: pl.BlockSpec((ROW_TILE, n), lambda i: (i, 0))
    const = lambda a: pl.BlockSpec(a.shape, lambda i: (0, 0))
    args = (x, yf, hf, hb, mo, rf, rb, rg, mlw, rtw, w, g, b, wrh, wrl, br)
    in_specs = [row(d), row(FOURIER_W),
                row(HW), row(HW), row(HW), row(HW), row(HW), row(HW)] + [const(a) for a in args[8:]]
    return pl.pallas_call(
        functools.partial(_outproj_kernel, alpha), grid=(t // ROW_TILE,), in_specs=in_specs,
        out_specs=[row(d), row(d), row(LANES)],
        out_shape=[jax.ShapeDtypeStruct((t, d), F32), jax.ShapeDtypeStruct((t, d), BF16),
                   jax.ShapeDtypeStruct((t, LANES), F32)],
        compiler_params=_cparams(("parallel",)), name="mixer_output_projection",
    )(*args)


def _pack_halves(y):
    half = y.shape[1] // 2
    bits = lambda t: lax.bitcast_convert_type(t.astype(BF16).astype(F32), jnp.uint32)
    return (bits(y[:, :half]) >> 16) | (bits(y[:, half:]) & jnp.uint32(0xFFFF0000))


def _unpack_halves(w):
    lo = lax.bitcast_convert_type(w << 16, F32)
    hi = lax.bitcast_convert_type(w & jnp.uint32(0xFFFF0000), F32)
    return lo, hi


def _expert_kernel(be_ref, nb_ref, x_ref, w1_ref, b1_ref, w2_ref, b2_ref, o_ref, w1b_ref, w2b_ref):
    i = pl.program_id(0)

    @pl.when(jnp.logical_or(i == 0, be_ref[i] != be_ref[jnp.maximum(i - 1, 0)]))
    def _():
        w1b_ref[...] = w1_ref[0].astype(BF16)
        w2b_ref[...] = w2_ref[0].astype(BF16)

    @pl.when(i < nb_ref[0])
    def _():
        hc = _dot(x_ref[...], w1b_ref[...]) + b1_ref[0]
        gate = jnp.minimum(hc[:, :D_FF], SWIGLU_LIMIT)
        up = jnp.clip(hc[:, D_FF:], -SWIGLU_LIMIT, SWIGLU_LIMIT)
        glu = gate * jax.nn.sigmoid(SWIGLU_ALPHA * gate)
        y = _dot(((up + 1.0) * glu).astype(BF16), w2b_ref[...]) + b2_ref[0]
        o_ref[...] = _pack_halves(y)

    @pl.when(i >= nb_ref[0])
    def _():
        o_ref[...] = jnp.zeros_like(o_ref)


def _experts(block_e, n_blocks, xs, w1, b1, w2, b2, layer):
    p, d = xs.shape
    off = layer * N_EXPERTS
    wmap = lambda i, be, nb: (off + be[i], 0, 0)
    xmap = lambda i, be, nb: (jnp.minimum(i, nb[0] - 1), 0)
    grid_spec = pltpu.PrefetchScalarGridSpec(
        num_scalar_prefetch=2, grid=(p // MOE_BLOCK,),
        in_specs=[pl.BlockSpec((MOE_BLOCK, d), xmap),
                  pl.BlockSpec((1, d, 2 * D_FF), wmap), pl.BlockSpec((1, 1, 2 * D_FF), wmap),
                  pl.BlockSpec((1, D_FF, d), wmap), pl.BlockSpec((1, 1, d), wmap)],
        out_specs=pl.BlockSpec((MOE_BLOCK, d // 2), lambda i, be, nb: (i, 0)),
        scratch_shapes=[pltpu.VMEM((d, 2 * D_FF), BF16), pltpu.VMEM((D_FF, d), BF16)])
    return pl.pallas_call(
        _expert_kernel, grid_spec=grid_spec, out_shape=jax.ShapeDtypeStruct((p, d // 2), jnp.uint32),
        compiler_params=_cparams(("arbitrary",)), name="routed_experts",
    )(block_e, n_blocks, xs, w1, b1, w2, b2)


def _route_kernel(lg_ref, gate_ref, eid_ref, rank_ref, cnt_ref):
    i = pl.program_id(0)
    tm = lg_ref.shape[0]
    lane = lax.broadcasted_iota(jnp.int32, (tm, LANES), 1)
    x = jnp.where(lane < N_EXPERTS, lg_ref[...], -jnp.inf)
    vals, ids = [], []
    onehot = jnp.zeros((tm, LANES), F32)
    for _ in range(TOP_K):
        m = jnp.max(x, axis=1, keepdims=True)
        idx = jnp.min(jnp.where(x == m, lane, LANES), axis=1, keepdims=True)
        sel = lane == idx
        onehot = jnp.where(sel, 1.0, onehot)
        x = jnp.where(sel, -jnp.inf, x)
        vals.append(m)
        ids.append(idx)
    tile_counts = jnp.sum(onehot, axis=0, keepdims=True)

    @pl.when(i == 0)
    def _():
        cnt_ref[...] = jnp.zeros_like(cnt_ref)

    rr = lax.broadcasted_iota(jnp.int32, (tm, tm), 0)
    cc = lax.broadcasted_iota(jnp.int32, (tm, tm), 1)
    strict = jnp.where(cc < rr, 1.0, 0.0).astype(BF16)
    prior = _dot(strict, onehot.astype(BF16)) + cnt_ref[0:1, :]
    denom = jnp.ones_like(vals[0])
    for k in range(1, TOP_K):
        denom = denom + jnp.exp(vals[k] - vals[0])
    gates = jnp.zeros((tm, LANES), F32)
    eids = jnp.zeros((tm, LANES), jnp.int32)
    ranks = jnp.zeros((tm, LANES), jnp.int32)
    for k in range(TOP_K):
        rk = jnp.sum(jnp.where(lane == ids[k], prior, 0.0), axis=1, keepdims=True)
        gates = jnp.where(lane == k, jnp.exp(vals[k] - vals[0]) / denom, gates)
        eids = jnp.where(lane == k, ids[k], eids)
        ranks = jnp.where(lane == k, rk.astype(jnp.int32), ranks)
    gate_ref[...] = gates
    eid_ref[...] = eids
    rank_ref[...] = ranks
    cnt_ref[...] += tile_counts


def _route(logits):
    t = logits.shape[0]
    tile = pl.BlockSpec((ROW_TILE, LANES), lambda i: (i, 0))
    gates, eids, ranks, cnt = pl.pallas_call(
        _route_kernel, grid=(t // ROW_TILE,), in_specs=[tile],
        out_specs=[tile, tile, tile, pl.BlockSpec((SUBLANES, LANES), lambda i: (0, 0))],
        out_shape=[jax.ShapeDtypeStruct((t, LANES), F32), jax.ShapeDtypeStruct((t, LANES), jnp.int32),
                   jax.ShapeDtypeStruct((t, LANES), jnp.int32), jax.ShapeDtypeStruct((SUBLANES, LANES), F32)],
        compiler_params=_cparams(("arbitrary",)), name="router_topk",
    )(logits)
    n = t * TOP_K
    p = n + N_EXPERTS * MOE_BLOCK
    nb = p // MOE_BLOCK
    sizes = cnt[0, :N_EXPERTS].astype(jnp.int32)
    psizes = (sizes + MOE_BLOCK - 1) // MOE_BLOCK * MOE_BLOCK
    pends = jnp.cumsum(psizes)
    pstarts = pends - psizes
    starts = jnp.cumsum(sizes) - sizes
    block_e = jnp.minimum(jnp.searchsorted(pends, jnp.arange(nb, dtype=jnp.int32) * MOE_BLOCK, side='right'),
                          N_EXPERTS - 1).astype(jnp.int32)
    n_blocks = (pends[-1] // MOE_BLOCK).astype(jnp.int32).reshape(1)
    keys = eids[:, :TOP_K].reshape(n) * n + jnp.arange(n, dtype=jnp.int32)
    order = jnp.sort(keys) % n
    per_row = lambda per_block: jnp.repeat(per_block, MOE_BLOCK)
    local = jnp.arange(p, dtype=jnp.int32) - per_row(pstarts[block_e])
    size_r = per_row(sizes[block_e])
    pair = jnp.take(order, per_row(starts[block_e]) + local, mode="clip")
    src_tok = jnp.where(local < size_r, pair // TOP_K, jnp.arange(p, dtype=jnp.int32) % t)
    pos = jnp.take(pstarts, eids[:, :TOP_K], mode="clip") + ranks[:, :TOP_K]
    return gates, pos, src_tok, block_e, n_blocks


def _combine_kernel(alpha, x_ref, y_ref, gate_ref, g_ref, b_ref, o_ref, ob_ref):
    gates = gate_ref[...]
    lo = hi = None
    for k in range(TOP_K):
        lo_k, hi_k = _unpack_halves(y_ref[k])
        gk = gates[:, k:k + 1]
        lo = gk * lo_k if lo is None else lo + gk * lo_k
        hi = gk * hi_k if hi is None else hi + gk * hi_k
    moe = jnp.concatenate([lo, hi], axis=1)
    y = _layernorm_rows(alpha * x_ref[...] + moe, g_ref[...], b_ref[...])
    o_ref[...] = y
    ob_ref[...] = y.astype(BF16)


def _combine(x, yk, gates, g, b, alpha):
    t, d = x.shape
    row = pl.BlockSpec((ROW_TILE, d), lambda i: (i, 0))
    vec = pl.BlockSpec((1, d), lambda i: (0, 0))
    return pl.pallas_call(
        functools.partial(_combine_kernel, alpha), grid=(t // ROW_TILE,),
        in_specs=[row, pl.BlockSpec((TOP_K, ROW_TILE, d // 2), lambda i: (0, i, 0)),
                  pl.BlockSpec((ROW_TILE, LANES), lambda i: (i, 0)), vec, vec],
        out_specs=[row, row],
        out_shape=[jax.ShapeDtypeStruct((t, d), F32), jax.ShapeDtypeStruct((t, d), BF16)],
        compiler_params=_cparams(("parallel",)), name="expert_combine_layernorm",
    )(x, yk, gates, g.reshape(1, d), b.reshape(1, d))


def kernel(x, emb_ln_g, emb_ln_b, w_in, b_in, conv_w, ml_norm_w, ret_norm_w, w_out, ln1_g, ln1_b,
           w_router, b_router, w1, b1, w2, b2, ln2_g, ln2_b):
    batch, seq, d = x.shape
    depth = w_in.shape[0]
    assert d == D_MODEL and seq % ROW_TILE == 0
    t = batch * seq
    alpha = (2.0 * depth) ** 0.25

    w_in_p = _layout_proj(w_in).astype(BF16)
    den_cols = jnp.array([MV_OFFSET + DVP * h + DEN_COL + h for h in range(HEADS)])
    b_in_p = _layout_proj(b_in).at[:, den_cols].set(1.0)[:, None, :]
    wg_p = jnp.swapaxes(_layout_gates(w_in), -1, -2).astype(BF16)
    bg_p = _layout_gates(b_in)[:, :, None]
    hk = HEADS * DK
    conv_p = jnp.concatenate([_pad_last(conv_w[..., :hk], QKP), _pad_last(conv_w[..., hk:], QKP)], axis=-1)
    mlw_p = _pad_heads(ml_norm_w)[:, None, :]
    rtw_p = _pad_heads(ret_norm_w)[:, None, :]
    w_out_p = _layout_wout(w_out).astype(BF16)
    wr_hi, wr_lo = _split_bf16(_pad_last(w_router, LANES))
    br_p = _pad_last(b_router, LANES)[:, None, :]
    w1_r = w1.reshape(depth * N_EXPERTS, d, 2 * D_FF)
    w2_r = w2.reshape(depth * N_EXPERTS, D_FF, d)
    b1_r = b1.reshape(depth * N_EXPERTS, 1, 2 * D_FF)
    b2_r = b2.reshape(depth * N_EXPERTS, 1, d)

    stage1, stage2, chan = _dft_tables(seq)
    rc, rs1, rs2 = _rotary_tables(seq)
    ret_tiles, ret_rows = _retention_tables()

    xf, xb = _ln(x.reshape(t, d), emb_ln_g, emb_ln_b)
    for l in range(depth):
        pf, mqk, mv, mo, rqk, rv, rg, gates_t = _inproj(xb, w_in_p[l], b_in_p[l], wg_p[l], bg_p[l])
        zv, zw = _fourier_channels(pf, chan)
        yf = _fourier_sequence(stage1, stage2, zv, zw, batch, seq)
        mq, mkt, rq, rkt, gw = _mixer_prep(mqk, rqk, rc, rs1, rs2, conv_p[l], gates_t, batch, seq)
        hf, hb = _mlstm(mq, mkt, mv, gw, batch, seq)
        rf, rb = _retention(rq, rkt, rv, ret_tiles, ret_rows, batch, seq)
        x1, x1b, logits = _outproj(
            xf, yf, hf.reshape(t, HW), hb.reshape(t, HW), mo, rf.reshape(t, HW), rb.reshape(t, HW), rg,
            mlw_p[l], rtw_p[l], w_out_p[l], ln1_g[l][None, :], ln1_b[l][None, :],
            wr_hi[l], wr_lo[l], br_p[l], alpha)
        gates, pos, src_tok, block_e, n_blocks = _route(logits)
        xs = jnp.take(x1b, src_tok, axis=0, mode="clip")
        ys = _experts(block_e, n_blocks, xs, w1_r, b1_r, w2_r, b2_r, l)
        yk = jnp.take(ys, pos.T.reshape(TOP_K * t), axis=0, mode="clip").reshape(TOP_K, t, d // 2)
        xf, xb = _combine(x1, yk, gates, ln2_g[l], ln2_b[l], alpha)
    return xf.reshape(batch, seq, d)
```

```python
import functools

import jax
import jax.numpy as jnp
from jax import lax
from jax.experimental import pallas as pl
from jax.experimental.pallas import tpu as pltpu

F32 = jnp.float32
BF16 = jnp.bfloat16

D_MODEL = 1024
CHUNK = 128
FOURIER_W = D_MODEL // 4
N_FGROUPS = 4
FG_W = FOURIER_W // N_FGROUPS
ML_W = 3 * D_MODEL // 8
RET_W = D_MODEL - FOURIER_W - ML_W
HEADS = 4
DV = ML_W // HEADS
DK = DV // 2
CONV_W = 3
ROPE_BASE = 10000.0
RET_GAMMA_EXP0 = 5.0
RET_BWD_EXP_OFFSET = 0.5
N_EXPERTS = 32
TOP_K = 4
D_FF = D_MODEL
SWIGLU_ALPHA = 1.702
SWIGLU_LIMIT = 7.0
LN_EPS = 1e-5

COL_F = 0
COL_MQK = COL_F + FOURIER_W
COL_MV = COL_MQK + 2 * HEADS * DK
COL_MO = COL_MV + ML_W
COL_MG = COL_MO + ML_W
COL_RQ = COL_MG + 4 * HEADS
COL_RK = COL_RQ + HEADS * DK
COL_RV = COL_RK + HEADS * DK
COL_RG = COL_RV + RET_W
PROJ_W = COL_RG + RET_W

LANES = 128
SUBLANES = 8
DVP = LANES
QKP = 2 * LANES
HW = HEADS * DVP
DEN_COL = DV
GATE_ROWS = 4 * SUBLANES
VMEM_LIMIT = 52 * 1024 * 1024

PROJ_GROUPS = (FOURIER_W, 2 * QKP, HW, HW, 2 * QKP, HW, HW)
PROJ_DTYPES = (BF16, F32, BF16, BF16, F32, BF16, BF16)
MV_OFFSET = FOURIER_W + 2 * QKP

FFT_N1 = 128
ROW_TILE = 512
PROJ_TILE = 1024
MOE_BLOCK = 512
MIXER_BATCH = 4

def _cparams(sem):
    return pltpu.CompilerParams(dimension_semantics=sem, vmem_limit_bytes=VMEM_LIMIT)


def _pad_last(w, n):
    return jnp.pad(w, [(0, 0)] * (w.ndim - 1) + [(0, n - w.shape[-1])])


def _pad_heads(w):
    lead = w.shape[:-1]
    w = w.reshape(*lead, HEADS, DV)
    w = jnp.pad(w, [(0, 0)] * (len(lead) + 1) + [(0, DVP - DV)])
    return w.reshape(*lead, HW)


def _pair_split(w):
    lead = w.shape[:-1]
    w = w.reshape(*lead, HEADS, DK // 2, 2)
    w = jnp.swapaxes(w, -1, -2).reshape(*lead, HEADS * DK)
    return _pad_last(w, QKP)


def _layout_proj(w):
    hk = HEADS * DK
    return jnp.concatenate([
        w[..., COL_F:COL_MQK],
        _pad_last(w[..., COL_MQK:COL_MQK + hk], QKP), _pad_last(w[..., COL_MQK + hk:COL_MV], QKP),
        _pad_heads(w[..., COL_MV:COL_MO]), _pad_heads(w[..., COL_MO:COL_MG]),
        _pair_split(w[..., COL_RQ:COL_RK]), _pair_split(w[..., COL_RK:COL_RV]),
        _pad_heads(w[..., COL_RV:COL_RG]), _pad_heads(w[..., COL_RG:PROJ_W]),
    ], axis=-1)


def _layout_gates(w):
    g = w[..., COL_MG:COL_RQ]
    pick = lambda j: _pad_last(g[..., HEADS * j:HEADS * (j + 1)], SUBLANES)
    return jnp.concatenate([pick(1), pick(0), pick(3), pick(2)], axis=-1)


def _layout_wout(w):
    wt = jnp.swapaxes(w, -1, -2)
    wt = jnp.concatenate([wt[..., :FOURIER_W], _pad_heads(wt[..., FOURIER_W:FOURIER_W + ML_W]),
                          _pad_heads(wt[..., FOURIER_W + ML_W:])], axis=-1)
    return jnp.swapaxes(wt, -1, -2)


def _split_bf16(w):
    hi = w.astype(BF16)
    return hi, (w - hi.astype(F32)).astype(BF16)


def _dft_tables(seq):
    n1c, n2c, g = FFT_N1, seq // FFT_N1, SUBLANES
    eye = jnp.eye(g, dtype=F32)
    ar = lambda n: jnp.arange(n, dtype=jnp.int32)
    n1 = (ar(n1c // g)[:, None] * g + ar(g)[None, :])[:, None, None, :]
    kn = (ar(n2c)[None, :, None, None] * (n1 + n1c * ar(n2c)[None, None, :, None])) % seq
    ang = kn.astype(F32) * (2.0 * jnp.pi / seq)
    spread = lambda t: jnp.einsum('aknj,jl->akjnl', t, eye).reshape(n1c // g, n2c * g, n2c * g)
    gr, gi = spread(jnp.cos(ang) * n2c ** -0.5), spread(-jnp.sin(ang) * n2c ** -0.5)
    stage1 = jnp.concatenate([jnp.concatenate([gr, gi], axis=2),
                              jnp.concatenate([gi, -gr], axis=2)], axis=1).astype(BF16)
    phi = ((ar(n1c)[:, None] * ar(n1c)[None, :]) % n1c).astype(F32) * (2.0 * jnp.pi / n1c)
    spread2 = lambda t: jnp.einsum('kn,jl->kjln', t, eye).reshape(n1c * g, g * n1c)
    stage2 = jnp.concatenate([spread2(jnp.cos(phi) * n1c ** -0.5),
                              spread2(jnp.sin(phi) * n1c ** -0.5)], axis=1).astype(BF16)
    c = jnp.arange(FG_W, dtype=jnp.int32)
    cc = ((c[:, None] * c[None, :]) % FG_W).astype(F32) * (2.0 * jnp.pi / FG_W)
    eye = jnp.eye(N_FGROUPS, dtype=F32)
    bd_c = jnp.kron(eye, jnp.cos(cc) * FG_W ** -0.5)
    bd_s = jnp.kron(eye, jnp.sin(cc) * FG_W ** -0.5)
    chan = jnp.concatenate([bd_c, bd_s], axis=1).astype(BF16)
    return stage1, stage2, chan


def _rotary_tables(seq):
    inv = 1.0 / (ROPE_BASE ** (jnp.arange(0, DK, 2, dtype=F32) / DK))
    ang = jnp.arange(seq, dtype=F32)[:, None] * inv[None, :]
    cos, sin = jnp.cos(ang), jnp.sin(ang)
    zero = jnp.zeros_like(sin)
    heads = lambda a, b: _pad_last(jnp.tile(jnp.concatenate([a, b], axis=1), (1, HEADS)), QKP)
    return heads(cos, cos), heads(-sin, zero), heads(zero, sin)


def _retention_tables():
    idx = jnp.arange(CHUNK, dtype=F32)
    diff = idx[:, None] - idx[None, :]
    tiles, rows = [], []
    for rev in (False, True):
        offset = RET_BWD_EXP_OFFSET if rev else 0.0
        lg = jnp.log1p(-jnp.exp2(-(RET_GAMMA_EXP0 + offset) - jnp.arange(HEADS, dtype=F32)))
        lg3 = lg[:, None, None]
        if rev:
            decay = jnp.where((diff < 0)[None], jnp.exp(lg3 * jnp.maximum(-diff, 0.0)[None]), 0.0)
            w_inter = jnp.exp(lg[:, None] * (CHUNK - idx)[None, :])
            w_key = jnp.exp(lg[:, None] * idx[None, :])
        else:
            decay = jnp.where((diff >= 0)[None], jnp.exp(lg3 * jnp.maximum(diff, 0.0)[None]), 0.0)
            w_inter = jnp.exp(lg[:, None] * (idx + 1.0)[None, :])
            w_key = jnp.exp(lg[:, None] * (CHUNK - 1 - idx)[None, :])
        g_chunk = jnp.broadcast_to(jnp.exp(lg * CHUNK)[:, None], (HEADS, CHUNK))
        tiles.append(jnp.stack([decay, jnp.broadcast_to(w_inter[:, :, None], (HEADS, CHUNK, LANES))], axis=1))
        pad = lambda t: jnp.pad(t, ((0, SUBLANES - HEADS), (0, 0)))
        rows.append(jnp.stack([pad(w_key), pad(g_chunk)], axis=0))
    return jnp.stack(tiles, axis=0), jnp.stack(rows, axis=0)


def _layernorm_rows(z, g, b):
    mu = jnp.mean(z, axis=-1, keepdims=True)
    d = z - mu
    var = jnp.mean(d * d, axis=-1, keepdims=True)
    return d * lax.rsqrt(var + LN_EPS) * g + b


def _ln_kernel(x_ref, g_ref, b_ref, o_ref, ob_ref):
    y = _layernorm_rows(x_ref[...], g_ref[...], b_ref[...])
    o_ref[...] = y
    ob_ref[...] = y.astype(BF16)


def _ln(x, g, b):
    t, d = x.shape
    row = pl.BlockSpec((ROW_TILE, d), lambda i: (i, 0))
    vec = pl.BlockSpec((1, d), lambda i: (0, 0))
    return pl.pallas_call(
        _ln_kernel, grid=(t // ROW_TILE,), in_specs=[row, vec, vec], out_specs=[row, row],
        out_shape=[jax.ShapeDtypeStruct((t, d), F32), jax.ShapeDtypeStruct((t, d), BF16)],
        compiler_params=_cparams(("parallel",)), name="input_layernorm",
    )(x, g.reshape(1, d), b.reshape(1, d))


def _inproj_kernel(x_ref, w_ref, b_ref, wg_ref, bg_ref, *o_refs):
    x = x_ref[...]
    c0 = 0
    for o_ref in o_refs[:-1]:
        n = o_ref.shape[-1]
        y = jnp.dot(x, w_ref[:, c0:c0 + n], preferred_element_type=F32) + b_ref[:, c0:c0 + n]
        o_ref[...] = y.astype(o_ref.dtype)
        c0 += n
    gt = lax.dot_general(wg_ref[...], x, (((1,), (1,)), ((), ())), preferred_element_type=F32)
    o_refs[-1][...] = gt + bg_ref[...]


def _inproj(xb, w, b, wg, bg):
    t, d = xb.shape
    n_p = w.shape[1]
    const = lambda a: pl.BlockSpec(a.shape, lambda i: (0, 0))
    return pl.pallas_call(
        _inproj_kernel, grid=(t // PROJ_TILE,),
        in_specs=[pl.BlockSpec((PROJ_TILE, d), lambda i: (i, 0)), const(w), const(b), const(wg), const(bg)],
        out_specs=[pl.BlockSpec((PROJ_TILE, n), lambda i: (i, 0)) for n in PROJ_GROUPS]
        + [pl.BlockSpec((GATE_ROWS, PROJ_TILE), lambda i: (0, i))],
        out_shape=[jax.ShapeDtypeStruct((t, n), dt) for n, dt in zip(PROJ_GROUPS, PROJ_DTYPES)]
        + [jax.ShapeDtypeStruct((GATE_ROWS, t), F32)],
        compiler_params=_cparams(("parallel",)), name="input_projection",
    )(xb, w, b, wg, bg)


def _fchan_kernel(u_ref, m_ref, v_ref, w_ref):
    vw = jnp.dot(u_ref[...], m_ref[...], preferred_element_type=F32)
    v_ref[...] = vw[:, :FOURIER_W]
    w_ref[...] = vw[:, FOURIER_W:]


def _fourier_channels(u, chan):
    t = u.shape[0]
    row = pl.BlockSpec((ROW_TILE, FOURIER_W), lambda i: (i, 0))
    return pl.pallas_call(
        _fchan_kernel, grid=(t // ROW_TILE,),
        in_specs=[row, pl.BlockSpec((FOURIER_W, 2 * FOURIER_W), lambda i: (0, 0))],
        out_specs=[row, row], out_shape=[jax.ShapeDtypeStruct((t, FOURIER_W), F32)] * 2,
        compiler_params=_cparams(("parallel",)), name="fourier_channel_dft",
    )(u, chan)


def _fstage1_kernel(m_ref, v_ref, w_ref, br_ref, bi_ref):
    rows = v_ref.shape[1] * SUBLANES
    x = jnp.concatenate([v_ref[0].reshape(rows, FOURIER_W), w_ref[0].reshape(rows, FOURIER_W)], axis=0)
    y = jnp.dot(m_ref[0], x.astype(BF16), preferred_element_type=F32)
    br_ref[0] = y[:rows].reshape(br_ref.shape[1:])
    bi_ref[0] = y[rows:].reshape(bi_ref.shape[1:])


def _fstage2_kernel(m_ref, br_ref, bi_ref, o_ref):
    rows = br_ref.shape[2] * SUBLANES
    x = jnp.concatenate([br_ref[0].reshape(rows, FOURIER_W), bi_ref[0].reshape(rows, FOURIER_W)], axis=0)
    y = jnp.dot(m_ref[...], x.astype(BF16), preferred_element_type=F32)
    o_ref[0] = y.reshape(o_ref.shape[1:])


def _fourier_sequence(stage1, stage2, v, w, batch, seq):
    n1c, n2c, g = FFT_N1, seq // FFT_N1, SUBLANES
    view = lambda a: a.reshape(batch, n2c, n1c, FOURIER_W)
    blk1 = pl.BlockSpec((1, n2c, g, FOURIER_W), lambda a, b: (b, 0, a, 0))
    shape1 = jax.ShapeDtypeStruct((batch, n2c, n1c, FOURIER_W), F32)
    br, bi = pl.pallas_call(
        _fstage1_kernel, grid=(n1c // g, batch),
        in_specs=[pl.BlockSpec((1,) + stage1.shape[1:], lambda a, b: (a, 0, 0)), blk1, blk1],
        out_specs=[blk1, blk1], out_shape=[shape1, shape1],
        compiler_params=_cparams(("parallel", "parallel")), name="fourier_sequence_stage1",
    )(stage1, view(v), view(w))
    blk2 = pl.BlockSpec((1, g, n1c, FOURIER_W), lambda kb, b: (b, kb, 0, 0))
    y = pl.pallas_call(
        _fstage2_kernel, grid=(n2c // g, batch),
        in_specs=[pl.BlockSpec(stage2.shape, lambda kb, b: (0, 0)), blk2, blk2],
        out_specs=pl.BlockSpec((1, n1c, g, FOURIER_W), lambda kb, b: (b, 0, kb, 0)),
        out_shape=jax.ShapeDtypeStruct((batch, n1c, n2c, FOURIER_W), F32),
        compiler_params=_cparams(("parallel", "parallel")), name="fourier_sequence_stage2",
    )(stage2, br, bi)
    return y.reshape(batch * seq, FOURIER_W)


def _dot(a, b):
    return jnp.dot(a, b, preferred_element_type=F32)


def _cumsum_dot(tri, x, tri_left):
    hi = x.astype(BF16)
    r1 = x - hi.astype(F32)
    mid = r1.astype(BF16)
    lo = (r1 - mid.astype(F32)).astype(BF16)
    if tri_left:
        return _dot(tri, hi) + _dot(tri, mid) + _dot(tri, lo)
    return _dot(hi, tri) + _dot(mid, tri) + _dot(lo, tri)


def _log_sigmoid(x):
    return jnp.minimum(x, 0.0) - jnp.log1p(jnp.exp(-jnp.abs(x)))


def _mixer_prep_kernel(mqk_ref, prev_ref, next_ref, rqk_ref, rc_ref, rs1_ref, rs2_ref, convw_ref, g_ref,
                       mq_ref, mkt_ref, rq_ref, rkt_ref, gw_ref):
    i = pl.program_id(1)
    rows_n = mqk_ref.shape[1]
    chunks = rows_n // CHUNK
    n = CHUNK
    r2 = lax.broadcasted_iota(jnp.int32, (n, n), 0)
    c2 = lax.broadcasted_iota(jnp.int32, (n, n), 1)
    for c in range(chunks):
        lanes = slice(c * n, (c + 1) * n)
        groups = []
        for rev in (False, True):
            r0 = 2 * SUBLANES if rev else 0
            tri = jnp.where((c2 <= r2) if rev else (c2 >= r2), 1.0, 0.0).astype(BF16)
            a = _cumsum_dot(tri, _log_sigmoid(g_ref[r0:r0 + SUBLANES, lanes]), False)
            groups += [g_ref[r0 + SUBLANES:r0 + 2 * SUBLANES, lanes] - a, a]
        gw_ref[c] = jnp.concatenate(groups, axis=0)

    qk = mqk_ref[0]
    rows = lax.broadcasted_iota(jnp.int32, (rows_n, 2 * QKP), 0)
    prev_row = prev_ref[0, 0, SUBLANES - 1:SUBLANES, :] * jnp.where(i == 0, 0.0, 1.0)
    next_row = next_ref[0, 0, 0:1, :] * jnp.where(i == pl.num_programs(1) - 1, 0.0, 1.0)
    xm1 = jnp.where(rows == 0, prev_row, pltpu.roll(qk, 1, 0))
    xp1 = jnp.where(rows == rows_n - 1, next_row, pltpu.roll(qk, rows_n - 1, 0))
    conv = xm1 * convw_ref[0:1, :] + qk * convw_ref[1:2, :] + xp1 * convw_ref[2:3, :]
    act = conv * jax.nn.sigmoid(conv)
    mq_ref[0] = act[:, :QKP].astype(BF16)

    rc, rs1, rs2 = rc_ref[...], rs1_ref[...], rs2_ref[...]
    half = DK // 2

    def rotate(t):
        return t * rc + pltpu.roll(t, QKP - half, 1) * rs1 + pltpu.roll(t, half, 1) * rs2

    rqk = rqk_ref[0]
    rq_ref[0] = rotate(rqk[:, :QKP]).astype(BF16)
    mk = act[:, QKP:] * DK ** -0.5
    rk = rotate(rqk[:, QKP:]) * DK ** -0.5
    for c in range(chunks):
        mkt_ref[0, c] = mk[c * n:(c + 1) * n, :].T.astype(BF16)
        rkt_ref[0, c] = rk[c * n:(c + 1) * n, :].T.astype(BF16)


def _mixer_prep(mqk, rqk, rc, rs1, rs2, convw, gates, batch, seq):
    n = seq // CHUNK
    tiles = seq // ROW_TILE
    chunks = ROW_TILE // CHUNK
    g8 = ROW_TILE // SUBLANES
    mqk3 = mqk.reshape(batch, seq, 2 * QKP)
    mqk8 = mqk.reshape(batch, seq // SUBLANES, SUBLANES, 2 * QKP)
    rqk3 = rqk.reshape(batch, seq, 2 * QKP)
    tile = lambda w: pl.BlockSpec((1, ROW_TILE, w), lambda b, i: (b, i, 0))
    halo = lambda index: pl.BlockSpec((1, 1, SUBLANES, 2 * QKP), index)
    pos = pl.BlockSpec((ROW_TILE, QKP), lambda b, i: (i, 0))
    kt = pl.BlockSpec((1, chunks, QKP, CHUNK), lambda b, i: (b, i, 0, 0))
    q_shape = jax.ShapeDtypeStruct((batch, seq, QKP), BF16)
    kt_shape = jax.ShapeDtypeStruct((batch, n, QKP, CHUNK), BF16)
    return pl.pallas_call(
        _mixer_prep_kernel, grid=(batch, tiles),
        in_specs=[tile(2 * QKP),
                  halo(lambda b, i: (b, jnp.maximum(i * g8 - 1, 0), 0, 0)),
                  halo(lambda b, i: (b, jnp.minimum((i + 1) * g8, seq // SUBLANES - 1), 0, 0)),
                  tile(2 * QKP), pos, pos, pos,
                  pl.BlockSpec((CONV_W, 2 * QKP), lambda b, i: (0, 0)),
                  pl.BlockSpec((GATE_ROWS, ROW_TILE), lambda b, i: (0, b * tiles + i))],
        out_specs=[tile(QKP), kt, tile(QKP), kt,
                   pl.BlockSpec((chunks, GATE_ROWS, CHUNK), lambda b, i: (b * tiles + i, 0, 0))],
        out_shape=[q_shape, kt_shape, q_shape, kt_shape,
                   jax.ShapeDtypeStruct((batch * n, GATE_ROWS, CHUNK), F32)],
        compiler_params=_cparams(("parallel", "parallel")), name="mixer_qk_prep",
    )(mqk3, mqk8, mqk8, rqk3, rc, rs1, rs2, convw, gates)


def _head_key_blocks(kt):
    rows = lax.broadcasted_iota(jnp.int32, kt.shape, 0)
    zero = jnp.zeros_like(kt)
    return jnp.concatenate([jnp.where((rows >= DK * h) & (rows < DK * (h + 1)), kt, zero)
                            for h in range(HEADS)], axis=1)


def _mlstm_direction(rev, q, kt, v, gw, c_ref, cb_ref, m_ref, d):
    n = CHUNK
    r0 = 2 * SUBLANES if rev else 0
    key_w = gw[r0:r0 + SUBLANES, :]
    a = gw[r0 + SUBLANES:r0 + 2 * SUBLANES, :]
    r2 = lax.broadcasted_iota(jnp.int32, (n, n), 0)
    c2 = lax.broadcasted_iota(jnp.int32, (n, n), 1)
    causal = (c2 >= r2) if rev else (c2 <= r2)
    lane = lax.broadcasted_iota(jnp.int32, (n, LANES), 1)

    cmax = jnp.zeros((n, LANES), F32)
    for h in range(HEADS):
        cm_h = jnp.max(jnp.where(causal, key_w[h:h + 1, :], -jnp.inf), axis=1, keepdims=True)
        cmax = jnp.where(lane == DEN_COL + h, cm_h, cmax)
    zero = jnp.zeros((SUBLANES, n), F32)
    den_group = DEN_COL // SUBLANES
    a_col = jnp.concatenate([zero] * den_group + [a] + [zero] * (n // SUBLANES - den_group - 1), axis=0).T
    m_lane = m_ref[d, 1, 0:1, :]
    mm = jnp.maximum(m_lane, cmax)
    s_inter = jnp.exp(m_lane - mm)
    e_negm = jnp.exp(-a_col - mm)

    s_all = _dot(q, _head_key_blocks(kt))
    inter_all = _dot(q, cb_ref[d])
    sr = lax.broadcasted_iota(jnp.int32, (LANES, HW), 0)
    sc = lax.broadcasted_iota(jnp.int32, (LANES, HW), 1)
    spread = jnp.where(sr - DEN_COL == sc // DVP, 1.0, 0.0).astype(BF16)
    inter_all = _dot(s_inter.astype(BF16), spread) * inter_all
    rs = []
    for h in range(HEADS):
        blk = slice(DVP * h, DVP * (h + 1))
        col = slice(DEN_COL + h, DEN_COL + h + 1)
        p = jnp.exp(jnp.where(causal, key_w[h:h + 1, :] - mm[:, col], -jnp.inf))
        scores = (s_all[:, blk] * p).astype(BF16)
        rs.append(_dot(scores, v[:, blk]) + inter_all[:, blk])
    den = rs[0]
    for h in range(1, HEADS):
        den = den + rs[h]
    rden = 1.0 / jnp.maximum(jnp.abs(den), e_negm)
    out = jnp.concatenate(rs, axis=1) * _dot(rden.astype(BF16), spread)

    m_prev = m_ref[d, 0]
    a_end = jnp.broadcast_to(a[:, 0:1] if rev else a[:, n - 1:n], (SUBLANES, n))
    w_key = a_end + key_w
    m_chunk = jnp.broadcast_to(jnp.max(w_key, axis=1, keepdims=True), (SUBLANES, n))
    m_new = jnp.maximum(a_end + m_prev, m_chunk)
    s_prev = jnp.exp(a_end + m_prev - m_new)
    p_key = jnp.exp(w_key - m_chunk) * jnp.exp(m_chunk - m_new)
    m_ref[d, 0] = m_new
    row8 = lax.broadcasted_iota(jnp.int32, (SUBLANES, n), 0)
    lane8 = lax.broadcasted_iota(jnp.int32, (SUBLANES, n), 1)
    on_diag = (lane8 == row8 + DEN_COL) & (row8 < HEADS)
    m_ref[d, 1] = jnp.broadcast_to(jnp.sum(jnp.where(on_diag, m_new, 0.0), axis=0, keepdims=True),
                                   (SUBLANES, n))
    for h in range(HEADS):
        blk = slice(DVP * h, DVP * (h + 1))
        keys = slice(DK * h, DK * (h + 1))
        kp = (kt[keys, :].astype(F32) * p_key[h:h + 1, :]).astype(BF16)
        c_new = s_prev[h:h + 1, :] * c_ref[d, h] + _dot(kp, v[:, blk])
        c_ref[d, h] = c_new
        cb_ref[d, keys, blk] = c_new.astype(BF16)
    return out


def _mlstm_kernel(q_f, kt_f, v_f, g_f, q_b, kt_b, v_b, g_b, hf_ref, hb_ref, c_ref, cb_ref, m_ref):
    @pl.when(pl.program_id(1) == 0)
    def _():
        c_ref[...] = jnp.zeros_like(c_ref)
        cb_ref[...] = jnp.zeros_like(cb_ref)
        m_ref[...] = jnp.zeros_like(m_ref)

    for j in range(q_f.shape[0]):
        state = (c_ref.at[j], cb_ref.at[j], m_ref.at[j])
        hf_ref[j] = _mlstm_direction(False, q_f[j], kt_f[j, 0], v_f[j], g_f[j, 0], *state, 0).astype(hf_ref.dtype)
        hb_ref[j] = _mlstm_direction(True, q_b[j], kt_b[j, 0], v_b[j], g_b[j, 0], *state, 1).astype(hb_ref.dtype)


def _mixer_specs(batch, seq):
    n = seq // CHUNK
    bs = MIXER_BATCH if batch % MIXER_BATCH == 0 else 1
    fwd = lambda b, i: (b, i, 0)
    bwd = lambda b, i: (b, n - 1 - i, 0)
    blk = lambda w, im: pl.BlockSpec((bs, CHUNK, w), im)
    per_chunk = lambda rows, cols: (
        pl.BlockSpec((bs, 1, rows, cols), lambda b, i: (b, i, 0, 0)),
        pl.BlockSpec((bs, 1, rows, cols), lambda b, i: (b, n - 1 - i, 0, 0)))
    return n, bs, fwd, bwd, blk, per_chunk


def _mlstm(q, kt, v, gw, batch, seq):
    n, bs, fwd, bwd, blk, per_chunk = _mixer_specs(batch, seq)
    v3 = v.reshape(batch, seq, HW)
    gw4 = gw.reshape(batch, n, GATE_ROWS, CHUNK)
    ktf, ktb = per_chunk(QKP, CHUNK)
    gf, gb = per_chunk(GATE_ROWS, CHUNK)
    return pl.pallas_call(
        _mlstm_kernel, grid=(batch // bs, n),
        in_specs=[blk(QKP, fwd), ktf, blk(HW, fwd), gf, blk(QKP, bwd), ktb, blk(HW, bwd), gb],
        out_specs=[blk(HW, fwd), blk(HW, bwd)],
        out_shape=[jax.ShapeDtypeStruct((batch, seq, HW), BF16)] * 2,
        scratch_shapes=[pltpu.VMEM((bs, 2, HEADS, DK, DVP), F32), pltpu.VMEM((bs, 2, QKP, HW), BF16),
                        pltpu.VMEM((bs, 2, 2, SUBLANES, CHUNK), F32)],
        compiler_params=_cparams(("parallel", "arbitrary")), name="mlstm_mixer",
    )(q, kt, v3, gw4, q, kt, v3, gw4)


def _retention_direction(q, kt, v, tile_ref, row_ref, s_ref, sb_ref, d):
    s_all = _dot(q, _head_key_blocks(kt))
    inter_all = _dot(q, sb_ref[d])
    outs = []
    for h in range(HEADS):
        blk = slice(DVP * h, DVP * (h + 1))
        scores = (s_all[:, blk] * tile_ref[d, h, 0]).astype(BF16)
        outs.append(_dot(scores, v[:, blk]) + tile_ref[d, h, 1] * inter_all[:, blk])
    for h in range(HEADS):
        blk = slice(DVP * h, DVP * (h + 1))
        keys = slice(DK * h, DK * (h + 1))
        kp = (kt[keys, :].astype(F32) * row_ref[d, 0, h:h + 1, :]).astype(BF16)
        s_new = row_ref[d, 1, h:h + 1, :] * s_ref[d, h] + _dot(kp, v[:, blk])
        s_ref[d, h] = s_new
        sb_ref[d, keys, blk] = s_new.astype(BF16)
    return jnp.concatenate(outs, axis=1)


def _retention_kernel(q_f, kt_f, v_f, q_b, kt_b, v_b, tile_ref, row_ref, yf_ref, yb_ref, s_ref, sb_ref):
    @pl.when(pl.program_id(1) == 0)
    def _():
        s_ref[...] = jnp.zeros_like(s_ref)
        sb_ref[...] = jnp.zeros_like(sb_ref)

    for j in range(q_f.shape[0]):
        state = (s_ref.at[j], sb_ref.at[j])
        yf_ref[j] = _retention_direction(q_f[j], kt_f[j, 0], v_f[j], tile_ref, row_ref, *state, 0
                                         ).astype(yf_ref.dtype)
        yb_ref[j] = _retention_direction(q_b[j], kt_b[j, 0], v_b[j], tile_ref, row_ref, *state, 1
                                         ).astype(yb_ref.dtype)


def _retention(q, kt, v, tiles, rows, batch, seq):
    n, bs, fwd, bwd, blk, per_chunk = _mixer_specs(batch, seq)
    v3 = v.reshape(batch, seq, HW)
    ktf, ktb = per_chunk(QKP, CHUNK)
    return pl.pallas_call(
        _retention_kernel, grid=(batch // bs, n),
        in_specs=[blk(QKP, fwd), ktf, blk(HW, fwd), blk(QKP, bwd), ktb, blk(HW, bwd),
                  pl.BlockSpec(tiles.shape, lambda b, i: (0, 0, 0, 0, 0)),
                  pl.BlockSpec(rows.shape, lambda b, i: (0, 0, 0, 0))],
        out_specs=[blk(HW, fwd), blk(HW, bwd)],
        out_shape=[jax.ShapeDtypeStruct((batch, seq, HW), BF16)] * 2,
        scratch_shapes=[pltpu.VMEM((bs, 2, HEADS, DK, DVP), F32), pltpu.VMEM((bs, 2, QKP, HW), BF16)],
        compiler_params=_cparams(("parallel", "arbitrary")), name="retention_mixer",
    )(q, kt, v3, q, kt, v3, tiles, rows)


def _head_norm(z, w):
    valid = lax.broadcasted_iota(jnp.int32, (1, DVP), 1) < DV
    outs = []
    for h in range(HEADS):
        zh = z[:, DVP * h:DVP * (h + 1)]
        mu = jnp.sum(jnp.where(valid, zh, 0.0), axis=1, keepdims=True) * (1.0 / DV)
        d = jnp.where(valid, zh - mu, 0.0)
        var = jnp.sum(d * d, axis=1, keepdims=True) * (1.0 / DV)
        outs.append(d * lax.rsqrt(var + LN_EPS))
    return jnp.concatenate(outs, axis=1) * w


def _outproj_kernel(alpha, x_ref, yf_ref, hf_ref, hb_ref, o_ref, rf_ref, rb_ref, rg_ref, mlw_ref, rtw_ref,
                    w_ref, g_ref, b_ref, wrh_ref, wrl_ref, br_ref, x1_ref, x1b_ref, lg_ref):
    f32 = lambda ref: ref[...].astype(F32)
    y_m = _head_norm(jax.nn.sigmoid(f32(o_ref)) * (f32(hf_ref) + f32(hb_ref)), mlw_ref[...])
    rg = f32(rg_ref)
    y_r = rg * jax.nn.sigmoid(rg) * _head_norm(f32(rf_ref) + f32(rb_ref), rtw_ref[...])
    cat = jnp.concatenate([yf_ref[...], y_m, y_r], axis=1).astype(BF16)
    mix = _dot(cat, w_ref[...])
    x1 = _layernorm_rows(alpha * x_ref[...] + mix, g_ref[...], b_ref[...])
    x1_ref[...] = x1
    hi = x1.astype(BF16)
    x1b_ref[...] = hi
    lo = (x1 - hi.astype(F32)).astype(BF16)
    lg_ref[...] = _dot(hi, wrh_ref[...]) + _dot(lo, wrh_ref[...]) + _dot(hi, wrl_ref[...]) + br_ref[...]


def _outproj(x, yf, hf, hb, mo, rf, rb, rg, mlw, rtw, w, g, b, wrh, wrl, br, alpha):
    t, d = x.shape
    row = lambda n: pl.BlockSpec((ROW_TILE, n), lambda i: (i, 0))
    const = lambda a: pl.BlockSpec(a.shape, lambda i: (0, 0))
    args = (x, yf, hf, hb, mo, rf, rb, rg, mlw, rtw, w, g, b, wrh, wrl, br)
    in_specs = [row(d), row(FOURIER_W),
                row(HW), row(HW), row(HW), row(HW), row(HW), row(HW)] + [const(a) for a in args[8:]]
    return pl.pallas_call(
        functools.partial(_outproj_kernel, alpha), grid=(t // ROW_TILE,), in_specs=in_specs,
        out_specs=[row(d), row(d), row(LANES)],
        out_shape=[jax.ShapeDtypeStruct((t, d), F32), jax.ShapeDtypeStruct((t, d), BF16),
                   jax.ShapeDtypeStruct((t, LANES), F32)],
        compiler_params=_cparams(("parallel",)), name="mixer_output_projection",
    )(*args)


def _pack_halves(y):
    half = y.shape[1] // 2
    bits = lambda t: lax.bitcast_convert_type(t.astype(BF16).astype(F32), jnp.uint32)
    return (bits(y[:, :half]) >> 16) | (bits(y[:, half:]) & jnp.uint32(0xFFFF0000))


def _unpack_halves(w):
    lo = lax.bitcast_convert_type(w << 16, F32)
    hi = lax.bitcast_convert_type(w & jnp.uint32(0xFFFF0000), F32)
    return lo, hi


def _expert_kernel(be_ref, nb_ref, x_ref, w1_ref, b1_ref, w2_ref, b2_ref, o_ref, w1b_ref, w2b_ref):
    i = pl.program_id(0)

    @pl.when(jnp.logical_or(i == 0, be_ref[i] != be_ref[jnp.maximum(i - 1, 0)]))
    def _():
        w1b_ref[...] = w1_ref[0].astype(BF16)
        w2b_ref[...] = w2_ref[0].astype(BF16)

    @pl.when(i < nb_ref[0])
    def _():
        hc = _dot(x_ref[...], w1b_ref[...]) + b1_ref[0]
        gate = jnp.minimum(hc[:, :D_FF], SWIGLU_LIMIT)
        up = jnp.clip(hc[:, D_FF:], -SWIGLU_LIMIT, SWIGLU_LIMIT)
        glu = gate * jax.nn.sigmoid(SWIGLU_ALPHA * gate)
        y = _dot(((up + 1.0) * glu).astype(BF16), w2b_ref[...]) + b2_ref[0]
        o_ref[...] = _pack_halves(y)

    @pl.when(i >= nb_ref[0])
    def _():
        o_ref[...] = jnp.zeros_like(o_ref)


def _experts(block_e, n_blocks, xs, w1, b1, w2, b2, layer):
    p, d = xs.shape
    off = layer * N_EXPERTS
    wmap = lambda i, be, nb: (off + be[i], 0, 0)
    xmap = lambda i, be, nb: (jnp.minimum(i, nb[0] - 1), 0)
    grid_spec = pltpu.PrefetchScalarGridSpec(
        num_scalar_prefetch=2, grid=(p // MOE_BLOCK,),
        in_specs=[pl.BlockSpec((MOE_BLOCK, d), xmap),
                  pl.BlockSpec((1, d, 2 * D_FF), wmap), pl.BlockSpec((1, 1, 2 * D_FF), wmap),
                  pl.BlockSpec((1, D_FF, d), wmap), pl.BlockSpec((1, 1, d), wmap)],
        out_specs=pl.BlockSpec((MOE_BLOCK, d // 2), lambda i, be, nb: (i, 0)),
        scratch_shapes=[pltpu.VMEM((d, 2 * D_FF), BF16), pltpu.VMEM((D_FF, d), BF16)])
    return pl.pallas_call(
        _expert_kernel, grid_spec=grid_spec, out_shape=jax.ShapeDtypeStruct((p, d // 2), jnp.uint32),
        compiler_params=_cparams(("arbitrary",)), name="routed_experts",
    )(block_e, n_blocks, xs, w1, b1, w2, b2)


def _route_kernel(lg_ref, gate_ref, eid_ref, rank_ref, cnt_ref):
    i = pl.program_id(0)
    tm = lg_ref.shape[0]
    lane = lax.broadcasted_iota(jnp.int32, (tm, LANES), 1)
    x = jnp.where(lane < N_EXPERTS, lg_ref[...], -jnp.inf)
    vals, ids = [], []
    onehot = jnp.zeros((tm, LANES), F32)
    for _ in range(TOP_K):
        m = jnp.max(x, axis=1, keepdims=True)
        idx = jnp.min(jnp.where(x == m, lane, LANES), axis=1, keepdims=True)
        sel = lane == idx
        onehot = jnp.where(sel, 1.0, onehot)
        x = jnp.where(sel, -jnp.inf, x)
        vals.append(m)
        ids.append(idx)
    tile_counts = jnp.sum(onehot, axis=0, keepdims=True)

    @pl.when(i == 0)
    def _():
        cnt_ref[...] = jnp.zeros_like(cnt_ref)

    rr = lax.broadcasted_iota(jnp.int32, (tm, tm), 0)
    cc = lax.broadcasted_iota(jnp.int32, (tm, tm), 1)
    strict = jnp.where(cc < rr, 1.0, 0.0).astype(BF16)
    prior = _dot(strict, onehot.astype(BF16)) + cnt_ref[0:1, :]
    denom = jnp.ones_like(vals[0])
    for k in range(1, TOP_K):
        denom = denom + jnp.exp(vals[k] - vals[0])
    gates = jnp.zeros((tm, LANES), F32)
    eids = jnp.zeros((tm, LANES), jnp.int32)
    ranks = jnp.zeros((tm, LANES), jnp.int32)
    for k in range(TOP_K):
        rk = jnp.sum(jnp.where(lane == ids[k], prior, 0.0), axis=1, keepdims=True)
        gates = jnp.where(lane == k, jnp.exp(vals[k] - vals[0]) / denom, gates)
        eids = jnp.where(lane == k, ids[k], eids)
        ranks = jnp.where(lane == k, rk.astype(jnp.int32), ranks)
    gate_ref[...] = gates
    eid_ref[...] = eids
    rank_ref[...] = ranks
    cnt_ref[...] += tile_counts


def _route(logits):
    t = logits.shape[0]
    tile = pl.BlockSpec((ROW_TILE, LANES), lambda i: (i, 0))
    gates, eids, ranks, cnt = pl.pallas_call(
        _route_kernel, grid=(t // ROW_TILE,), in_specs=[tile],
        out_specs=[tile, tile, tile, pl.BlockSpec((SUBLANES, LANES), lambda i: (0, 0))],
        out_shape=[jax.ShapeDtypeStruct((t, LANES), F32), jax.ShapeDtypeStruct((t, LANES), jnp.int32),
                   jax.ShapeDtypeStruct((t, LANES), jnp.int32), jax.ShapeDtypeStruct((SUBLANES, LANES), F32)],
        compiler_params=_cparams(("arbitrary",)), name="router_topk",
    )(logits)
    n = t * TOP_K
    p = n + N_EXPERTS * MOE_BLOCK
    nb = p // MOE_BLOCK
    sizes = cnt[0, :N_EXPERTS].astype(jnp.int32)
    psizes = (sizes + MOE_BLOCK - 1) // MOE_BLOCK * MOE_BLOCK
    pends = jnp.cumsum(psizes)
    pstarts = pends - psizes
    starts = jnp.cumsum(sizes) - sizes
    block_e = jnp.minimum(jnp.searchsorted(pends, jnp.arange(nb, dtype=jnp.int32) * MOE_BLOCK, side='right'),
                          N_EXPERTS - 1).astype(jnp.int32)
    n_blocks = (pends[-1] // MOE_BLOCK).astype(jnp.int32).reshape(1)
    keys = eids[:, :TOP_K].reshape(n) * n + jnp.arange(n, dtype=jnp.int32)
    order = jnp.sort(keys) % n
    per_row = lambda per_block: jnp.repeat(per_block, MOE_BLOCK)
    local = jnp.arange(p, dtype=jnp.int32) - per_row(pstarts[block_e])
    size_r = per_row(sizes[block_e])
    pair = jnp.take(order, per_row(starts[block_e]) + local, mode="clip")
    src_tok = jnp.where(local < size_r, pair // TOP_K, jnp.arange(p, dtype=jnp.int32) % t)
    pos = jnp.take(pstarts, eids[:, :TOP_K], mode="clip") + ranks[:, :TOP_K]
    return gates, pos, src_tok, block_e, n_blocks


def _combine_kernel(alpha, x_ref, y_ref, gate_ref, g_ref, b_ref, o_ref, ob_ref):
    gates = gate_ref[...]
    lo = hi = None
    for k in range(TOP_K):
        lo_k, hi_k = _unpack_halves(y_ref[k])
        gk = gates[:, k:k + 1]
        lo = gk * lo_k if lo is None else lo + gk * lo_k
        hi = gk * hi_k if hi is None else hi + gk * hi_k
    moe = jnp.concatenate([lo, hi], axis=1)
    y = _layernorm_rows(alpha * x_ref[...] + moe, g_ref[...], b_ref[...])
    o_ref[...] = y
    ob_ref[...] = y.astype(BF16)


def _combine(x, yk, gates, g, b, alpha):
    t, d = x.shape
    row = pl.BlockSpec((ROW_TILE, d), lambda i: (i, 0))
    vec = pl.BlockSpec((1, d), lambda i: (0, 0))
    return pl.pallas_call(
        functools.partial(_combine_kernel, alpha), grid=(t // ROW_TILE,),
        in_specs=[row, pl.BlockSpec((TOP_K, ROW_TILE, d // 2), lambda i: (0, i, 0)),
                  pl.BlockSpec((ROW_TILE, LANES), lambda i: (i, 0)), vec, vec],
        out_specs=[row, row],
        out_shape=[jax.ShapeDtypeStruct((t, d), F32), jax.ShapeDtypeStruct((t, d), BF16)],
        compiler_params=_cparams(("parallel",)), name="expert_combine_layernorm",
    )(x, yk, gates, g.reshape(1, d), b.reshape(1, d))


def kernel(x, emb_ln_g, emb_ln_b, w_in, b_in, conv_w, ml_norm_w, ret_norm_w, w_out, ln1_g, ln1_b,
           w_router, b_router, w1, b1, w2, b2, ln2_g, ln2_b):
    batch, seq, d = x.shape
    depth = w_in.shape[0]
    assert d == D_MODEL and seq % PROJ_TILE == 0 and PROJ_TILE % ROW_TILE == 0
    assert (seq // FFT_N1) % SUBLANES == 0
    t = batch * seq
    alpha = (2.0 * depth) ** 0.25

    w_in_p = _layout_proj(w_in).astype(BF16)
    den_cols = jnp.array([MV_OFFSET + DVP * h + DEN_COL + h for h in range(HEADS)])
    b_in_p = _layout_proj(b_in).at[:, den_cols].set(1.0)[:, None, :]
    wg_p = jnp.swapaxes(_layout_gates(w_in), -1, -2).astype(BF16)
    bg_p = _layout_gates(b_in)[:, :, None]
    hk = HEADS * DK
    conv_p = jnp.concatenate([_pad_last(conv_w[..., :hk], QKP), _pad_last(conv_w[..., hk:], QKP)], axis=-1)
    mlw_p = _pad_heads(ml_norm_w)[:, None, :]
    rtw_p = _pad_heads(ret_norm_w)[:, None, :]
    w_out_p = _layout_wout(w_out).astype(BF16)
    wr_hi, wr_lo = _split_bf16(_pad_last(w_router, LANES))
    br_p = _pad_last(b_router, LANES)[:, None, :]
    w1_r = w1.reshape(depth * N_EXPERTS, d, 2 * D_FF)
    w2_r = w2.reshape(depth * N_EXPERTS, D_FF, d)
    b1_r = b1.reshape(depth * N_EXPERTS, 1, 2 * D_FF)
    b2_r = b2.reshape(depth * N_EXPERTS, 1, d)

    stage1, stage2, chan = _dft_tables(seq)
    rc, rs1, rs2 = _rotary_tables(seq)
    ret_tiles, ret_rows = _retention_tables()

    xf, xb = _ln(x.reshape(t, d), emb_ln_g, emb_ln_b)
    for l in range(depth):
        pf, mqk, mv, mo, rqk, rv, rg, gates_t = _inproj(xb, w_in_p[l], b_in_p[l], wg_p[l], bg_p[l])
        zv, zw = _fourier_channels(pf, chan)
        yf = _fourier_sequence(stage1, stage2, zv, zw, batch, seq)
        mq, mkt, rq, rkt, gw = _mixer_prep(mqk, rqk, rc, rs1, rs2, conv_p[l], gates_t, batch, seq)
        hf, hb = _mlstm(mq, mkt, mv, gw, batch, seq)
        rf, rb = _retention(rq, rkt, rv, ret_tiles, ret_rows, batch, seq)
        x1, x1b, logits = _outproj(
            xf, yf, hf.reshape(t, HW), hb.reshape(t, HW), mo, rf.reshape(t, HW), rb.reshape(t, HW), rg,
            mlw_p[l], rtw_p[l], w_out_p[l], ln1_g[l][None, :], ln1_b[l][None, :],
            wr_hi[l], wr_lo[l], br_p[l], alpha)
        gates, pos, src_tok, block_e, n_blocks = _route(logits)
        xs = jnp.take(x1b, src_tok, axis=0, mode="clip")
        ys = _experts(block_e, n_blocks, xs, w1_r, b1_r, w2_r, b2_r, l)
        yk = jnp.take(ys, pos.T.reshape(TOP_K * t), axis=0, mode="clip").reshape(TOP_K, t, d // 2)
        xf, xb = _combine(x1, yk, gates, ln2_g[l], ln2_b[l], alpha)
    return xf.reshape(batch, seq, d)
```

```python
import functools

import jax
import jax.numpy as jnp
from jax import lax
from jax.experimental import pallas as pl
from jax.experimental.pallas import tpu as pltpu

F32 = jnp.float32
BF16 = jnp.bfloat16

D_MODEL = 1024
CHUNK = 128
FOURIER_W = D_MODEL // 4
N_FGROUPS = 4
FG_W = FOURIER_W // N_FGROUPS
ML_W = 3 * D_MODEL // 8
RET_W = D_MODEL - FOURIER_W - ML_W
HEADS = 4
DV = ML_W // HEADS
DK = DV // 2
CONV_W = 3
ROPE_BASE = 10000.0
RET_GAMMA_EXP0 = 5.0
RET_BWD_EXP_OFFSET = 0.5
N_EXPERTS = 32
TOP_K = 4
D_FF = D_MODEL
SWIGLU_ALPHA = 1.702
SWIGLU_LIMIT = 7.0
LN_EPS = 1e-5

COL_F = 0
COL_MQK = COL_F + FOURIER_W
COL_MV = COL_MQK + 2 * HEADS * DK
COL_MO = COL_MV + ML_W
COL_MG = COL_MO + ML_W
COL_RQ = COL_MG + 4 * HEADS
COL_RK = COL_RQ + HEADS * DK
COL_RV = COL_RK + HEADS * DK
COL_RG = COL_RV + RET_W
PROJ_W = COL_RG + RET_W

LANES = 128
SUBLANES = 8
DVP = LANES
QKP = 2 * LANES
HW = HEADS * DVP
DEN_COL = DV
GATE_ROWS = 4 * SUBLANES
VMEM_LIMIT = 52 * 1024 * 1024

PROJ_GROUPS = (FOURIER_W, 2 * QKP, HW, HW, 2 * QKP, HW, HW)
PROJ_DTYPES = (BF16, F32, BF16, BF16, F32, BF16, BF16)
MV_OFFSET = FOURIER_W + 2 * QKP

FFT_N1 = 128
ROW_TILE = 512
PROJ_TILE = 1024
MOE_BLOCK = 512
MIXER_BATCH = 4

def _cparams(sem):
    return pltpu.CompilerParams(dimension_semantics=sem, vmem_limit_bytes=VMEM_LIMIT)


def _pad_last(w, n):
    return jnp.pad(w, [(0, 0)] * (w.ndim - 1) + [(0, n - w.shape[-1])])


def _pad_heads(w):
    lead = w.shape[:-1]
    w = w.reshape(*lead, HEADS, DV)
    w = jnp.pad(w, [(0, 0)] * (len(lead) + 1) + [(0, DVP - DV)])
    return w.reshape(*lead, HW)


def _pair_split(w):
    lead = w.shape[:-1]
    w = w.reshape(*lead, HEADS, DK // 2, 2)
    w = jnp.swapaxes(w, -1, -2).reshape(*lead, HEADS * DK)
    return _pad_last(w, QKP)


def _layout_proj(w):
    hk = HEADS * DK
    return jnp.concatenate([
        w[..., COL_F:COL_MQK],
        _pad_last(w[..., COL_MQK:COL_MQK + hk], QKP), _pad_last(w[..., COL_MQK + hk:COL_MV], QKP),
        _pad_heads(w[..., COL_MV:COL_MO]), _pad_heads(w[..., COL_MO:COL_MG]),
        _pair_split(w[..., COL_RQ:COL_RK]), _pair_split(w[..., COL_RK:COL_RV]),
        _pad_heads(w[..., COL_RV:COL_RG]), _pad_heads(w[..., COL_RG:PROJ_W]),
    ], axis=-1)


def _layout_gates(w):
    g = w[..., COL_MG:COL_RQ]
    pick = lambda j: _pad_last(g[..., HEADS * j:HEADS * (j + 1)], SUBLANES)
    return jnp.concatenate([pick(1), pick(0), pick(3), pick(2)], axis=-1)


def _layout_wout(w):
    wt = jnp.swapaxes(w, -1, -2)
    wt = jnp.concatenate([wt[..., :FOURIER_W], _pad_heads(wt[..., FOURIER_W:FOURIER_W + ML_W]),
                          _pad_heads(wt[..., FOURIER_W + ML_W:])], axis=-1)
    return jnp.swapaxes(wt, -1, -2)


def _split_bf16(w):
    hi = w.astype(BF16)
    return hi, (w - hi.astype(F32)).astype(BF16)


def _dft_tables(seq):
    n1c, n2c, g = FFT_N1, seq // FFT_N1, SUBLANES
    eye = jnp.eye(g, dtype=F32)
    ar = lambda n: jnp.arange(n, dtype=jnp.int32)
    n1 = (ar(n1c // g)[:, None] * g + ar(g)[None, :])[:, None, None, :]
    kn = (ar(n2c)[None, :, None, None] * (n1 + n1c * ar(n2c)[None, None, :, None])) % seq
    ang = kn.astype(F32) * (2.0 * jnp.pi / seq)
    spread = lambda t: jnp.einsum('aknj,jl->akjnl', t, eye).reshape(n1c // g, n2c * g, n2c * g)
    gr, gi = spread(jnp.cos(ang) * n2c ** -0.5), spread(-jnp.sin(ang) * n2c ** -0.5)
    stage1 = jnp.concatenate([jnp.concatenate([gr, gi], axis=2),
                              jnp.concatenate([gi, -gr], axis=2)], axis=1).astype(BF16)
    phi = ((ar(n1c)[:, None] * ar(n1c)[None, :]) % n1c).astype(F32) * (2.0 * jnp.pi / n1c)
    spread2 = lambda t: jnp.einsum('kn,jl->kjln', t, eye).reshape(n1c * g, g * n1c)
    stage2 = jnp.concatenate([spread2(jnp.cos(phi) * n1c ** -0.5),
                              spread2(jnp.sin(phi) * n1c ** -0.5)], axis=1).astype(BF16)
    c = jnp.arange(FG_W, dtype=jnp.int32)
    cc = ((c[:, None] * c[None, :]) % FG_W).astype(F32) * (2.0 * jnp.pi / FG_W)
    eye = jnp.eye(N_FGROUPS, dtype=F32)
    bd_c = jnp.kron(eye, jnp.cos(cc) * FG_W ** -0.5)
    bd_s = jnp.kron(eye, jnp.sin(cc) * FG_W ** -0.5)
    chan = jnp.concatenate([bd_c, bd_s], axis=1).astype(BF16)
    return stage1, stage2, chan


def _rotary_tables(seq):
    inv = 1.0 / (ROPE_BASE ** (jnp.arange(0, DK, 2, dtype=F32) / DK))
    ang = jnp.arange(seq, dtype=F32)[:, None] * inv[None, :]
    cos, sin = jnp.cos(ang), jnp.sin(ang)
    zero = jnp.zeros_like(sin)
    heads = lambda a, b: _pad_last(jnp.tile(jnp.concatenate([a, b], axis=1), (1, HEADS)), QKP)
    return heads(cos, cos), heads(-sin, zero), heads(zero, sin)


def _retention_tables():
    idx = jnp.arange(CHUNK, dtype=F32)
    diff = idx[:, None] - idx[None, :]
    tiles, rows = [], []
    for rev in (False, True):
        offset = RET_BWD_EXP_OFFSET if rev else 0.0
        lg = jnp.log1p(-jnp.exp2(-(RET_GAMMA_EXP0 + offset) - jnp.arange(HEADS, dtype=F32)))
        lg3 = lg[:, None, None]
        if rev:
            decay = jnp.where((diff < 0)[None], jnp.exp(lg3 * jnp.maximum(-diff, 0.0)[None]), 0.0)
            w_inter = jnp.exp(lg[:, None] * (CHUNK - idx)[None, :])
            w_key = jnp.exp(lg[:, None] * idx[None, :])
        else:
            decay = jnp.where((diff >= 0)[None], jnp.exp(lg3 * jnp.maximum(diff, 0.0)[None]), 0.0)
            w_inter = jnp.exp(lg[:, None] * (idx + 1.0)[None, :])
            w_key = jnp.exp(lg[:, None] * (CHUNK - 1 - idx)[None, :])
        g_chunk = jnp.broadcast_to(jnp.exp(lg * CHUNK)[:, None], (HEADS, CHUNK))
        tiles.append(jnp.stack([decay, jnp.broadcast_to(w_inter[:, :, None], (HEADS, CHUNK, LANES))], axis=1))
        pad = lambda t: jnp.pad(t, ((0, SUBLANES - HEADS), (0, 0)))
        rows.append(jnp.stack([pad(w_key), pad(g_chunk)], axis=0))
    return jnp.stack(tiles, axis=0), jnp.stack(rows, axis=0)


def _layernorm_rows(z, g, b):
    mu = jnp.mean(z, axis=-1, keepdims=True)
    d = z - mu
    var = jnp.mean(d * d, axis=-1, keepdims=True)
    return d * lax.rsqrt(var + LN_EPS) * g + b


def _ln_kernel(x_ref, g_ref, b_ref, o_ref, ob_ref):
    y = _layernorm_rows(x_ref[...], g_ref[...], b_ref[...])
    o_ref[...] = y
    ob_ref[...] = y.astype(BF16)


def _ln(x, g, b):
    t, d = x.shape
    row = pl.BlockSpec((ROW_TILE, d), lambda i: (i, 0))
    vec = pl.BlockSpec((1, d), lambda i: (0, 0))
    return pl.pallas_call(
        _ln_kernel, grid=(t // ROW_TILE,), in_specs=[row, vec, vec], out_specs=[row, row],
        out_shape=[jax.ShapeDtypeStruct((t, d), F32), jax.ShapeDtypeStruct((t, d), BF16)],
        compiler_params=_cparams(("parallel",)), name="input_layernorm",
    )(x, g.reshape(1, d), b.reshape(1, d))


def _inproj_kernel(x_ref, w_ref, b_ref, wg_ref, bg_ref, *o_refs):
    x = x_ref[...]
    c0 = 0
    for o_ref in o_refs[:-1]:
        n = o_ref.shape[-1]
        y = jnp.dot(x, w_ref[:, c0:c0 + n], preferred_element_type=F32) + b_ref[:, c0:c0 + n]
        o_ref[...] = y.astype(o_ref.dtype)
        c0 += n
    gt = lax.dot_general(wg_ref[...], x, (((1,), (1,)), ((), ())), preferred_element_type=F32)
    o_refs[-1][...] = gt + bg_ref[...]


def _inproj(xb, w, b, wg, bg):
    t, d = xb.shape
    n_p = w.shape[1]
    const = lambda a: pl.BlockSpec(a.shape, lambda i: (0, 0))
    return pl.pallas_call(
        _inproj_kernel, grid=(t // PROJ_TILE,),
        in_specs=[pl.BlockSpec((PROJ_TILE, d), lambda i: (i, 0)), const(w), const(b), const(wg), const(bg)],
        out_specs=[pl.BlockSpec((PROJ_TILE, n), lambda i: (i, 0)) for n in PROJ_GROUPS]
        + [pl.BlockSpec((GATE_ROWS, PROJ_TILE), lambda i: (0, i))],
        out_shape=[jax.ShapeDtypeStruct((t, n), dt) for n, dt in zip(PROJ_GROUPS, PROJ_DTYPES)]
        + [jax.ShapeDtypeStruct((GATE_ROWS, t), F32)],
        compiler_params=_cparams(("parallel",)), name="input_projection",
    )(xb, w, b, wg, bg)


def _fchan_kernel(u_ref, m_ref, v_ref, w_ref):
    vw = jnp.dot(u_ref[...], m_ref[...], preferred_element_type=F32)
    v_ref[...] = vw[:, :FOURIER_W]
    w_ref[...] = vw[:, FOURIER_W:]


def _fourier_channels(u, chan):
    t = u.shape[0]
    row = pl.BlockSpec((ROW_TILE, FOURIER_W), lambda i: (i, 0))
    return pl.pallas_call(
        _fchan_kernel, grid=(t // ROW_TILE,),
        in_specs=[row, pl.BlockSpec((FOURIER_W, 2 * FOURIER_W), lambda i: (0, 0))],
        out_specs=[row, row], out_shape=[jax.ShapeDtypeStruct((t, FOURIER_W), F32)] * 2,
        compiler_params=_cparams(("parallel",)), name="fourier_channel_dft",
    )(u, chan)


def _fstage1_kernel(m_ref, v_ref, w_ref, br_ref, bi_ref):
    rows = v_ref.shape[1] * SUBLANES
    x = jnp.concatenate([v_ref[0].reshape(rows, FOURIER_W), w_ref[0].reshape(rows, FOURIER_W)], axis=0)
    y = jnp.dot(m_ref[0], x.astype(BF16), preferred_element_type=F32)
    br_ref[0] = y[:rows].reshape(br_ref.shape[1:])
    bi_ref[0] = y[rows:].reshape(bi_ref.shape[1:])


def _fstage2_kernel(m_ref, br_ref, bi_ref, o_ref):
    rows = br_ref.shape[2] * SUBLANES
    x = jnp.concatenate([br_ref[0].reshape(rows, FOURIER_W), bi_ref[0].reshape(rows, FOURIER_W)], axis=0)
    y = jnp.dot(m_ref[...], x.astype(BF16), preferred_element_type=F32)
    o_ref[0] = y.reshape(o_ref.shape[1:])


def _fourier_sequence(stage1, stage2, v, w, batch, seq):
    n1c, n2c, g = FFT_N1, seq // FFT_N1, SUBLANES
    view = lambda a: a.reshape(batch, n2c, n1c, FOURIER_W)
    blk1 = pl.BlockSpec((1, n2c, g, FOURIER_W), lambda a, b: (b, 0, a, 0))
    shape1 = jax.ShapeDtypeStruct((batch, n2c, n1c, FOURIER_W), F32)
    br, bi = pl.pallas_call(
        _fstage1_kernel, grid=(n1c // g, batch),
        in_specs=[pl.BlockSpec((1,) + stage1.shape[1:], lambda a, b: (a, 0, 0)), blk1, blk1],
        out_specs=[blk1, blk1], out_shape=[shape1, shape1],
        compiler_params=_cparams(("parallel", "parallel")), name="fourier_sequence_stage1",
    )(stage1, view(v), view(w))
    blk2 = pl.BlockSpec((1, g, n1c, FOURIER_W), lambda kb, b: (b, kb, 0, 0))
    y = pl.pallas_call(
        _fstage2_kernel, grid=(n2c // g, batch),
        in_specs=[pl.BlockSpec(stage2.shape, lambda kb, b: (0, 0)), blk2, blk2],
        out_specs=pl.BlockSpec((1, n1c, g, FOURIER_W), lambda kb, b: (b, 0, kb, 0)),
        out_shape=jax.ShapeDtypeStruct((batch, n1c, n2c, FOURIER_W), F32),
        compiler_params=_cparams(("parallel", "parallel")), name="fourier_sequence_stage2",
    )(stage2, br, bi)
    return y.reshape(batch * seq, FOURIER_W)


def _dot(a, b):
    return jnp.dot(a, b, preferred_element_type=F32)


def _cumsum_dot(tri, x, tri_left):
    hi = x.astype(BF16)
    r1 = x - hi.astype(F32)
    mid = r1.astype(BF16)
    lo = (r1 - mid.astype(F32)).astype(BF16)
    if tri_left:
        return _dot(tri, hi) + _dot(tri, mid) + _dot(tri, lo)
    return _dot(hi, tri) + _dot(mid, tri) + _dot(lo, tri)


def _log_sigmoid(x):
    return jnp.minimum(x, 0.0) - jnp.log1p(jnp.exp(-jnp.abs(x)))


def _mixer_prep_kernel(mqk_ref, prev_ref, next_ref, rqk_ref, rc_ref, rs1_ref, rs2_ref, convw_ref, g_ref,
                       mq_ref, mkt_ref, rq_ref, rkt_ref, gw_ref):
    i = pl.program_id(0)
    rows_n = mqk_ref.shape[1]
    chunks = rows_n // CHUNK
    n = CHUNK
    r2 = lax.broadcasted_iota(jnp.int32, (n, n), 0)
    c2 = lax.broadcasted_iota(jnp.int32, (n, n), 1)
    for c in range(chunks):
        lanes = slice(c * n, (c + 1) * n)
        groups = []
        for rev in (False, True):
            r0 = 2 * SUBLANES if rev else 0
            tri = jnp.where((c2 <= r2) if rev else (c2 >= r2), 1.0, 0.0).astype(BF16)
            a = _cumsum_dot(tri, _log_sigmoid(g_ref[r0:r0 + SUBLANES, lanes]), False)
            groups += [g_ref[r0 + SUBLANES:r0 + 2 * SUBLANES, lanes] - a, a]
        gw_ref[c] = jnp.concatenate(groups, axis=0)

    qk = mqk_ref[0]
    rows = lax.broadcasted_iota(jnp.int32, (rows_n, 2 * QKP), 0)
    prev_row = prev_ref[0, 0, SUBLANES - 1:SUBLANES, :] * jnp.where(i == 0, 0.0, 1.0)
    next_row = next_ref[0, 0, 0:1, :] * jnp.where(i == pl.num_programs(0) - 1, 0.0, 1.0)
    xm1 = jnp.where(rows == 0, prev_row, pltpu.roll(qk, 1, 0))
    xp1 = jnp.where(rows == rows_n - 1, next_row, pltpu.roll(qk, rows_n - 1, 0))
    conv = xm1 * convw_ref[0:1, :] + qk * convw_ref[1:2, :] + xp1 * convw_ref[2:3, :]
    act = conv * jax.nn.sigmoid(conv)
    mq_ref[0] = act[:, :QKP].astype(BF16)

    rc, rs1, rs2 = rc_ref[...], rs1_ref[...], rs2_ref[...]
    half = DK // 2

    def rotate(t):
        return t * rc + pltpu.roll(t, QKP - half, 1) * rs1 + pltpu.roll(t, half, 1) * rs2

    rqk = rqk_ref[0]
    rq_ref[0] = rotate(rqk[:, :QKP]).astype(BF16)
    mk = act[:, QKP:] * DK ** -0.5
    rk = rotate(rqk[:, QKP:]) * DK ** -0.5
    for c in range(chunks):
        mkt_ref[0, c] = mk[c * n:(c + 1) * n, :].T.astype(BF16)
        rkt_ref[0, c] = rk[c * n:(c + 1) * n, :].T.astype(BF16)


def _mixer_prep(mqk, rqk, rc, rs1, rs2, convw, gates, batch, seq):
    n = seq // CHUNK
    tiles = seq // ROW_TILE
    chunks = ROW_TILE // CHUNK
    g8 = ROW_TILE // SUBLANES
    mqk3 = mqk.reshape(batch, seq, 2 * QKP)
    mqk8 = mqk.reshape(batch, seq // SUBLANES, SUBLANES, 2 * QKP)
    rqk3 = rqk.reshape(batch, seq, 2 * QKP)
    tile = lambda w: pl.BlockSpec((1, ROW_TILE, w), lambda i, b: (b, i, 0))
    halo = lambda index: pl.BlockSpec((1, 1, SUBLANES, 2 * QKP), index)
    pos = pl.BlockSpec((ROW_TILE, QKP), lambda i, b: (i, 0))
    kt = pl.BlockSpec((1, chunks, QKP, CHUNK), lambda i, b: (b, i, 0, 0))
    q_shape = jax.ShapeDtypeStruct((batch, seq, QKP), BF16)
    kt_shape = jax.ShapeDtypeStruct((batch, n, QKP, CHUNK), BF16)
    return pl.pallas_call(
        _mixer_prep_kernel, grid=(tiles, batch),
        in_specs=[tile(2 * QKP),
                  halo(lambda i, b: (b, jnp.maximum(i * g8 - 1, 0), 0, 0)),
                  halo(lambda i, b: (b, jnp.minimum((i + 1) * g8, seq // SUBLANES - 1), 0, 0)),
                  tile(2 * QKP), pos, pos, pos,
                  pl.BlockSpec((CONV_W, 2 * QKP), lambda i, b: (0, 0)),
                  pl.BlockSpec((GATE_ROWS, ROW_TILE), lambda i, b: (0, b * tiles + i))],
        out_specs=[tile(QKP), kt, tile(QKP), kt,
                   pl.BlockSpec((chunks, GATE_ROWS, CHUNK), lambda i, b: (b * tiles + i, 0, 0))],
        out_shape=[q_shape, kt_shape, q_shape, kt_shape,
                   jax.ShapeDtypeStruct((batch * n, GATE_ROWS, CHUNK), F32)],
        compiler_params=_cparams(("parallel", "parallel")), name="mixer_qk_prep",
    )(mqk3, mqk8, mqk8, rqk3, rc, rs1, rs2, convw, gates)


def _head_key_blocks(kt):
    rows = lax.broadcasted_iota(jnp.int32, kt.shape, 0)
    zero = jnp.zeros_like(kt)
    return jnp.concatenate([jnp.where((rows >= DK * h) & (rows < DK * (h + 1)), kt, zero)
                            for h in range(HEADS)], axis=1)


def _mlstm_direction(rev, q, kt, v, gw, c_ref, cb_ref, m_ref, d):
    n = CHUNK
    r0 = 2 * SUBLANES if rev else 0
    key_w = gw[r0:r0 + SUBLANES, :]
    a = gw[r0 + SUBLANES:r0 + 2 * SUBLANES, :]
    r2 = lax.broadcasted_iota(jnp.int32, (n, n), 0)
    c2 = lax.broadcasted_iota(jnp.int32, (n, n), 1)
    causal = (c2 >= r2) if rev else (c2 <= r2)
    lane = lax.broadcasted_iota(jnp.int32, (n, LANES), 1)

    cmax = jnp.zeros((n, LANES), F32)
    for h in range(HEADS):
        cm_h = jnp.max(jnp.where(causal, key_w[h:h + 1, :], -jnp.inf), axis=1, keepdims=True)
        cmax = jnp.where(lane == DEN_COL + h, cm_h, cmax)
    zero = jnp.zeros((SUBLANES, n), F32)
    den_group = DEN_COL // SUBLANES
    a_col = jnp.concatenate([zero] * den_group + [a] + [zero] * (n // SUBLANES - den_group - 1), axis=0).T
    m_lane = m_ref[d, 1, 0:1, :]
    mm = jnp.maximum(m_lane, cmax)
    s_inter = jnp.exp(m_lane - mm)
    e_negm = jnp.exp(-a_col - mm)

    s_all = _dot(q, _head_key_blocks(kt))
    inter_all = _dot(q, cb_ref[d])
    sr = lax.broadcasted_iota(jnp.int32, (LANES, HW), 0)
    sc = lax.broadcasted_iota(jnp.int32, (LANES, HW), 1)
    spread = jnp.where(sr - DEN_COL == sc // DVP, 1.0, 0.0).astype(BF16)
    inter_all = _dot(s_inter.astype(BF16), spread) * inter_all
    rs = []
    for h in range(HEADS):
        blk = slice(DVP * h, DVP * (h + 1))
        col = slice(DEN_COL + h, DEN_COL + h + 1)
        p = jnp.exp(jnp.where(causal, key_w[h:h + 1, :] - mm[:, col], -jnp.inf))
        scores = (s_all[:, blk] * p).astype(BF16)
        rs.append(_dot(scores, v[:, blk]) + inter_all[:, blk])
    den = rs[0]
    for h in range(1, HEADS):
        den = den + rs[h]
    rden = 1.0 / jnp.maximum(jnp.abs(den), e_negm)
    out = jnp.concatenate(rs, axis=1) * _dot(rden.astype(BF16), spread)

    m_prev = m_ref[d, 0]
    a_end = jnp.broadcast_to(a[:, 0:1] if rev else a[:, n - 1:n], (SUBLANES, n))
    w_key = a_end + key_w
    m_chunk = jnp.broadcast_to(jnp.max(w_key, axis=1, keepdims=True), (SUBLANES, n))
    m_new = jnp.maximum(a_end + m_prev, m_chunk)
    s_prev = jnp.exp(a_end + m_prev - m_new)
    p_key = jnp.exp(w_key - m_chunk) * jnp.exp(m_chunk - m_new)
    m_ref[d, 0] = m_new
    row8 = lax.broadcasted_iota(jnp.int32, (SUBLANES, n), 0)
    lane8 = lax.broadcasted_iota(jnp.int32, (SUBLANES, n), 1)
    on_diag = (lane8 == row8 + DEN_COL) & (row8 < HEADS)
    m_ref[d, 1] = jnp.broadcast_to(jnp.sum(jnp.where(on_diag, m_new, 0.0), axis=0, keepdims=True),
                                   (SUBLANES, n))
    for h in range(HEADS):
        blk = slice(DVP * h, DVP * (h + 1))
        keys = slice(DK * h, DK * (h + 1))
        kp = (kt[keys, :].astype(F32) * p_key[h:h + 1, :]).astype(BF16)
        c_new = s_prev[h:h + 1, :] * c_ref[d, h] + _dot(kp, v[:, blk])
        c_ref[d, h] = c_new
        cb_ref[d, keys, blk] = c_new.astype(BF16)
    return out


def _mlstm_kernel(q_f, kt_f, v_f, g_f, q_b, kt_b, v_b, g_b, hf_ref, hb_ref, c_ref, cb_ref, m_ref):
    @pl.when(pl.program_id(1) == 0)
    def _():
        c_ref[...] = jnp.zeros_like(c_ref)
        cb_ref[...] = jnp.zeros_like(cb_ref)
        m_ref[...] = jnp.zeros_like(m_ref)

    for j in range(q_f.shape[0]):
        state = (c_ref.at[j], cb_ref.at[j], m_ref.at[j])
        hf_ref[j] = _mlstm_direction(False, q_f[j], kt_f[j, 0], v_f[j], g_f[j, 0], *state, 0).astype(hf_ref.dtype)
        hb_ref[j] = _mlstm_direction(True, q_b[j], kt_b[j, 0], v_b[j], g_b[j, 0], *state, 1).astype(hb_ref.dtype)


def _mixer_specs(batch, seq):
    n = seq // CHUNK
    bs = MIXER_BATCH if batch % MIXER_BATCH == 0 else 1
    fwd = lambda b, i: (b, i, 0)
    bwd = lambda b, i: (b, n - 1 - i, 0)
    blk = lambda w, im: pl.BlockSpec((bs, CHUNK, w), im)
    per_chunk = lambda rows, cols: (
        pl.BlockSpec((bs, 1, rows, cols), lambda b, i: (b, i, 0, 0)),
        pl.BlockSpec((bs, 1, rows, cols), lambda b, i: (b, n - 1 - i, 0, 0)))
    return n, bs, fwd, bwd, blk, per_chunk


def _mlstm(q, kt, v, gw, batch, seq):
    n, bs, fwd, bwd, blk, per_chunk = _mixer_specs(batch, seq)
    v3 = v.reshape(batch, seq, HW)
    gw4 = gw.reshape(batch, n, GATE_ROWS, CHUNK)
    ktf, ktb = per_chunk(QKP, CHUNK)
    gf, gb = per_chunk(GATE_ROWS, CHUNK)
    return pl.pallas_call(
        _mlstm_kernel, grid=(batch // bs, n),
        in_specs=[blk(QKP, fwd), ktf, blk(HW, fwd), gf, blk(QKP, bwd), ktb, blk(HW, bwd), gb],
        out_specs=[blk(HW, fwd), blk(HW, bwd)],
        out_shape=[jax.ShapeDtypeStruct((batch, seq, HW), BF16)] * 2,
        scratch_shapes=[pltpu.VMEM((bs, 2, HEADS, DK, DVP), F32), pltpu.VMEM((bs, 2, QKP, HW), BF16),
                        pltpu.VMEM((bs, 2, 2, SUBLANES, CHUNK), F32)],
        compiler_params=_cparams(("parallel", "arbitrary")), name="mlstm_mixer",
    )(q, kt, v3, gw4, q, kt, v3, gw4)


def _retention_direction(q, kt, v, tile_ref, row_ref, s_ref, sb_ref, d):
    s_all = _dot(q, _head_key_blocks(kt))
    inter_all = _dot(q, sb_ref[d])
    outs = []
    for h in range(HEADS):
        blk = slice(DVP * h, DVP * (h + 1))
        scores = (s_all[:, blk] * tile_ref[d, h, 0]).astype(BF16)
        outs.append(_dot(scores, v[:, blk]) + tile_ref[d, h, 1] * inter_all[:, blk])
    for h in range(HEADS):
        blk = slice(DVP * h, DVP * (h + 1))
        keys = slice(DK * h, DK * (h + 1))
        kp = (kt[keys, :].astype(F32) * row_ref[d, 0, h:h + 1, :]).astype(BF16)
        s_new = row_ref[d, 1, h:h + 1, :] * s_ref[d, h] + _dot(kp, v[:, blk])
        s_ref[d, h] = s_new
        sb_ref[d, keys, blk] = s_new.astype(BF16)
    return jnp.concatenate(outs, axis=1)


def _retention_kernel(q_f, kt_f, v_f, q_b, kt_b, v_b, tile_ref, row_ref, yf_ref, yb_ref, s_ref, sb_ref):
    @pl.when(pl.program_id(1) == 0)
    def _():
        s_ref[...] = jnp.zeros_like(s_ref)
        sb_ref[...] = jnp.zeros_like(sb_ref)

    for j in range(q_f.shape[0]):
        state = (s_ref.at[j], sb_ref.at[j])
        yf_ref[j] = _retention_direction(q_f[j], kt_f[j, 0], v_f[j], tile_ref, row_ref, *state, 0
                                         ).astype(yf_ref.dtype)
        yb_ref[j] = _retention_direction(q_b[j], kt_b[j, 0], v_b[j], tile_ref, row_ref, *state, 1
                                         ).astype(yb_ref.dtype)


def _retention(q, kt, v, tiles, rows, batch, seq):
    n, bs, fwd, bwd, blk, per_chunk = _mixer_specs(batch, seq)
    v3 = v.reshape(batch, seq, HW)
    ktf, ktb = per_chunk(QKP, CHUNK)
    return pl.pallas_call(
        _retention_kernel, grid=(batch // bs, n),
        in_specs=[blk(QKP, fwd), ktf, blk(HW, fwd), blk(QKP, bwd), ktb, blk(HW, bwd),
                  pl.BlockSpec(tiles.shape, lambda b, i: (0, 0, 0, 0, 0)),
                  pl.BlockSpec(rows.shape, lambda b, i: (0, 0, 0, 0))],
        out_specs=[blk(HW, fwd), blk(HW, bwd)],
        out_shape=[jax.ShapeDtypeStruct((batch, seq, HW), BF16)] * 2,
        scratch_shapes=[pltpu.VMEM((bs, 2, HEADS, DK, DVP), F32), pltpu.VMEM((bs, 2, QKP, HW), BF16)],
        compiler_params=_cparams(("parallel", "arbitrary")), name="retention_mixer",
    )(q, kt, v3, q, kt, v3, tiles, rows)


def _head_norm(z, w):
    valid = lax.broadcasted_iota(jnp.int32, (1, DVP), 1) < DV
    outs = []
    for h in range(HEADS):
        zh = z[:, DVP * h:DVP * (h + 1)]
        mu = jnp.sum(jnp.where(valid, zh, 0.0), axis=1, keepdims=True) * (1.0 / DV)
        d = jnp.where(valid, zh - mu, 0.0)
        var = jnp.sum(d * d, axis=1, keepdims=True) * (1.0 / DV)
        outs.append(d * lax.rsqrt(var + LN_EPS))
    return jnp.concatenate(outs, axis=1) * w


def _outproj_kernel(alpha, x_ref, yf_ref, hf_ref, hb_ref, o_ref, rf_ref, rb_ref, rg_ref, mlw_ref, rtw_ref,
                    w_ref, g_ref, b_ref, wrh_ref, wrl_ref, br_ref, x1_ref, x1b_ref, lg_ref):
    f32 = lambda ref: ref[...].astype(F32)
    y_m = _head_norm(jax.nn.sigmoid(f32(o_ref)) * (f32(hf_ref) + f32(hb_ref)), mlw_ref[...])
    rg = f32(rg_ref)
    y_r = rg * jax.nn.sigmoid(rg) * _head_norm(f32(rf_ref) + f32(rb_ref), rtw_ref[...])
    cat = jnp.concatenate([yf_ref[...], y_m, y_r], axis=1).astype(BF16)
    mix = _dot(cat, w_ref[...])
    x1 = _layernorm_rows(alpha * x_ref[...] + mix, g_ref[...], b_ref[...])
    x1_ref[...] = x1
    hi = x1.astype(BF16)
    x1b_ref[...] = hi
    lo = (x1 - hi.astype(F32)).astype(BF16)
    lg_ref[...] = _dot(hi, wrh_ref[...]) + _dot(lo, wrh_ref[...]) + _dot(hi, wrl_ref[...]) + br_ref[...]


def _outproj(x, yf, hf, hb, mo, rf, rb, rg, mlw, rtw, w, g, b, wrh, wrl, br, alpha):
    t, d = x.shape
    row = lambda n: pl.BlockSpec((ROW_TILE, n), lambda i: (i, 0))
    const = lambda a: pl.BlockSpec(a.shape, lambda i: (0, 0))
    args = (x, yf, hf, hb, mo, rf, rb, rg, mlw, rtw, w, g, b, wrh, wrl, br)
    in_specs = [row(d), row(FOURIER_W),
                row(HW), row(HW), row(HW), row(HW), row(HW), row(HW)] + [const(a) for a in args[8:]]
    return pl.pallas_call(
        functools.partial(_outproj_kernel, alpha), grid=(t // ROW_TILE,), in_specs=in_specs,
        out_specs=[row(d), row(d), row(LANES)],
        out_shape=[jax.ShapeDtypeStruct((t, d), F32), jax.ShapeDtypeStruct((t, d), BF16),
                   jax.ShapeDtypeStruct((t, LANES), F32)],
        compiler_params=_cparams(("parallel",)), name="mixer_output_projection",
    )(*args)


def _pack_halves(y):
    half = y.shape[1] // 2
    bits = lambda t: lax.bitcast_convert_type(t.astype(BF16).astype(F32), jnp.uint32)
    return (bits(y[:, :half]) >> 16) | (bits(y[:, half:]) & jnp.uint32(0xFFFF0000))


def _unpack_halves(w):
    lo = lax.bitcast_convert_type(w << 16, F32)
    hi = lax.bitcast_convert_type(w & jnp.uint32(0xFFFF0000), F32)
    return lo, hi


def _expert_kernel(be_ref, nb_ref, x_ref, w1_ref, b1_ref, w2_ref, b2_ref, o_ref, w1b_ref, w2b_ref):
    i = pl.program_id(0)

    @pl.when(jnp.logical_or(i == 0, be_ref[i] != be_ref[jnp.maximum(i - 1, 0)]))
    def _():
        w1b_ref[...] = w1_ref[0].astype(BF16)
        w2b_ref[...] = w2_ref[0].astype(BF16)

    @pl.when(i < nb_ref[0])
    def _():
        hc = _dot(x_ref[...], w1b_ref[...]) + b1_ref[0]
        gate = jnp.minimum(hc[:, :D_FF], SWIGLU_LIMIT)
        up = jnp.clip(hc[:, D_FF:], -SWIGLU_LIMIT, SWIGLU_LIMIT)
        glu = gate * jax.nn.sigmoid(SWIGLU_ALPHA * gate)
        y = _dot(((up + 1.0) * glu).astype(BF16), w2b_ref[...]) + b2_ref[0]
        o_ref[...] = _pack_halves(y)

    @pl.when(i >= nb_ref[0])
    def _():
        o_ref[...] = jnp.zeros_like(o_ref)


def _experts(block_e, n_blocks, xs, w1, b1, w2, b2, layer):
    p, d = xs.shape
    off = layer * N_EXPERTS
    wmap = lambda i, be, nb: (off + be[i], 0, 0)
    xmap = lambda i, be, nb: (jnp.minimum(i, nb[0] - 1), 0)
    grid_spec = pltpu.PrefetchScalarGridSpec(
        num_scalar_prefetch=2, grid=(p // MOE_BLOCK,),
        in_specs=[pl.BlockSpec((MOE_BLOCK, d), xmap),
                  pl.BlockSpec((1, d, 2 * D_FF), wmap), pl.BlockSpec((1, 1, 2 * D_FF), wmap),
                  pl.BlockSpec((1, D_FF, d), wmap), pl.BlockSpec((1, 1, d), wmap)],
        out_specs=pl.BlockSpec((MOE_BLOCK, d // 2), lambda i, be, nb: (i, 0)),
        scratch_shapes=[pltpu.VMEM((d, 2 * D_FF), BF16), pltpu.VMEM((D_FF, d), BF16)])
    return pl.pallas_call(
        _expert_kernel, grid_spec=grid_spec, out_shape=jax.ShapeDtypeStruct((p, d // 2), jnp.uint32),
        compiler_params=_cparams(("arbitrary",)), name="routed_experts",
    )(block_e, n_blocks, xs, w1, b1, w2, b2)


def _route_kernel(lg_ref, gate_ref, eid_ref, rank_ref, cnt_ref):
    i = pl.program_id(0)
    tm = lg_ref.shape[0]
    lane = lax.broadcasted_iota(jnp.int32, (tm, LANES), 1)
    x = jnp.where(lane < N_EXPERTS, lg_ref[...], -jnp.inf)
    vals, ids = [], []
    onehot = jnp.zeros((tm, LANES), F32)
    for _ in range(TOP_K):
        m = jnp.max(x, axis=1, keepdims=True)
        idx = jnp.min(jnp.where(x == m, lane, LANES), axis=1, keepdims=True)
        sel = lane == idx
        onehot = jnp.where(sel, 1.0, onehot)
        x = jnp.where(sel, -jnp.inf, x)
        vals.append(m)
        ids.append(idx)
    tile_counts = jnp.sum(onehot, axis=0, keepdims=True)

    @pl.when(i == 0)
    def _():
        cnt_ref[...] = jnp.zeros_like(cnt_ref)

    rr = lax.broadcasted_iota(jnp.int32, (tm, tm), 0)
    cc = lax.broadcasted_iota(jnp.int32, (tm, tm), 1)
    strict = jnp.where(cc < rr, 1.0, 0.0).astype(BF16)
    prior = _dot(strict, onehot.astype(BF16)) + cnt_ref[0:1, :]
    denom = jnp.ones_like(vals[0])
    for k in range(1, TOP_K):
        denom = denom + jnp.exp(vals[k] - vals[0])
    gates = jnp.zeros((tm, LANES), F32)
    eids = jnp.zeros((tm, LANES), jnp.int32)
    ranks = jnp.zeros((tm, LANES), jnp.int32)
    for k in range(TOP_K):
        rk = jnp.sum(jnp.where(lane == ids[k], prior, 0.0), axis=1, keepdims=True)
        gates = jnp.where(lane == k, jnp.exp(vals[k] - vals[0]) / denom, gates)
        eids = jnp.where(lane == k, ids[k], eids)
        ranks = jnp.where(lane == k, rk.astype(jnp.int32), ranks)
    gate_ref[...] = gates
    eid_ref[...] = eids
    rank_ref[...] = ranks
    cnt_ref[...] += tile_counts


def _route(logits):
    t = logits.shape[0]
    tile = pl.BlockSpec((ROW_TILE, LANES), lambda i: (i, 0))
    gates, eids, ranks, cnt = pl.pallas_call(
        _route_kernel, grid=(t // ROW_TILE,), in_specs=[tile],
        out_specs=[tile, tile, tile, pl.BlockSpec((SUBLANES, LANES), lambda i: (0, 0))],
        out_shape=[jax.ShapeDtypeStruct((t, LANES), F32), jax.ShapeDtypeStruct((t, LANES), jnp.int32),
                   jax.ShapeDtypeStruct((t, LANES), jnp.int32), jax.ShapeDtypeStruct((SUBLANES, LANES), F32)],
        compiler_params=_cparams(("arbitrary",)), name="router_topk",
    )(logits)
    n = t * TOP_K
    p = n + N_EXPERTS * MOE_BLOCK
    nb = p // MOE_BLOCK
    sizes = cnt[0, :N_EXPERTS].astype(jnp.int32)
    psizes = (sizes + MOE_BLOCK - 1) // MOE_BLOCK * MOE_BLOCK
    pends = jnp.cumsum(psizes)
    pstarts = pends - psizes
    starts = jnp.cumsum(sizes) - sizes
    block_e = jnp.minimum(jnp.searchsorted(pends, jnp.arange(nb, dtype=jnp.int32) * MOE_BLOCK, side='right'),
                          N_EXPERTS - 1).astype(jnp.int32)
    n_blocks = (pends[-1] // MOE_BLOCK).astype(jnp.int32).reshape(1)
    keys = eids[:, :TOP_K].reshape(n) * n + jnp.arange(n, dtype=jnp.int32)
    order = jnp.sort(keys) % n
    per_row = lambda per_block: jnp.repeat(per_block, MOE_BLOCK)
    local = jnp.arange(p, dtype=jnp.int32) - per_row(pstarts[block_e])
    size_r = per_row(sizes[block_e])
    pair = jnp.take(order, per_row(starts[block_e]) + local, mode="clip")
    src_tok = jnp.where(local < size_r, pair // TOP_K, jnp.arange(p, dtype=jnp.int32) % t)
    pos = jnp.take(pstarts, eids[:, :TOP_K], mode="clip") + ranks[:, :TOP_K]
    return gates, pos, src_tok, block_e, n_blocks


def _combine_kernel(alpha, x_ref, y_ref, gate_ref, g_ref, b_ref, o_ref, ob_ref):
    gates = gate_ref[...]
    lo = hi = None
    for k in range(TOP_K):
        lo_k, hi_k = _unpack_halves(y_ref[k])
        gk = gates[:, k:k + 1]
        lo = gk * lo_k if lo is None else lo + gk * lo_k
        hi = gk * hi_k if hi is None else hi + gk * hi_k
    moe = jnp.concatenate([lo, hi], axis=1)
    y = _layernorm_rows(alpha * x_ref[...] + moe, g_ref[...], b_ref[...])
    o_ref[...] = y
    ob_ref[...] = y.astype(BF16)


def _combine(x, yk, gates, g, b, alpha):
    t, d = x.shape
    row = pl.BlockSpec((ROW_TILE, d), lambda i: (i, 0))
    vec = pl.BlockSpec((1, d), lambda i: (0, 0))
    return pl.pallas_call(
        functools.partial(_combine_kernel, alpha), grid=(t // ROW_TILE,),
        in_specs=[row, pl.BlockSpec((TOP_K, ROW_TILE, d // 2), lambda i: (0, i, 0)),
                  pl.BlockSpec((ROW_TILE, LANES), lambda i: (i, 0)), vec, vec],
        out_specs=[row, row],
        out_shape=[jax.ShapeDtypeStruct((t, d), F32), jax.ShapeDtypeStruct((t, d), BF16)],
        compiler_params=_cparams(("parallel",)), name="expert_combine_layernorm",
    )(x, yk, gates, g.reshape(1, d), b.reshape(1, d))


def kernel(x, emb_ln_g, emb_ln_b, w_in, b_in, conv_w, ml_norm_w, ret_norm_w, w_out, ln1_g, ln1_b,
           w_router, b_router, w1, b1, w2, b2, ln2_g, ln2_b):
    batch, seq, d = x.shape
    depth = w_in.shape[0]
    assert d == D_MODEL and seq % PROJ_TILE == 0 and PROJ_TILE % ROW_TILE == 0
    assert (seq // FFT_N1) % SUBLANES == 0
    t = batch * seq
    alpha = (2.0 * depth) ** 0.25

    w_in_p = _layout_proj(w_in).astype(BF16)
    den_cols = jnp.array([MV_OFFSET + DVP * h + DEN_COL + h for h in range(HEADS)])
    b_in_p = _layout_proj(b_in).at[:, den_cols].set(1.0)[:, None, :]
    wg_p = jnp.swapaxes(_layout_gates(w_in), -1, -2).astype(BF16)
    bg_p = _layout_gates(b_in)[:, :, None]
    hk = HEADS * DK
    conv_p = jnp.concatenate([_pad_last(conv_w[..., :hk], QKP), _pad_last(conv_w[..., hk:], QKP)], axis=-1)
    mlw_p = _pad_heads(ml_norm_w)[:, None, :]
    rtw_p = _pad_heads(ret_norm_w)[:, None, :]
    w_out_p = _layout_wout(w_out).astype(BF16)
    wr_hi, wr_lo = _split_bf16(_pad_last(w_router, LANES))
    br_p = _pad_last(b_router, LANES)[:, None, :]
    w1_r = w1.reshape(depth * N_EXPERTS, d, 2 * D_FF)
    w2_r = w2.reshape(depth * N_EXPERTS, D_FF, d)
    b1_r = b1.reshape(depth * N_EXPERTS, 1, 2 * D_FF)
    b2_r = b2.reshape(depth * N_EXPERTS, 1, d)

    stage1, stage2, chan = _dft_tables(seq)
    rc, rs1, rs2 = _rotary_tables(seq)
    ret_tiles, ret_rows = _retention_tables()

    xf, xb = _ln(x.reshape(t, d), emb_ln_g, emb_ln_b)
    for l in range(depth):
        pf, mqk, mv, mo, rqk, rv, rg, gates_t = _inproj(xb, w_in_p[l], b_in_p[l], wg_p[l], bg_p[l])
        zv, zw = _fourier_channels(pf, chan)
        yf = _fourier_sequence(stage1, stage2, zv, zw, batch, seq)
        mq, mkt, rq, rkt, gw = _mixer_prep(mqk, rqk, rc, rs1, rs2, conv_p[l], gates_t, batch, seq)
        hf, hb = _mlstm(mq, mkt, mv, gw, batch, seq)
        rf, rb = _retention(rq, rkt, rv, ret_tiles, ret_rows, batch, seq)
        x1, x1b, logits = _outproj(
            xf, yf, hf.reshape(t, HW), hb.reshape(t, HW), mo, rf.reshape(t, HW), rb.reshape(t, HW), rg,
            mlw_p[l], rtw_p[l], w_out_p[l], ln1_g[l][None, :], ln1_b[l][None, :],
            wr_hi[l], wr_lo[l], br_p[l], alpha)
        gates, pos, src_tok, block_e, n_blocks = _route(logits)
        xs = jnp.take(x1b, src_tok, axis=0, mode="clip")
        ys = _experts(block_e, n_blocks, xs, w1_r, b1_r, w2_r, b2_r, l)
        yk = jnp.take(ys, pos.T.reshape(TOP_K * t), axis=0, mode="clip").reshape(TOP_K, t, d // 2)
        xf, xb = _combine(x1, yk, gates, ln2_g[l], ln2_b[l], alpha)
    return xf.reshape(batch, seq, d)
```
